```python
import math
import jax, jax.numpy as jnp
from jax import lax
import numpy as np

D_MODEL = 1024
BATCH = 4
SEQ = 4096
DEPTH = 2
DEC_BATCH = 128
DEC_SEQ = 8
PAST_LEN = 16384
PAGE_SIZE = 128

BRANCH_W = D_MODEL // 2
N_BRANCH = 3
RET_HEADS = 4
RET_DV = BRANCH_W // RET_HEADS
RET_DK = RET_DV // 2
RET_W = RET_HEADS * RET_DV
RET_CHUNK = 128
ATT_HEADS = 8
ATT_KV_HEADS = 2
ATT_DH = BRANCH_W // ATT_HEADS
WINDOW = 128
CONV_DIM = BRANCH_W
CONV_W = 3
D_FF = 4 * D_MODEL
D_PLE = 256
EPS = 1e-6

IN_SPLITS = (RET_HEADS * RET_DK, RET_HEADS * RET_DK, RET_W, RET_W,
             ATT_HEADS * ATT_DH, ATT_KV_HEADS * ATT_DH, ATT_KV_HEADS * ATT_DH,
             CONV_DIM, CONV_DIM, CONV_DIM, N_BRANCH * D_MODEL)
N_IN = sum(IN_SPLITS)

kernel_name = 'hybrid_retention_swa_shortconv_decoder_step'


def _rmsnorm(x, g):
    xf = x.astype(jnp.float32)
    y = xf * lax.rsqrt(jnp.mean(xf * xf, axis=-1, keepdims=True) + EPS)
    return (y * g.astype(jnp.float32)).astype(x.dtype)


def _split_in(z):
    offs = [int(o) for o in np.cumsum(IN_SPLITS)[:-1]]
    return jnp.split(z, offs, axis=-1)


def _retention_scan(q, k, v, s0):
    b, h, t, _ = q.shape
    dv = v.shape[-1]
    c = math.gcd(t, RET_CHUNK)
    n = t // c
    lg = jnp.log1p(-jnp.exp2(-5.0 - jnp.arange(RET_HEADS, dtype=jnp.float32)))
    idx = jnp.arange(c, dtype=jnp.float32)
    diff = idx[:, None] - idx[None, :]
    dmask = jnp.where(diff >= 0, jnp.exp(lg[:, None, None] * jnp.maximum(diff, 0.0)), 0.0)
    q_dec = jnp.exp(lg[:, None] * (idx + 1.0))[:, :, None]
    k_dec = jnp.exp(lg[:, None] * (c - 1.0 - idx))[:, :, None]
    c_dec = jnp.exp(lg * c)[:, None, None]

    def chunks(a):
        return a.astype(jnp.float32).reshape(b, h, n, c, a.shape[-1]).transpose(2, 0, 1, 3, 4)

    def step(s, inp):
        qi, ki, vi = inp
        inner = jnp.einsum('bhid,bhjd->bhij', qi, ki) * dmask
        o = jnp.einsum('bhij,bhje->bhie', inner, vi) + jnp.einsum('bhid,bhde->bhie', qi * q_dec, s)
        s = s * c_dec + jnp.einsum('bhjd,bhje->bhde', ki * k_dec, vi)
        return s, o

    s, o = lax.scan(step, s0.astype(jnp.float32), (chunks(q), chunks(k), chunks(v)))
    return o.transpose(1, 2, 0, 3, 4).reshape(b, h, t, dv), s


def _retention_branch(rq, rk, rv, rg, s0):
    b, t, _ = rq.shape

    def heads(a, d):
        return a.reshape(b, t, RET_HEADS, d).transpose(0, 2, 1, 3)

    o, s = _retention_scan(heads(rq, RET_DK), heads(rk, RET_DK) * (RET_DK ** -0.5), heads(rv, RET_DV), s0)
    o = o * lax.rsqrt(jnp.mean(o * o, axis=-1, keepdims=True) + EPS)
    o = o.transpose(0, 2, 1, 3).reshape(b, t, RET_W).astype(rq.dtype)
    return jax.nn.silu(rg) * o, s


def _sink_window_attend(q, k, v, qpos, kpos, sinks):
    g = ATT_HEADS // ATT_KV_HEADS
    s = jnp.einsum('...qkgd,...skd->...kgqs', q.astype(jnp.float32), k.astype(jnp.float32)) * (ATT_DH ** -0.5)
    dist = qpos[..., :, None] - kpos[..., None, :]
    allowed = (dist >= 0) & (dist < WINDOW) & (kpos[..., None, :] >= 0)
    slopes = jnp.exp2(-8.0 * (jnp.arange(ATT_HEADS, dtype=jnp.float32) + 1.0) / ATT_HEADS).reshape(ATT_KV_HEADS, g)
    s = s - slopes[:, :, None, None] * dist[..., None, None, :, :].astype(jnp.float32)
    s = jnp.where(allowed[..., None, None, :, :], s, -jnp.inf)
    sink = sinks.astype(jnp.float32).reshape(ATT_KV_HEADS, g)[:, :, None, None]
    m = jnp.maximum(jnp.max(s, axis=-1, keepdims=True), sink)
    pr = jnp.exp(s - m)
    pr = pr / (jnp.sum(pr, axis=-1, keepdims=True) + jnp.exp(sink - m))
    return jnp.einsum('...kgqs,...skd->...qkgd', pr, v.astype(jnp.float32))


def _attn_prompt(aq, ak, av, sinks):
    b, t, _ = aq.shape
    g = ATT_HEADS // ATT_KV_HEADS
    blk = WINDOW
    nb = t // blk
    qb = aq.reshape(b, nb, blk, ATT_KV_HEADS, g, ATT_DH)
    k = ak.reshape(b, t, ATT_KV_HEADS, ATT_DH)
    v = av.reshape(b, t, ATT_KV_HEADS, ATT_DH)

    def band(a):
        ap = jnp.pad(a, ((0, 0), (blk, 0), (0, 0), (0, 0)))
        prev = ap[:, :t].reshape(b, nb, blk, ATT_KV_HEADS, ATT_DH)
        return jnp.concatenate([prev, a.reshape(b, nb, blk, ATT_KV_HEADS, ATT_DH)], axis=2)

    pos = jnp.arange(t, dtype=jnp.int32).reshape(nb, blk)
    kpos = jnp.concatenate([pos - blk, pos], axis=1)
    o = _sink_window_attend(qb, band(k), band(v), pos, kpos, sinks)
    w = min(WINDOW, t)
    return o.reshape(b, t, ATT_HEADS * ATT_DH).astype(aq.dtype), k[:, t - w:], v[:, t - w:]


def _attn_sample(aq, ak, av, ck, cv, sinks):
    b, t, _ = aq.shape
    g = ATT_HEADS // ATT_KV_HEADS
    w = ck.shape[1]
    k = jnp.concatenate([ck.astype(ak.dtype), ak.reshape(b, t, ATT_KV_HEADS, ATT_DH)], axis=1)
    v = jnp.concatenate([cv.astype(av.dtype), av.reshape(b, t, ATT_KV_HEADS, ATT_DH)], axis=1)
    qpos = PAST_LEN + jnp.arange(t, dtype=jnp.int32)
    kpos = PAST_LEN - w + jnp.arange(w + t, dtype=jnp.int32)
    o = _sink_window_attend(aq.reshape(b, t, ATT_KV_HEADS, g, ATT_DH), k, v, qpos, kpos, sinks)
    return o.reshape(b, t, ATT_HEADS * ATT_DH).astype(aq.dtype), k[:, t:], v[:, t:]


def _short_conv_branch(cb, cc, ch, buf, conv_w):
    t = cb.shape[1]
    u = cc * ch
    up = jnp.concatenate([buf.astype(u.dtype), u], axis=1)
    y = up[:, 0:t] * conv_w[0]
    for j in range(1, CONV_W):
        y = y + up[:, j:j + t] * conv_w[j]
    return cb * y, up[:, t:]


def _layer(x, p_l, ret_s0, win_k, win_v, conv_buf,
           g_mix_pre, w_in, conv_w, sinks, w_branch, w_out, g_mix_post,
           g_ffn_pre, w_ff1, w_ff2, g_ffn_post, g_ple, w_ple_gate, w_ple_proj):
    b, t, d = x.shape
    h = _rmsnorm(x, g_mix_pre)
    rq, rk, rv, rg, aq, ak, av, cb, cc, ch, gates = _split_in(h @ w_in)
    o_ret, ret_new = _retention_branch(rq, rk, rv, rg, ret_s0)
    if win_k is None:
        o_att, wk_new, wv_new = _attn_prompt(aq, ak, av, sinks)
    else:
        o_att, wk_new, wv_new = _attn_sample(aq, ak, av, win_k, win_v, sinks)
    o_conv, conv_new = _short_conv_branch(cb, cc, ch, conv_buf, conv_w)
    branches = jnp.stack([o_ret, o_att, o_conv], axis=2)
    proj = jnp.einsum('btnw,nwd->btnd', branches, w_branch)
    gate = jax.nn.sigmoid(gates.reshape(b, t, N_BRANCH, d))
    mixed = jnp.sum(gate * proj, axis=2) @ w_out
    x = x + _rmsnorm(mixed, g_mix_post)
    f = jnp.square(jax.nn.relu(_rmsnorm(x, g_ffn_pre) @ w_ff1)) @ w_ff2
    x = x + _rmsnorm(f, g_ffn_post)
    x = x + jax.nn.sigmoid(_rmsnorm(x, g_ple) @ w_ple_gate) * (p_l @ w_ple_proj)
    return x, (ret_new, wk_new, wv_new, conv_new)


def setup_inputs(seed: int = 0) -> dict:
    key = jax.random.key(seed)
    ks = jax.random.split(key, 24)
    f32 = jnp.float32
    win = min(WINDOW, PAST_LEN)

    def nrm(k, shape, scale):
        return jax.random.normal(k, shape, f32) * scale

    def gain(k):
        return 1.0 + 0.05 * jax.random.normal(k, (DEPTH, D_MODEL), f32)

    return {
        'x_prompt': nrm(ks[0], (BATCH, SEQ, D_MODEL), 1.0),
        'x_sample': nrm(ks[1], (DEC_BATCH, DEC_SEQ, D_MODEL), 1.0),
        'p_prompt': nrm(ks[2], (DEPTH, BATCH, SEQ, D_PLE), 1.0),
        'p_sample': nrm(ks[3], (DEPTH, DEC_BATCH, DEC_SEQ, D_PLE), 1.0),
        'state_ret': nrm(ks[4], (DEPTH, DEC_BATCH, RET_HEADS, RET_DK, RET_DV), 1.0),
        'cache_win_k': nrm(ks[5], (DEPTH, DEC_BATCH, win, ATT_KV_HEADS, ATT_DH), 1.0),
        'cache_win_v': nrm(ks[6], (DEPTH, DEC_BATCH, win, ATT_KV_HEADS, ATT_DH), 1.0),
        'state_conv': nrm(ks[7], (DEPTH, DEC_BATCH, CONV_W - 1, CONV_DIM), 1.0),
        'g_mix_pre': gain(ks[8]),
        'w_in': nrm(ks[9], (DEPTH, D_MODEL, N_IN), D_MODEL ** -0.5),
        'conv_w': nrm(ks[10], (DEPTH, CONV_W, CONV_DIM), CONV_W ** -0.5),
        'attn_sinks': nrm(ks[11], (DEPTH, ATT_HEADS), 0.5),
        'w_branch': nrm(ks[12], (DEPTH, N_BRANCH, BRANCH_W, D_MODEL), BRANCH_W ** -0.5),
        'w_out': nrm(ks[13], (DEPTH, D_MODEL, D_MODEL), D_MODEL ** -0.5),
        'g_mix_post': gain(ks[14]),
        'g_ffn_pre': gain(ks[15]),
        'w_ff1': nrm(ks[16], (DEPTH, D_MODEL, D_FF), D_MODEL ** -0.5),
        'w_ff2': nrm(ks[17], (DEPTH, D_FF, D_MODEL), D_FF ** -0.5),
        'g_ffn_post': gain(ks[18]),
        'g_ple': gain(ks[19]),
        'w_ple_gate': nrm(ks[20], (DEPTH, D_MODEL, D_MODEL), D_MODEL ** -0.5),
        'w_ple_proj': nrm(ks[21], (DEPTH, D_PLE, D_MODEL), D_PLE ** -0.5),
    }


def reference(x_prompt, x_sample, p_prompt, p_sample, state_ret, cache_win_k, cache_win_v, state_conv,
              g_mix_pre, w_in, conv_w, attn_sinks, w_branch, w_out, g_mix_post,
              g_ffn_pre, w_ff1, w_ff2, g_ffn_post, g_ple, w_ple_gate, w_ple_proj):
    yp, ys = x_prompt, x_sample
    bp = x_prompt.shape[0]
    rp, kp, vp, cp = [], [], [], []
    rs, kss, vs, cs = [], [], [], []
    for l in range(DEPTH):
        wl = (g_mix_pre[l], w_in[l], conv_w[l], attn_sinks[l], w_branch[l], w_out[l], g_mix_post[l],
              g_ffn_pre[l], w_ff1[l], w_ff2[l], g_ffn_post[l], g_ple[l], w_ple_gate[l], w_ple_proj[l])
        yp, (r, k, v, c) = _layer(yp, p_prompt[l],
                                  jnp.zeros((bp, RET_HEADS, RET_DK, RET_DV), jnp.float32),
                                  None, None,
                                  jnp.zeros((bp, CONV_W - 1, CONV_DIM), yp.dtype), *wl)
        rp.append(r); kp.append(k); vp.append(v); cp.append(c)
        ys, (r, k, v, c) = _layer(ys, p_sample[l], state_ret[l], cache_win_k[l], cache_win_v[l],
                                  state_conv[l], *wl)
        rs.append(r); kss.append(k); vs.append(v); cs.append(c)
    return (yp, ys,
            jnp.stack(rp), jnp.stack(kp), jnp.stack(vp), jnp.stack(cp),
            jnp.stack(rs), jnp.stack(kss), jnp.stack(vs), jnp.stack(cs))
```

```python
import functools
import math

import numpy as np
import jax
import jax.numpy as jnp
from jax import lax
from jax.experimental import pallas as pl
from jax.experimental.pallas import tpu as pltpu

F32 = jnp.float32
BF16 = jnp.bfloat16

D_MODEL = 1024
BRANCH_W = 512
N_BRANCH = 3
RET_HEADS = 4
RET_DK = 64
RET_DV = 128
ATT_HEADS = 8
ATT_KV_HEADS = 2
ATT_GROUP = ATT_HEADS // ATT_KV_HEADS
ATT_DH = 64
WINDOW = 128
CONV_DIM = 512
CONV_W = 3
D_FF = 4096
D_PLE = 256
EPS = 1e-6
PAST_LEN = 16384

OFF_RET = 0
OFF_ATT = 1536
OFF_CONV = 2304
OFF_GATE = 3840
N_IN = 6912

BLK = 128
NEG = -1e30
VMEM_LIMIT = 56 * 1024 * 1024

PROMPT_TILE = 512
FFN_TILE = 512
FFN_CHUNK = 1024
SAMPLE_SEQS = 16


def _dot(a, b):
    return jnp.dot(a, b, preferred_element_type=F32)


def _dot_nt(a, b):
    return lax.dot_general(a, b, (((1,), (1,)), ((), ())), preferred_element_type=F32)


def _rms(x, g):
    return x * lax.rsqrt(jnp.mean(x * x, axis=-1, keepdims=True) + EPS) * g


def _const_spec(shape):
    nd = len(shape)
    return pl.BlockSpec(shape, lambda *_: (0,) * nd, pipeline_mode=pl.Buffered(1))


def _log_gamma():
    return np.log1p(-np.exp2(-5.0 - np.arange(RET_HEADS, dtype=np.float64)))


def _retention_consts(c):
    lg = _log_gamma()
    idx = np.arange(c, dtype=np.float64)
    diff = idx[:, None] - idx[None, :]
    dmask = np.where(diff >= 0, np.exp(lg[:, None, None] * np.maximum(diff, 0.0)), 0.0) * RET_DK ** -0.5
    qdec = np.repeat(np.exp(lg[:, None] * (idx + 1.0)).T, RET_DK, axis=1)
    kdec = np.repeat(np.exp(lg[:, None] * (c - 1.0 - idx)).T, RET_DK, axis=1) * RET_DK ** -0.5
    cdec = np.broadcast_to(np.repeat(np.exp(lg * c), RET_DK)[:, None], (RET_HEADS * RET_DK, RET_DV))
    return dmask, qdec, kdec, cdec


def _slopes():
    return np.exp2(-8.0 * (np.arange(ATT_HEADS, dtype=np.float64) + 1.0) / ATT_HEADS)


def _prompt_attn_bias():
    q = np.arange(BLK)[:, None]
    s = np.arange(2 * BLK)[None, :]
    dist = q + BLK - s
    allowed = (dist >= 0) & (dist < WINDOW)
    sl = _slopes().reshape(ATT_KV_HEADS, ATT_GROUP)
    b = np.where(allowed[None, None], -sl[:, :, None, None] * dist[None, None], NEG)
    return b.transpose(0, 2, 1, 3).reshape(ATT_KV_HEADS, BLK, ATT_GROUP * 2 * BLK)


def _sample_attn_bias(t, w):
    i = np.arange(t)[:, None]
    s = np.arange(2 * BLK)[None, :]
    dist = i + w - s
    allowed = (dist >= 0) & (dist < WINDOW) & (s < w + t)
    sl = _slopes()
    b = np.where(allowed[None], -sl[:, None, None] * dist[None], NEG)
    return b.reshape(ATT_HEADS * t, 2 * BLK)


def _ffn_body(x_ref, p_ref, gpre_ref, w1_ref, w2_ref, gpost_ref, gple_ref, wg_ref, wp_ref, o_ref):
    x = x_ref[...]
    h = _rms(x, gpre_ref[...]).astype(BF16)
    acc = None
    for c in range(D_FF // FFN_CHUNK):
        a = _dot(h, w1_ref[:, c * FFN_CHUNK:(c + 1) * FFN_CHUNK])
        a = jnp.square(jnp.maximum(a, 0.0)).astype(BF16)
        d = _dot(a, w2_ref[c * FFN_CHUNK:(c + 1) * FFN_CHUNK, :])
        acc = d if acc is None else acc + d
    x = x + _rms(acc, gpost_ref[...])
    gate = jax.nn.sigmoid(_dot(_rms(x, gple_ref[...]).astype(BF16), wg_ref[...]))
    o_ref[...] = x + gate * _dot(p_ref[...].astype(BF16), wp_ref[...])


def _ffn_call(x, p, gpre, w1, w2, gpost, gple, wg, wp):
    n = x.shape[0]
    tile = min(FFN_TILE, n)
    row = lambda i: (i, 0)
    return pl.pallas_call(
        _ffn_body,
        grid=(n // tile,),
        in_specs=[
            pl.BlockSpec((tile, D_MODEL), row),
            pl.BlockSpec((tile, D_PLE), row),
            _const_spec((1, D_MODEL)),
            _const_spec((D_MODEL, D_FF)),
            _const_spec((D_FF, D_MODEL)),
            _const_spec((1, D_MODEL)),
            _const_spec((1, D_MODEL)),
            _const_spec((D_MODEL, D_MODEL)),
            _const_spec((D_PLE, D_MODEL)),
        ],
        out_specs=pl.BlockSpec((tile, D_MODEL), row),
        out_shape=jax.ShapeDtypeStruct((n, D_MODEL), F32),
        compiler_params=pltpu.CompilerParams(
            dimension_semantics=("arbitrary",), vmem_limit_bytes=VMEM_LIMIT),
        name="ffn",
    )(x, p, gpre, w1, w2, gpost, gple, wg, wp)


def _half_swap_select(a, want_low):
    rolled = pltpu.roll(a, 64, 1)
    low = lax.broadcasted_iota(jnp.int32, (1, 128), 1) < 64
    return jnp.where(low, a, rolled) if want_low else jnp.where(low, rolled, a)


def _merge_and_residual(x, mixed, wout_ref, gpost_ref):
    y = _dot(mixed.astype(BF16), wout_ref[...])
    return x + _rms(y, gpost_ref[...])


def _gated_proj(h, win_ref, n, br, wbr_ref):
    gate = jax.nn.sigmoid(_dot(h, win_ref[:, OFF_GATE + n * D_MODEL:OFF_GATE + (n + 1) * D_MODEL]))
    return gate * _dot(br, wbr_ref[n])


def _mix_prompt_body(sink_ref, x_ref, gpre_ref, win_ref, convw_ref, bias_ref, dmask_ref, qdec_ref, kdec_ref,
                     cdec_ref, wbr_ref, wout_ref, gpost_ref,
                     o_ref, ret_ref, wk_ref, wv_ref, conv_ref,
                     br_scr, u_scr):
    tt = x_ref.shape[1]
    nblk = tt // BLK
    first = pl.program_id(1) == 0

    @pl.when(first)
    def _init():
        ret_ref[...] = jnp.zeros_like(ret_ref)
        wk_ref[...] = jnp.zeros_like(wk_ref)
        wv_ref[...] = jnp.zeros_like(wv_ref)
        u_scr[0:8, :] = jnp.zeros((8, CONV_DIM), F32)

    x = x_ref[0]
    h = _rms(x, gpre_ref[...]).astype(BF16)
    lane256 = lax.broadcasted_iota(jnp.int32, (1, 256), 1)
    zero_bf = jnp.zeros((), BF16)

    zr = _dot(h, win_ref[:, OFF_RET:OFF_RET + 1536])
    for c in range(nblk):
        r = slice(c * BLK, (c + 1) * BLK)
        q = zr[r, 0:256]
        k = zr[r, 256:512]
        qb = q.astype(BF16)
        kb = k.astype(BF16)
        qd = (q * qdec_ref[...]).astype(BF16)
        kdt = (k * kdec_ref[...]).T.astype(BF16)
        s_old = ret_ref[0]
        s_bf = s_old.astype(BF16)
        for hd in range(RET_HEADS):
            hm = (lane256 // RET_DK) == hd
            hr = slice(hd * RET_DK, (hd + 1) * RET_DK)
            inner = _dot_nt(jnp.where(hm, qb, zero_bf), kb) * dmask_ref[hd]
            vh = zr[r, 512 + hd * RET_DV:512 + (hd + 1) * RET_DV].astype(BF16)
            o = _dot(inner.astype(BF16), vh) + _dot(jnp.where(hm, qd, zero_bf), s_bf)
            ret_ref[0, hr, :] = s_old[hr] * cdec_ref[hr, :] + _dot(kdt[hr], vh)
            o = o * lax.rsqrt(jnp.mean(o * o, axis=-1, keepdims=True) + EPS)
            g = zr[r, 1024 + hd * RET_DV:1024 + (hd + 1) * RET_DV]
            br_scr[r, hd * RET_DV:(hd + 1) * RET_DV] = (g * jax.nn.sigmoid(g) * o).astype(BF16)
    mixed = _gated_proj(h, win_ref, 0, br_scr[...], wbr_ref)

    za = _dot(h, win_ref[:, OFF_ATT:OFF_ATT + 768])
    lane1024 = lax.broadcasted_iota(jnp.int32, (1, ATT_GROUP * 2 * BLK), 1)
    no_prev = jnp.where((lane1024 % (2 * BLK)) < BLK, jnp.where(first, NEG, 0.0), 0.0).astype(F32)
    seg = lane256 // ATT_DH
    kprev = wk_ref[0]
    vprev = wv_ref[0]
    for i in range(nblk):
        r = slice(i * BLK, (i + 1) * BLK)
        aq = (za[r, 0:512] * (ATT_DH ** -0.5)).astype(BF16)
        ak = za[r, 512:640]
        av = za[r, 640:768]
        k2 = jnp.concatenate([kprev, ak], axis=0)
        v2 = jnp.concatenate([vprev, av], axis=0)
        for j in range(ATT_KV_HEADS):
            kk = _half_swap_select(k2, j == 0).astype(BF16)
            vv = _half_swap_select(v2, j == 0).astype(BF16)
            kk = jnp.concatenate([kk, kk], axis=1)
            vv = jnp.concatenate([vv, vv], axis=1)
            kbd = jnp.concatenate([jnp.where(seg == g, kk, zero_bf) for g in range(ATT_GROUP)], axis=0)
            vbd = jnp.concatenate([jnp.where(seg == g, vv, zero_bf) for g in range(ATT_GROUP)], axis=0)
            s = _dot_nt(aq[:, j * 256:(j + 1) * 256], kbd) + bias_ref[j]
            if i == 0:
                s = s + no_prev
            ps = []
            for g in range(ATT_GROUP):
                sg = s[:, g * 256:(g + 1) * 256]
                snk = sink_ref[j * ATT_GROUP + g]
                m = jnp.maximum(jnp.max(sg, axis=-1, keepdims=True), snk)
                p = jnp.exp(sg - m)
                den = jnp.sum(p, axis=-1, keepdims=True) + jnp.exp(snk - m)
                ps.append((p / den).astype(BF16))
            o = _dot(jnp.concatenate(ps, axis=1), vbd)
            br_scr[r, j * 256:(j + 1) * 256] = o.astype(BF16)
        kprev, vprev = ak, av
    wk_ref[0] = kprev
    wv_ref[0] = vprev
    mixed = mixed + _gated_proj(h, win_ref, 1, br_scr[...], wbr_ref)

    zc = _dot(h, win_ref[:, OFF_CONV:OFF_CONV + 1536])
    u = zc[:, 512:1024] * zc[:, 1024:1536]
    u_scr[8:8 + tt, :] = u
    y = (u_scr[6:6 + tt, :] * convw_ref[0:1, :] + u_scr[7:7 + tt, :] * convw_ref[1:2, :]
         + u * convw_ref[2:3, :])
    br_scr[...] = (zc[:, 0:512] * y).astype(BF16)
    tail = u[tt - 2:tt]
    conv_ref[0] = tail
    u_scr[6:8, :] = tail
    mixed = mixed + _gated_proj(h, win_ref, 2, br_scr[...], wbr_ref)

    o_ref[0] = _merge_and_residual(x, mixed, wout_ref, gpost_ref)


def _mix_prompt_call(x, sinks, gpre, win, convw, wbr, wout, gpost):
    b, t, _ = x.shape
    tt = min(PROMPT_TILE, t)
    dmask, qdec, kdec, cdec = (jnp.asarray(a, F32) for a in _retention_consts(BLK))
    bias = jnp.asarray(_prompt_attn_bias(), F32)
    per_b = lambda bi, ti: (bi, 0, 0)
    return pl.pallas_call(
        _mix_prompt_body,
        grid=(b, t // tt),
        in_specs=[
            pl.BlockSpec(memory_space=pltpu.SMEM),
            pl.BlockSpec((1, tt, D_MODEL), lambda bi, ti: (bi, ti, 0)),
            _const_spec((1, D_MODEL)),
            _const_spec((D_MODEL, N_IN)),
            _const_spec((CONV_W, CONV_DIM)),
            _const_spec(bias.shape),
            _const_spec(dmask.shape),
            _const_spec(qdec.shape),
            _const_spec(kdec.shape),
            _const_spec(cdec.shape),
            _const_spec((N_BRANCH, BRANCH_W, D_MODEL)),
            _const_spec((D_MODEL, D_MODEL)),
            _const_spec((1, D_MODEL)),
        ],
        out_specs=[
            pl.BlockSpec((1, tt, D_MODEL), lambda bi, ti: (bi, ti, 0)),
            pl.BlockSpec((1, RET_HEADS * RET_DK, RET_DV), per_b),
            pl.BlockSpec((1, WINDOW, 128), per_b),
            pl.BlockSpec((1, WINDOW, 128), per_b),
            pl.BlockSpec((1, CONV_W - 1, CONV_DIM), per_b),
        ],
        out_shape=[
            jax.ShapeDtypeStruct((b, t, D_MODEL), F32),
            jax.ShapeDtypeStruct((b, RET_HEADS * RET_DK, RET_DV), F32),
            jax.ShapeDtypeStruct((b, WINDOW, 128), F32),
            jax.ShapeDtypeStruct((b, WINDOW, 128), F32),
            jax.ShapeDtypeStruct((b, CONV_W - 1, CONV_DIM), F32),
        ],
        scratch_shapes=[
            pltpu.VMEM((tt, BRANCH_W), BF16),
            pltpu.VMEM((tt + 8, CONV_DIM), F32),
        ],
        compiler_params=pltpu.CompilerParams(
            dimension_semantics=("arbitrary", "arbitrary"), vmem_limit_bytes=VMEM_LIMIT),
        name="mix_prompt",
    )(sinks, x, gpre, win, convw, bias, dmask, qdec, kdec, cdec, wbr, wout, gpost)


def _mix_sample_body(x_ref, sret_ref, ck_ref, cv_ref, sconv_ref, gpre_ref, win_ref, convw_ref,
                     bias_ref, sinkrow_ref, dmask_ref, qdec_ref, kdec_ref, cdec_ref, wbr_ref, wout_ref, gpost_ref,
                     o_ref, ret_ref, wk_ref, wv_ref, conv_ref,
                     z_scr, oc_scr, att_scr, cv_scr, br_scr, u_scr):
    rows = x_ref.shape[0]
    t = rows // ret_ref.shape[0]
    nseq = ret_ref.shape[0]
    w = ck_ref.shape[1]

    x = x_ref[...]
    h = _rms(x, gpre_ref[...]).astype(BF16)
    lane256 = lax.broadcasted_iota(jnp.int32, (1, 256), 1)
    lane128 = lax.broadcasted_iota(jnp.int32, (1, 128), 1)
    zero_bf = jnp.zeros((), BF16)

    z_scr[...] = _dot(h, win_ref[:, 0:OFF_GATE])

    def per_seq(b, carry):
        r0 = pl.multiple_of(b * t, t)
        rs = pl.ds(r0, t)
        q = z_scr[rs, 0:256]
        k = z_scr[rs, 256:512]
        v = z_scr[rs, 512:1024]
        qd = q * qdec_ref[...]
        qm = jnp.concatenate(
            [jnp.where((lane256 // RET_DK) == hd, qd, 0.0) for hd in range(RET_HEADS)], axis=0).astype(BF16)
        s_old = sret_ref[b]
        oc = _dot(qm, s_old.astype(BF16))
        for hd in range(RET_HEADS):
            oc_scr[hd, rs, :] = oc[hd * t:(hd + 1) * t]
        kdt = (k * kdec_ref[...]).T.astype(BF16)
        vb = v.astype(BF16)
        for hd in range(RET_HEADS):
            hr = slice(hd * RET_DK, (hd + 1) * RET_DK)
            ret_ref[b, hr, :] = (s_old[hr] * cdec_ref[hr, :]
                                 + _dot(kdt[hr], vb[:, hd * RET_DV:(hd + 1) * RET_DV]))
        aq = z_scr[rs, 1536:2048] * (ATT_DH ** -0.5)
        nk = z_scr[rs, 2048:2176]
        nv = z_scr[rs, 2176:2304]
        pieces = []
        for hd in range(ATT_HEADS):
            piece = aq[:, (hd // 2) * 128:(hd // 2 + 1) * 128]
            if hd % 2 != hd // ATT_GROUP:
                piece = pltpu.roll(piece, 64, 1)
            pieces.append(jnp.where((lane128 // ATT_DH) == hd // ATT_GROUP, piece, 0.0))
        qs = jnp.concatenate(pieces, axis=0).astype(BF16)
        pad = jnp.zeros((2 * BLK - w - t, 128), F32)
        ck = ck_ref[b]
        cv = cv_ref[b]
        k2 = jnp.concatenate([ck, nk, pad], axis=0).astype(BF16)
        v2 = jnp.concatenate([cv, nv, pad], axis=0).astype(BF16)
        s = _dot_nt(qs, k2) + bias_ref[...]
        snk = sinkrow_ref[:, 0:1]
        m = jnp.maximum(jnp.max(s, axis=-1, keepdims=True), snk)
        p = jnp.exp(s - m)
        den = jnp.sum(p, axis=-1, keepdims=True) + jnp.exp(snk - m)
        o = _dot((p / den).astype(BF16), v2)
        for pr in range(ATT_HEADS // 2):
            lo = o[(2 * pr) * t:(2 * pr + 1) * t]
            hi = o[(2 * pr + 1) * t:(2 * pr + 2) * t]
            if (2 * pr) // ATT_GROUP == 1:
                lo = pltpu.roll(lo, 64, 1)
            if (2 * pr + 1) // ATT_GROUP == 0:
                hi = pltpu.roll(hi, 64, 1)
            att_scr[rs, pr * 128:(pr + 1) * 128] = jnp.where(lane128 < ATT_DH, lo, hi)
        wk_ref[b, 0:w - t, :] = ck[t:w]
        wk_ref[b, w - t:w, :] = nk
        wv_ref[b, 0:w - t, :] = cv[t:w]
        wv_ref[b, w - t:w, :] = nv
        u = z_scr[rs, 2816:3328] * z_scr[rs, 3328:3840]
        u_scr[6:8, :] = sconv_ref[b]
        u_scr[8:8 + t, :] = u
        y = (u_scr[6:6 + t, :] * convw_ref[0:1, :] + u_scr[7:7 + t, :] * convw_ref[1:2, :]
             + u * convw_ref[2:3, :])
        cv_scr[rs, :] = z_scr[rs, 2304:2816] * y
        conv_ref[b] = u_scr[6 + t:8 + t, :]
        return carry

    lax.fori_loop(0, nseq, per_seq, 0)

    qb = z_scr[:, 0:256].astype(BF16)
    kb = z_scr[:, 256:512].astype(BF16)
    for hd in range(RET_HEADS):
        hm = (lane256 // RET_DK) == hd
        inner = _dot_nt(jnp.where(hm, qb, zero_bf), kb) * dmask_ref[hd]
        vh = z_scr[:, 512 + hd * RET_DV:512 + (hd + 1) * RET_DV].astype(BF16)
        o = _dot(inner.astype(BF16), vh) + oc_scr[hd]
        o = o * lax.rsqrt(jnp.mean(o * o, axis=-1, keepdims=True) + EPS)
        g = z_scr[:, 1024 + hd * RET_DV:1024 + (hd + 1) * RET_DV]
        br_scr[:, hd * RET_DV:(hd + 1) * RET_DV] = (g * jax.nn.sigmoid(g) * o).astype(BF16)
    mixed = _gated_proj(h, win_ref, 0, br_scr[...], wbr_ref)
    mixed = mixed + _gated_proj(h, win_ref, 1, att_scr[...].astype(BF16), wbr_ref)
    mixed = mixed + _gated_proj(h, win_ref, 2, cv_scr[...].astype(BF16), wbr_ref)
    o_ref[...] = _merge_and_residual(x, mixed, wout_ref, gpost_ref)


def _mix_sample_call(x, sinks, sret, ck, cv, sconv, gpre, win, convw, wbr, wout, gpost):
    nb, t, _ = x.shape
    w = ck.shape[1]
    sb = min(SAMPLE_SEQS, nb)
    rows = sb * t
    dm, qd, kd, cdec = _retention_consts(t)
    same_seq = np.kron(np.eye(sb), np.ones((t, t)))
    dmask = jnp.asarray(np.tile(dm, (1, sb, sb)) * same_seq[None], F32)
    qdec = jnp.asarray(qd, F32)
    kdec = jnp.asarray(kd, F32)
    cdec = jnp.asarray(cdec, F32)
    bias = jnp.asarray(_sample_attn_bias(t, w), F32)
    sinkrow = jnp.broadcast_to(jnp.repeat(sinks, t)[:, None], (ATT_HEADS * t, 128))
    x2 = x.reshape(nb * t, D_MODEL)
    seq3 = lambda i: (i, 0, 0)
    row = lambda i: (i, 0)
    outs = pl.pallas_call(
        _mix_sample_body,
        grid=(nb // sb,),
        in_specs=[
            pl.BlockSpec((rows, D_MODEL), row),
            pl.BlockSpec((sb, RET_HEADS * RET_DK, RET_DV), seq3),
            pl.BlockSpec((sb, w, 128), seq3),
            pl.BlockSpec((sb, w, 128), seq3),
            pl.BlockSpec((sb, CONV_W - 1, CONV_DIM), seq3),
            _const_spec((1, D_MODEL)),
            _const_spec((D_MODEL, N_IN)),
            _const_spec((CONV_W, CONV_DIM)),
            _const_spec(bias.shape),
            _const_spec(sinkrow.shape),
            _const_spec(dmask.shape),
            _const_spec(qdec.shape),
            _const_spec(kdec.shape),
            _const_spec(cdec.shape),
            _const_spec((N_BRANCH, BRANCH_W, D_MODEL)),
            _const_spec((D_MODEL, D_MODEL)),
            _const_spec((1, D_MODEL)),
        ],
        out_specs=[
            pl.BlockSpec((rows, D_MODEL), row),
            pl.BlockSpec((sb, RET_HEADS * RET_DK, RET_DV), seq3),
            pl.BlockSpec((sb, w, 128), seq3),
            pl.BlockSpec((sb, w, 128), seq3),
            pl.BlockSpec((sb, CONV_W - 1, CONV_DIM), seq3),
        ],
        out_shape=[
            jax.ShapeDtypeStruct((nb * t, D_MODEL), F32),
            jax.ShapeDtypeStruct((nb, RET_HEADS * RET_DK, RET_DV), F32),
            jax.ShapeDtypeStruct((nb, w, 128), F32),
            jax.ShapeDtypeStruct((nb, w, 128), F32),
            jax.ShapeDtypeStruct((nb, CONV_W - 1, CONV_DIM), F32),
        ],
        scratch_shapes=[
            pltpu.VMEM((rows, OFF_GATE), F32),
            pltpu.VMEM((RET_HEADS, rows, RET_DV), F32),
            pltpu.VMEM((rows, BRANCH_W), F32),
            pltpu.VMEM((rows, BRANCH_W), F32),
            pltpu.VMEM((rows, BRANCH_W), BF16),
            pltpu.VMEM((8 + t, CONV_DIM), F32),
        ],
        compiler_params=pltpu.CompilerParams(
            dimension_semantics=("arbitrary",), vmem_limit_bytes=VMEM_LIMIT),
        name="mix_sample",
    )(x2, sret, ck, cv, sconv, gpre, win, convw, bias, sinkrow, dmask, qdec, kdec, cdec, wbr, wout, gpost)
    return outs


def kernel(x_prompt, x_sample, p_prompt, p_sample, state_ret, cache_win_k, cache_win_v, state_conv,
           g_mix_pre, w_in, conv_w, attn_sinks, w_branch, w_out, g_mix_post,
           g_ffn_pre, w_ff1, w_ff2, g_ffn_post, g_ple, w_ple_gate, w_ple_proj):
    depth = w_in.shape[0]
    bp, tp, d = x_prompt.shape
    bs, ts, _ = x_sample.shape
    win_w = cache_win_k.shape[2]
    assert d == D_MODEL and w_in.shape[2] == N_IN and tp % BLK == 0 and math.gcd(ts, BLK) == ts
    assert win_w == WINDOW and win_w + ts <= 2 * BLK

    yp = x_prompt
    ys = x_sample.reshape(bs * ts, d)
    outs_p, outs_s = [], []
    for l in range(depth):
        g = lambda a: a[l].reshape(1, D_MODEL)
        win = w_in[l].astype(BF16)
        wbr = w_branch[l].astype(BF16)
        wout = w_out[l].astype(BF16)
        w1 = w_ff1[l].astype(BF16)
        w2 = w_ff2[l].astype(BF16)
        wg = w_ple_gate[l].astype(BF16)
        wp = w_ple_proj[l].astype(BF16)
        ffn = functools.partial(_ffn_call, gpre=g(g_ffn_pre), w1=w1, w2=w2, gpost=g(g_ffn_post),
                                gple=g(g_ple), wg=wg, wp=wp)

        yp, r, k, v, c = _mix_prompt_call(yp, attn_sinks[l], g(g_mix_pre), win, conv_w[l], wbr, wout,
                                          g(g_mix_post))
        yp = ffn(yp.reshape(bp * tp, d), p_prompt[l].reshape(bp * tp, D_PLE)).reshape(bp, tp, d)
        outs_p.append((r, k, v, c))

        ys, r, k, v, c = _mix_sample_call(
            ys.reshape(bs, ts, d), attn_sinks[l],
            state_ret[l].reshape(bs, RET_HEADS * RET_DK, RET_DV),
            cache_win_k[l].reshape(bs, win_w, 128), cache_win_v[l].reshape(bs, win_w, 128),
            state_conv[l], g(g_mix_pre), win, conv_w[l], wbr, wout, g(g_mix_post))
        ys = ffn(ys, p_sample[l].reshape(bs * ts, D_PLE))
        outs_s.append((r, k, v, c))

    def stack(outs, i, shape):
        return jnp.stack([o[i] for o in outs]).reshape(shape)

    return (yp, ys.reshape(bs, ts, d),
            stack(outs_p, 0, (depth, bp, RET_HEADS, RET_DK, RET_DV)),
            stack(outs_p, 1, (depth, bp, WINDOW, ATT_KV_HEADS, ATT_DH)),
            stack(outs_p, 2, (depth, bp, WINDOW, ATT_KV_HEADS, ATT_DH)),
            stack(outs_p, 3, (depth, bp, CONV_W - 1, CONV_DIM)),
            stack(outs_s, 0, (depth, bs, RET_HEADS, RET_DK, RET_DV)),
            stack(outs_s, 1, (depth, bs, win_w, ATT_KV_HEADS, ATT_DH)),
            stack(outs_s, 2, (depth, bs, win_w, ATT_KV_HEADS, ATT_DH)),
            stack(outs_s, 3, (depth, bs, CONV_W - 1, CONV_DIM)))
```

```python
import functools
import math

import numpy as np
import jax
import jax.numpy as jnp
from jax import lax
from jax.experimental import pallas as pl
from jax.experimental.pallas import tpu as pltpu

F32 = jnp.float32
BF16 = jnp.bfloat16

D_MODEL = 1024
BRANCH_W = 512
N_BRANCH = 3
RET_HEADS = 4
RET_DK = 64
RET_DV = 128
ATT_HEADS = 8
ATT_KV_HEADS = 2
ATT_GROUP = ATT_HEADS // ATT_KV_HEADS
ATT_DH = 64
WINDOW = 128
CONV_DIM = 512
CONV_W = 3
D_FF = 4096
D_PLE = 256
EPS = 1e-6
PAST_LEN = 16384

OFF_RET = 0
OFF_ATT = 1536
OFF_CONV = 2304
OFF_GATE = 3840
N_IN = 6912

BLK = 128
NEG = -1e30
VMEM_LIMIT = 56 * 1024 * 1024

PROMPT_TILE = 512
ROW_SPLIT = 4
FFN_TILE = 512
FFN_CHUNK = 1024
SAMPLE_SEQS = 16


def _dot(a, b):
    return jnp.dot(a, b, preferred_element_type=F32)


def _dot_nt(a, b):
    return lax.dot_general(a, b, (((1,), (1,)), ((), ())), preferred_element_type=F32)


def _rms(x, g):
    return x * lax.rsqrt(jnp.mean(x * x, axis=-1, keepdims=True) + EPS) * g


def _const_spec(shape):
    nd = len(shape)
    return pl.BlockSpec(shape, lambda *_: (0,) * nd, pipeline_mode=pl.Buffered(1))


def _log_gamma():
    return np.log1p(-np.exp2(-5.0 - np.arange(RET_HEADS, dtype=np.float64)))


def _retention_consts(c):
    lg = _log_gamma()
    idx = np.arange(c, dtype=np.float64)
    diff = idx[:, None] - idx[None, :]
    dmask = np.where(diff >= 0, np.exp(lg[:, None, None] * np.maximum(diff, 0.0)), 0.0) * RET_DK ** -0.5
    qdec = np.repeat(np.exp(lg[:, None] * (idx + 1.0)).T, RET_DK, axis=1)
    kdec = np.repeat(np.exp(lg[:, None] * (c - 1.0 - idx)).T, RET_DK, axis=1) * RET_DK ** -0.5
    cdec = np.broadcast_to(np.repeat(np.exp(lg * c), RET_DK)[:, None], (RET_HEADS * RET_DK, RET_DV))
    return dmask, qdec, kdec, cdec


def _slopes():
    return np.exp2(-8.0 * (np.arange(ATT_HEADS, dtype=np.float64) + 1.0) / ATT_HEADS)


def _prompt_attn_bias():
    s = np.arange(2 * BLK)[:, None]
    q = np.arange(BLK)[None, :]
    dist = q + BLK - s
    allowed = (dist >= 0) & (dist < WINDOW)
    per_head = np.where(allowed[None], -_slopes()[:, None, None] * dist[None], NEG)
    out = [np.concatenate([per_head[ATT_GROUP * j + par], per_head[ATT_GROUP * j + par + 2]], axis=1)
           for j in range(ATT_KV_HEADS) for par in range(2)]
    return np.stack(out)


def _sample_attn_bias(t, w):
    i = np.arange(t)[:, None]
    s = np.arange(2 * BLK)[None, :]
    dist = i + w - s
    allowed = (dist >= 0) & (dist < WINDOW) & (s < w + t)
    sl = _slopes()
    b = np.where(allowed[None], -sl[:, None, None] * dist[None], NEG)
    return b.reshape(ATT_HEADS * t, 2 * BLK)


def _ffn_body(x_ref, p_ref, gpre_ref, w1_ref, w2_ref, gpost_ref, gple_ref, wg_ref, wp_ref, o_ref):
    x = x_ref[...]
    h = _rms(x, gpre_ref[...]).astype(BF16)
    acc = None
    for c in range(D_FF // FFN_CHUNK):
        a = _dot(h, w1_ref[:, c * FFN_CHUNK:(c + 1) * FFN_CHUNK])
        a = jnp.square(jnp.maximum(a, 0.0)).astype(BF16)
        d = _dot(a, w2_ref[c * FFN_CHUNK:(c + 1) * FFN_CHUNK, :])
        acc = d if acc is None else acc + d
    x = x + _rms(acc, gpost_ref[...])
    gate = jax.nn.sigmoid(_dot(_rms(x, gple_ref[...]).astype(BF16), wg_ref[...]))
    o_ref[...] = x + gate * _dot(p_ref[...].astype(BF16), wp_ref[...])


def _ffn_call(x, p, gpre, w1, w2, gpost, gple, wg, wp):
    n = x.shape[0]
    tile = min(FFN_TILE, n)
    row = lambda i: (i, 0)
    return pl.pallas_call(
        _ffn_body,
        grid=(n // tile,),
        in_specs=[
            pl.BlockSpec((tile, D_MODEL), row),
            pl.BlockSpec((tile, D_PLE), row),
            _const_spec((1, D_MODEL)),
            _const_spec((D_MODEL, D_FF)),
            _const_spec((D_FF, D_MODEL)),
            _const_spec((1, D_MODEL)),
            _const_spec((1, D_MODEL)),
            _const_spec((D_MODEL, D_MODEL)),
            _const_spec((D_PLE, D_MODEL)),
        ],
        out_specs=pl.BlockSpec((tile, D_MODEL), row),
        out_shape=jax.ShapeDtypeStruct((n, D_MODEL), F32),
        compiler_params=pltpu.CompilerParams(
            dimension_semantics=("arbitrary",), vmem_limit_bytes=VMEM_LIMIT),
        name="ffn",
    )(x, p, gpre, w1, w2, gpost, gple, wg, wp)


def _merge_and_residual(x, mixed, wout_ref, gpost_ref):
    y = _dot(mixed.astype(BF16), wout_ref[...])
    return x + _rms(y, gpost_ref[...])


def _gated_proj(h, win_ref, n, br, wbr_ref):
    gate = jax.nn.sigmoid(_dot(h, win_ref[:, OFF_GATE + n * D_MODEL:OFF_GATE + (n + 1) * D_MODEL]))
    return gate * _dot(br, wbr_ref[n])


def _mix_prompt_body(sink_ref, x_ref, gpre_ref, win_ref, convw_ref, bias_ref, dmask_ref, qdec_ref, kdec_ref,
                     cdec_ref, wbr_ref, wout_ref, gpost_ref,
                     o_ref, ret_ref, wk_ref, wv_ref, conv_ref,
                     br_scr, att_scr, u_scr):
    tt = x_ref.shape[1]
    nblk = tt // BLK
    first = pl.program_id(1) == 0

    @pl.when(first)
    def _init():
        ret_ref[...] = jnp.zeros_like(ret_ref)
        wk_ref[...] = jnp.zeros_like(wk_ref)
        wv_ref[...] = jnp.zeros_like(wv_ref)
        u_scr[0:8, :] = jnp.zeros((8, CONV_DIM), F32)

    lane256 = lax.broadcasted_iota(jnp.int32, (1, 256), 1)
    zero_bf = jnp.zeros((), BF16)

    rows = [slice(i * tt // ROW_SPLIT, (i + 1) * tt // ROW_SPLIT) for i in range(ROW_SPLIT)]
    h_parts = [_rms(x_ref[0, rw, :], gpre_ref[...]).astype(BF16) for rw in rows]
    zr = jnp.concatenate([_dot(hp, win_ref[:, OFF_RET:OFF_RET + 1536]) for hp in h_parts], axis=0)
    h = jnp.concatenate(h_parts, axis=0)

    head_masks = [(lane256 // RET_DK) == hd for hd in range(RET_HEADS)]

    def head_stack(a):
        return jnp.concatenate([jnp.where(hm, a, zero_bf) for hm in head_masks], axis=0)

    def inner_scores(c):
        r = slice(c * BLK, (c + 1) * BLK)
        return _dot_nt(head_stack(zr[r, 0:256].astype(BF16)), zr[r, 256:512].astype(BF16))

    gate_w = D_MODEL // nblk
    gate1_off = OFF_GATE + D_MODEL
    gate1_pre = []
    inner_next = inner_scores(0)
    for c in range(nblk):
        r = slice(c * BLK, (c + 1) * BLK)
        inner_raw = inner_next
        if c + 1 < nblk:
            inner_next = inner_scores(c + 1)
        gate1_pre.append(_dot(h, win_ref[:, gate1_off + c * gate_w:gate1_off + (c + 1) * gate_w]))
        qd = (zr[r, 0:256] * qdec_ref[...]).astype(BF16)
        kdt = (zr[r, 256:512] * kdec_ref[...]).T.astype(BF16)
        s_old = ret_ref[0]
        cross = _dot(head_stack(qd), s_old.astype(BF16))
        inner = (inner_raw * dmask_ref[...]).astype(BF16)
        for hd in range(RET_HEADS):
            hr = slice(hd * RET_DK, (hd + 1) * RET_DK)
            hq = slice(hd * BLK, (hd + 1) * BLK)
            vh = zr[r, 512 + hd * RET_DV:512 + (hd + 1) * RET_DV].astype(BF16)
            o = _dot(inner[hq], vh) + cross[hq]
            ret_ref[0, hr, :] = s_old[hr] * cdec_ref[hr, :] + _dot(kdt[hr], vh)
            o = o * lax.rsqrt(jnp.mean(o * o, axis=-1, keepdims=True) + EPS)
            g = zr[r, 1024 + hd * RET_DV:1024 + (hd + 1) * RET_DV]
            br_scr[r, hd * RET_DV:(hd + 1) * RET_DV] = (g * jax.nn.sigmoid(g) * o).astype(BF16)
    proj0 = _dot(br_scr[...], wbr_ref[0])

    za = _dot(h, win_ref[:, OFF_ATT:OFF_ATT + 768])
    low = lax.broadcasted_iota(jnp.int32, (1, 128), 1) < ATT_DH
    key_row = lax.broadcasted_iota(jnp.int32, (2 * BLK, 1), 0)
    no_prev = jnp.where(key_row < BLK, jnp.where(first, NEG, 0.0), 0.0).astype(F32)
    sink_rows = [jnp.where(lane256 < BLK, sink_ref[ATT_GROUP * j + par], sink_ref[ATT_GROUP * j + par + 2])
                 for j in range(ATT_KV_HEADS) for par in range(2)]

    def key_variants(kblk):
        rolled = pltpu.roll(kblk, ATT_DH, 1)
        return [jnp.where(low, kblk, 0.0).astype(BF16), jnp.where(low, 0.0, rolled).astype(BF16),
                jnp.where(low, rolled, 0.0).astype(BF16), jnp.where(low, 0.0, kblk).astype(BF16)]

    def scores(i, k_prev):
        r = slice(i * BLK, (i + 1) * BLK)
        aq = (za[r, 0:512] * (ATT_DH ** -0.5)).astype(BF16)
        k_cur = key_variants(za[r, 512:640])
        out = []
        for j in range(ATT_KV_HEADS):
            wq = jnp.concatenate([aq[:, j * 256:j * 256 + 128], aq[:, j * 256 + 128:j * 256 + 256]], axis=0)
            for par in range(2):
                v = 2 * j + par
                out.append(_dot_nt(jnp.concatenate([k_prev[v], k_cur[v]], axis=0), wq))
        return out, k_cur

    cg_total = 1536 + D_MODEL
    cg_w = cg_total // nblk // 256 * 256
    cg_parts = []
    s_next, k_prev = scores(0, key_variants(wk_ref[0]))
    vt_prev = wv_ref[0].T.astype(BF16)
    for i in range(nblk):
        r = slice(i * BLK, (i + 1) * BLK)
        s_cur = s_next
        if i + 1 < nblk:
            s_next, k_prev = scores(i + 1, k_prev)
        cg_parts.append(_dot(h, win_ref[:, OFF_CONV + i * cg_w:OFF_CONV + (i + 1) * cg_w]))
        ps, rdens = [], []
        for v in range(2 * ATT_KV_HEADS):
            s = s_cur[v] + bias_ref[v]
            if i == 0:
                s = s + no_prev
            m = jnp.maximum(jnp.max(s, axis=0, keepdims=True), sink_rows[v])
            p = jnp.exp(s - m)
            rdens.append(1.0 / (jnp.sum(p, axis=0, keepdims=True) + jnp.exp(sink_rows[v] - m)))
            ps.append(p.astype(BF16))
        vt_cur = za[r, 640:768].T.astype(BF16)
        vt2 = jnp.concatenate([vt_prev, vt_cur], axis=1)
        heads = [None] * ATT_HEADS
        for j in range(ATT_KV_HEADS):
            for par in range(2):
                v = 2 * j + par
                ot = _dot(vt2, ps[v])[j * ATT_DH:(j + 1) * ATT_DH] * rdens[v]
                heads[ATT_GROUP * j + par] = ot[:, 0:BLK]
                heads[ATT_GROUP * j + par + 2] = ot[:, BLK:2 * BLK]
        att_scr[r, :] = jnp.concatenate(heads, axis=0).T.astype(BF16)
        vt_prev = vt_cur
    wk_ref[0] = za[tt - BLK:tt, 512:640]
    wv_ref[0] = za[tt - BLK:tt, 640:768]
    if nblk * cg_w < cg_total:
        cg_parts.append(_dot(h, win_ref[:, OFF_CONV + nblk * cg_w:OFF_CONV + cg_total]))
    cg = jnp.concatenate(cg_parts, axis=1)
    mixed = jax.nn.sigmoid(cg[:, 1536:cg_total]) * proj0
    mixed = mixed + jax.nn.sigmoid(jnp.concatenate(gate1_pre, axis=1)) * _dot(att_scr[...], wbr_ref[1])

    zc = cg[:, 0:1536]
    u = zc[:, 512:1024] * zc[:, 1024:1536]
    u_scr[8:8 + tt, :] = u
    y = (u_scr[6:6 + tt, :] * convw_ref[0:1, :] + u_scr[7:7 + tt, :] * convw_ref[1:2, :]
         + u * convw_ref[2:3, :])
    conv_out = (zc[:, 0:512] * y).astype(BF16)
    tail = u[tt - 2:tt]
    conv_ref[0] = tail
    u_scr[6:8, :] = tail
    mixed = mixed + _gated_proj(h, win_ref, 2, conv_out, wbr_ref)

    mixed = mixed.astype(BF16)
    for rw in rows:
        o_ref[0, rw, :] = x_ref[0, rw, :] + _rms(_dot(mixed[rw], wout_ref[...]), gpost_ref[...])


def _mix_prompt_call(x, sinks, gpre, win, convw, wbr, wout, gpost):
    b, t, _ = x.shape
    tt = min(PROMPT_TILE, t)
    dmask, qdec, kdec, cdec = (jnp.asarray(a, F32) for a in _retention_consts(BLK))
    dmask = dmask.reshape(RET_HEADS * BLK, BLK)
    bias = jnp.asarray(_prompt_attn_bias(), F32)
    per_b = lambda bi, ti: (bi, 0, 0)
    return pl.pallas_call(
        _mix_prompt_body,
        grid=(b, t // tt),
        in_specs=[
            pl.BlockSpec(memory_space=pltpu.SMEM),
            pl.BlockSpec((1, tt, D_MODEL), lambda bi, ti: (bi, ti, 0)),
            _const_spec((1, D_MODEL)),
            _const_spec((D_MODEL, N_IN)),
            _const_spec((CONV_W, CONV_DIM)),
            _const_spec(bias.shape),
            _const_spec(dmask.shape),
            _const_spec(qdec.shape),
            _const_spec(kdec.shape),
            _const_spec(cdec.shape),
            _const_spec((N_BRANCH, BRANCH_W, D_MODEL)),
            _const_spec((D_MODEL, D_MODEL)),
            _const_spec((1, D_MODEL)),
        ],
        out_specs=[
            pl.BlockSpec((1, tt, D_MODEL), lambda bi, ti: (bi, ti, 0)),
            pl.BlockSpec((1, RET_HEADS * RET_DK, RET_DV), per_b),
            pl.BlockSpec((1, WINDOW, 128), per_b),
            pl.BlockSpec((1, WINDOW, 128), per_b),
            pl.BlockSpec((1, CONV_W - 1, CONV_DIM), per_b),
        ],
        out_shape=[
            jax.ShapeDtypeStruct((b, t, D_MODEL), F32),
            jax.ShapeDtypeStruct((b, RET_HEADS * RET_DK, RET_DV), F32),
            jax.ShapeDtypeStruct((b, WINDOW, 128), F32),
            jax.ShapeDtypeStruct((b, WINDOW, 128), F32),
            jax.ShapeDtypeStruct((b, CONV_W - 1, CONV_DIM), F32),
        ],
        scratch_shapes=[
            pltpu.VMEM((tt, BRANCH_W), BF16),
            pltpu.VMEM((tt, BRANCH_W), BF16),
            pltpu.VMEM((tt + 8, CONV_DIM), F32),
        ],
        compiler_params=pltpu.CompilerParams(
            dimension_semantics=("arbitrary", "arbitrary"), vmem_limit_bytes=VMEM_LIMIT),
        name="mix_prompt",
    )(sinks, x, gpre, win, convw, bias, dmask, qdec, kdec, cdec, wbr, wout, gpost)


def _mix_sample_body(x_ref, sret_ref, ck_ref, cv_ref, sconv_ref, gpre_ref, win_ref, convw_ref,
                     bias_ref, sinkrow_ref, dmask_ref, qdec_ref, kdec_ref, cdec_ref, wbr_ref, wout_ref, gpost_ref,
                     o_ref, ret_ref, wk_ref, wv_ref, conv_ref,
                     z_scr, oc_scr, att_scr, cv_scr, br_scr, u_scr):
    rows = x_ref.shape[0]
    t = rows // ret_ref.shape[0]
    nseq = ret_ref.shape[0]
    w = ck_ref.shape[1]

    x = x_ref[...]
    h = _rms(x, gpre_ref[...]).astype(BF16)
    lane256 = lax.broadcasted_iota(jnp.int32, (1, 256), 1)
    lane128 = lax.broadcasted_iota(jnp.int32, (1, 128), 1)
    zero_bf = jnp.zeros((), BF16)

    z_scr[...] = _dot(h, win_ref[:, 0:OFF_GATE])

    def per_seq(b, carry):
        r0 = pl.multiple_of(b * t, t)
        rs = pl.ds(r0, t)
        q = z_scr[rs, 0:256]
        k = z_scr[rs, 256:512]
        v = z_scr[rs, 512:1024]
        qd = q * qdec_ref[...]
        qm = jnp.concatenate(
            [jnp.where((lane256 // RET_DK) == hd, qd, 0.0) for hd in range(RET_HEADS)], axis=0).astype(BF16)
        s_old = sret_ref[b]
        oc = _dot(qm, s_old.astype(BF16))
        for hd in range(RET_HEADS):
            oc_scr[hd, rs, :] = oc[hd * t:(hd + 1) * t]
        kdt = (k * kdec_ref[...]).T.astype(BF16)
        vb = v.astype(BF16)
        for hd in range(RET_HEADS):
            hr = slice(hd * RET_DK, (hd + 1) * RET_DK)
            ret_ref[b, hr, :] = (s_old[hr] * cdec_ref[hr, :]
                                 + _dot(kdt[hr], vb[:, hd * RET_DV:(hd + 1) * RET_DV]))
        aq = z_scr[rs, 1536:2048] * (ATT_DH ** -0.5)
        nk = z_scr[rs, 2048:2176]
        nv = z_scr[rs, 2176:2304]
        pieces = []
        for hd in range(ATT_HEADS):
            piece = aq[:, (hd // 2) * 128:(hd // 2 + 1) * 128]
            if hd % 2 != hd // ATT_GROUP:
                piece = pltpu.roll(piece, 64, 1)
            pieces.append(jnp.where((lane128 // ATT_DH) == hd // ATT_GROUP, piece, 0.0))
        qs = jnp.concatenate(pieces, axis=0).astype(BF16)
        pad = jnp.zeros((2 * BLK - w - t, 128), F32)
        ck = ck_ref[b]
        cv = cv_ref[b]
        k2 = jnp.concatenate([ck, nk, pad], axis=0).astype(BF16)
        v2 = jnp.concatenate([cv, nv, pad], axis=0).astype(BF16)
        s = _dot_nt(qs, k2) + bias_ref[...]
        snk = sinkrow_ref[:, 0:1]
        m = jnp.maximum(jnp.max(s, axis=-1, keepdims=True), snk)
        p = jnp.exp(s - m)
        den = jnp.sum(p, axis=-1, keepdims=True) + jnp.exp(snk - m)
        o = _dot((p / den).astype(BF16), v2)
        for pr in range(ATT_HEADS // 2):
            lo = o[(2 * pr) * t:(2 * pr + 1) * t]
            hi = o[(2 * pr + 1) * t:(2 * pr + 2) * t]
            if (2 * pr) // ATT_GROUP == 1:
                lo = pltpu.roll(lo, 64, 1)
            if (2 * pr + 1) // ATT_GROUP == 0:
                hi = pltpu.roll(hi, 64, 1)
            att_scr[rs, pr * 128:(pr + 1) * 128] = jnp.where(lane128 < ATT_DH, lo, hi)
        wk_ref[b, 0:w - t, :] = ck[t:w]
        wk_ref[b, w - t:w, :] = nk
        wv_ref[b, 0:w - t, :] = cv[t:w]
        wv_ref[b, w - t:w, :] = nv
        u = z_scr[rs, 2816:3328] * z_scr[rs, 3328:3840]
        u_scr[6:8, :] = sconv_ref[b]
        u_scr[8:8 + t, :] = u
        y = (u_scr[6:6 + t, :] * convw_ref[0:1, :] + u_scr[7:7 + t, :] * convw_ref[1:2, :]
             + u * convw_ref[2:3, :])
        cv_scr[rs, :] = z_scr[rs, 2304:2816] * y
        conv_ref[b] = u_scr[6 + t:8 + t, :]
        return carry

    lax.fori_loop(0, nseq, per_seq, 0)

    qb = z_scr[:, 0:256].astype(BF16)
    kb = z_scr[:, 256:512].astype(BF16)
    for hd in range(RET_HEADS):
        hm = (lane256 // RET_DK) == hd
        inner = _dot_nt(jnp.where(hm, qb, zero_bf), kb) * dmask_ref[hd]
        vh = z_scr[:, 512 + hd * RET_DV:512 + (hd + 1) * RET_DV].astype(BF16)
        o = _dot(inner.astype(BF16), vh) + oc_scr[hd]
        o = o * lax.rsqrt(jnp.mean(o * o, axis=-1, keepdims=True) + EPS)
        g = z_scr[:, 1024 + hd * RET_DV:1024 + (hd + 1) * RET_DV]
        br_scr[:, hd * RET_DV:(hd + 1) * RET_DV] = (g * jax.nn.sigmoid(g) * o).astype(BF16)
    mixed = _gated_proj(h, win_ref, 0, br_scr[...], wbr_ref)
    mixed = mixed + _gated_proj(h, win_ref, 1, att_scr[...].astype(BF16), wbr_ref)
    mixed = mixed + _gated_proj(h, win_ref, 2, cv_scr[...].astype(BF16), wbr_ref)
    o_ref[...] = _merge_and_residual(x, mixed, wout_ref, gpost_ref)


def _mix_sample_call(x, sinks, sret, ck, cv, sconv, gpre, win, convw, wbr, wout, gpost):
    nb, t, _ = x.shape
    w = ck.shape[1]
    sb = min(SAMPLE_SEQS, nb)
    rows = sb * t
    dm, qd, kd, cdec = _retention_consts(t)
    same_seq = np.kron(np.eye(sb), np.ones((t, t)))
    dmask = jnp.asarray(np.tile(dm, (1, sb, sb)) * same_seq[None], F32)
    qdec = jnp.asarray(qd, F32)
    kdec = jnp.asarray(kd, F32)
    cdec = jnp.asarray(cdec, F32)
    bias = jnp.asarray(_sample_attn_bias(t, w), F32)
    sinkrow = jnp.broadcast_to(jnp.repeat(sinks, t)[:, None], (ATT_HEADS * t, 128))
    x2 = x.reshape(nb * t, D_MODEL)
    seq3 = lambda i: (i, 0, 0)
    row = lambda i: (i, 0)
    outs = pl.pallas_call(
        _mix_sample_body,
        grid=(nb // sb,),
        in_specs=[
            pl.BlockSpec((rows, D_MODEL), row),
            pl.BlockSpec((sb, RET_HEADS * RET_DK, RET_DV), seq3),
            pl.BlockSpec((sb, w, 128), seq3),
            pl.BlockSpec((sb, w, 128), seq3),
            pl.BlockSpec((sb, CONV_W - 1, CONV_DIM), seq3),
            _const_spec((1, D_MODEL)),
            _const_spec((D_MODEL, N_IN)),
            _const_spec((CONV_W, CONV_DIM)),
            _const_spec(bias.shape),
            _const_spec(sinkrow.shape),
            _const_spec(dmask.shape),
            _const_spec(qdec.shape),
            _const_spec(kdec.shape),
            _const_spec(cdec.shape),
            _const_spec((N_BRANCH, BRANCH_W, D_MODEL)),
            _const_spec((D_MODEL, D_MODEL)),
            _const_spec((1, D_MODEL)),
        ],
        out_specs=[
            pl.BlockSpec((rows, D_MODEL), row),
            pl.BlockSpec((sb, RET_HEADS * RET_DK, RET_DV), seq3),
            pl.BlockSpec((sb, w, 128), seq3),
            pl.BlockSpec((sb, w, 128), seq3),
            pl.BlockSpec((sb, CONV_W - 1, CONV_DIM), seq3),
        ],
        out_shape=[
            jax.ShapeDtypeStruct((nb * t, D_MODEL), F32),
            jax.ShapeDtypeStruct((nb, RET_HEADS * RET_DK, RET_DV), F32),
            jax.ShapeDtypeStruct((nb, w, 128), F32),
            jax.ShapeDtypeStruct((nb, w, 128), F32),
            jax.ShapeDtypeStruct((nb, CONV_W - 1, CONV_DIM), F32),
        ],
        scratch_shapes=[
            pltpu.VMEM((rows, OFF_GATE), F32),
            pltpu.VMEM((RET_HEADS, rows, RET_DV), F32),
            pltpu.VMEM((rows, BRANCH_W), F32),
            pltpu.VMEM((rows, BRANCH_W), F32),
            pltpu.VMEM((rows, BRANCH_W), BF16),
            pltpu.VMEM((8 + t, CONV_DIM), F32),
        ],
        compiler_params=pltpu.CompilerParams(
            dimension_semantics=("arbitrary",), vmem_limit_bytes=VMEM_LIMIT),
        name="mix_sample",
    )(x2, sret, ck, cv, sconv, gpre, win, convw, bias, sinkrow, dmask, qdec, kdec, cdec, wbr, wout, gpost)
    return outs


def kernel(x_prompt, x_sample, p_prompt, p_sample, state_ret, cache_win_k, cache_win_v, state_conv,
           g_mix_pre, w_in, conv_w, attn_sinks, w_branch, w_out, g_mix_post,
           g_ffn_pre, w_ff1, w_ff2, g_ffn_post, g_ple, w_ple_gate, w_ple_proj):
    depth = w_in.shape[0]
    bp, tp, d = x_prompt.shape
    bs, ts, _ = x_sample.shape
    win_w = cache_win_k.shape[2]
    assert d == D_MODEL and w_in.shape[2] == N_IN and tp % BLK == 0 and math.gcd(ts, BLK) == ts
    assert win_w == WINDOW and win_w + ts <= 2 * BLK

    yp = x_prompt
    ys = x_sample.reshape(bs * ts, d)
    outs_p, outs_s = [], []
    for l in range(depth):
        g = lambda a: a[l].reshape(1, D_MODEL)
        win = w_in[l].astype(BF16)
        wbr = w_branch[l].astype(BF16)
        wout = w_out[l].astype(BF16)
        w1 = w_ff1[l].astype(BF16)
        w2 = w_ff2[l].astype(BF16)
        wg = w_ple_gate[l].astype(BF16)
        wp = w_ple_proj[l].astype(BF16)
        ffn = functools.partial(_ffn_call, gpre=g(g_ffn_pre), w1=w1, w2=w2, gpost=g(g_ffn_post),
                                gple=g(g_ple), wg=wg, wp=wp)

        yp, r, k, v, c = _mix_prompt_call(yp, attn_sinks[l], g(g_mix_pre), win, conv_w[l], wbr, wout,
                                          g(g_mix_post))
        yp = ffn(yp.reshape(bp * tp, d), p_prompt[l].reshape(bp * tp, D_PLE)).reshape(bp, tp, d)
        outs_p.append((r, k, v, c))

        ys, r, k, v, c = _mix_sample_call(
            ys.reshape(bs, ts, d), attn_sinks[l],
            state_ret[l].reshape(bs, RET_HEADS * RET_DK, RET_DV),
            cache_win_k[l].reshape(bs, win_w, 128), cache_win_v[l].reshape(bs, win_w, 128),
            state_conv[l], g(g_mix_pre), win, conv_w[l], wbr, wout, g(g_mix_post))
        ys = ffn(ys, p_sample[l].reshape(bs * ts, D_PLE))
        outs_s.append((r, k, v, c))

    def stack(outs, i, shape):
        return jnp.stack([o[i] for o in outs]).reshape(shape)

    return (yp, ys.reshape(bs, ts, d),
            stack(outs_p, 0, (depth, bp, RET_HEADS, RET_DK, RET_DV)),
            stack(outs_p, 1, (depth, bp, WINDOW, ATT_KV_HEADS, ATT_DH)),
            stack(outs_p, 2, (depth, bp, WINDOW, ATT_KV_HEADS, ATT_DH)),
            stack(outs_p, 3, (depth, bp, CONV_W - 1, CONV_DIM)),
            stack(outs_s, 0, (depth, bs, RET_HEADS, RET_DK, RET_DV)),
            stack(outs_s, 1, (depth, bs, win_w, ATT_KV_HEADS, ATT_DH)),
            stack(outs_s, 2, (depth, bs, win_w, ATT_KV_HEADS, ATT_DH)),
            stack(outs_s, 3, (depth, bs, CONV_W - 1, CONV_DIM)))
```

```python
import functools
import math

import numpy as np
import jax
import jax.numpy as jnp
from jax import lax
from jax.experimental import pallas as pl
from jax.experimental.pallas import tpu as pltpu

F32 = jnp.float32
BF16 = jnp.bfloat16

D_MODEL = 1024
BRANCH_W = 512
N_BRANCH = 3
RET_HEADS = 4
RET_DK = 64
RET_DV = 128
RET_W = RET_HEADS * RET_DK
ATT_HEADS = 8
ATT_KV_HEADS = 2
ATT_GROUP = ATT_HEADS // ATT_KV_HEADS
ATT_DH = 64
KV_W = ATT_KV_HEADS * ATT_DH
WINDOW = 128
CONV_DIM = 512
CONV_W = 3
D_FF = 4096
D_PLE = 256
EPS = 1e-6

OFF_RET = 0
OFF_ATT = 1536
OFF_CONV = 2304
OFF_GATE = 3840
N_IN = 6912

BLK = 128
NEG = -1e30
VMEM_LIMIT = 56 * 1024 * 1024

PROMPT_TILE = 512
ROW_SPLIT = 4
FFN_TILE = 512
FFN_CHUNK = 1024
SAMPLE_SEQS = 16
SEQ_GROUP = 4


def _dot(a, b):
    return jnp.dot(a, b, preferred_element_type=F32)


def _dot_nt(a, b):
    return lax.dot_general(a, b, (((1,), (1,)), ((), ())), preferred_element_type=F32)


def _rms(x, g):
    return x * lax.rsqrt(jnp.mean(x * x, axis=-1, keepdims=True) + EPS) * g


def _const_spec(shape):
    nd = len(shape)
    return pl.BlockSpec(shape, lambda *_: (0,) * nd, pipeline_mode=pl.Buffered(1))


def _layer_spec(layer, shape):
    nd = len(shape)
    return pl.BlockSpec((None,) + tuple(shape), lambda *_: (layer,) + (0,) * nd, pipeline_mode=pl.Buffered(1))


def _log_gamma():
    return np.log1p(-np.exp2(-5.0 - np.arange(RET_HEADS, dtype=np.float64)))


def _retention_consts(c):
    lg = _log_gamma()
    idx = np.arange(c, dtype=np.float64)
    diff = idx[:, None] - idx[None, :]
    dmask = np.where(diff >= 0, np.exp(lg[:, None, None] * np.maximum(diff, 0.0)), 0.0) * RET_DK ** -0.5
    qdec = np.repeat(np.exp(lg[:, None] * (idx + 1.0)).T, RET_DK, axis=1)
    kdec = np.repeat(np.exp(lg[:, None] * (c - 1.0 - idx)).T, RET_DK, axis=1) * RET_DK ** -0.5
    cdec = np.broadcast_to(np.repeat(np.exp(lg * c), RET_DK)[:, None], (RET_W, RET_DV))
    return dmask, qdec, kdec, cdec


def _slopes():
    return np.exp2(-8.0 * (np.arange(ATT_HEADS, dtype=np.float64) + 1.0) / ATT_HEADS)


def _prompt_attn_bias():
    s = np.arange(2 * BLK)[:, None]
    q = np.arange(BLK)[None, :]
    dist = q + BLK - s
    allowed = (dist >= 0) & (dist < WINDOW)
    per_head = np.where(allowed[None], -_slopes()[:, None, None] * dist[None], NEG)
    out = [np.concatenate([per_head[ATT_GROUP * j + par], per_head[ATT_GROUP * j + par + 2]], axis=1)
           for j in range(ATT_KV_HEADS) for par in range(2)]
    return np.stack(out)


def _sample_attn_bias(t):
    i = np.arange(t)[:, None]
    col = np.arange(2 * BLK)[None, :]
    new = col >= 2 * BLK - t
    dist = np.where(new, i - (col - (2 * BLK - t)), i + WINDOW - col)
    allowed = (dist >= 0) & (dist < WINDOW) & ((col < WINDOW) | new)
    b = np.where(allowed[None], -_slopes()[:, None, None] * dist[None], NEG)
    return b.reshape(ATT_HEADS * t, 2 * BLK)


def _ffn_body(x_ref, p_ref, gpre_ref, w1_ref, w2_ref, gpost_ref, gple_ref, wg_ref, wp_ref, o_ref):
    x = x_ref[...]
    h = _rms(x, gpre_ref[...]).astype(BF16)
    acc = None
    for c in range(D_FF // FFN_CHUNK):
        a = _dot(h, w1_ref[:, c * FFN_CHUNK:(c + 1) * FFN_CHUNK])
        a = jnp.square(jnp.maximum(a, 0.0)).astype(BF16)
        d = _dot(a, w2_ref[c * FFN_CHUNK:(c + 1) * FFN_CHUNK, :])
        acc = d if acc is None else acc + d
    x = x + _rms(acc, gpost_ref[...])
    gate = jax.nn.sigmoid(_dot(_rms(x, gple_ref[...]).astype(BF16), wg_ref[...]))
    o_ref[...] = x + gate * _dot(p_ref[...].astype(BF16), wp_ref[...])


def _ffn_call(layer, x, p, gpre, w1, w2, gpost, gple, wg, wp):
    n = x.shape[0]
    tile = min(FFN_TILE, n)
    row = lambda i: (i, 0)
    return pl.pallas_call(
        _ffn_body,
        grid=(n // tile,),
        in_specs=[
            pl.BlockSpec((tile, D_MODEL), row),
            pl.BlockSpec((None, tile, D_PLE), lambda i: (layer, i, 0)),
            _layer_spec(layer, (1, D_MODEL)),
            _layer_spec(layer, (D_MODEL, D_FF)),
            _layer_spec(layer, (D_FF, D_MODEL)),
            _layer_spec(layer, (1, D_MODEL)),
            _layer_spec(layer, (1, D_MODEL)),
            _layer_spec(layer, (D_MODEL, D_MODEL)),
            _layer_spec(layer, (D_PLE, D_MODEL)),
        ],
        out_specs=pl.BlockSpec((tile, D_MODEL), row),
        out_shape=jax.ShapeDtypeStruct((n, D_MODEL), F32),
        compiler_params=pltpu.CompilerParams(
            dimension_semantics=("arbitrary",), vmem_limit_bytes=VMEM_LIMIT),
        name="ffn",
    )(x, p, gpre, w1, w2, gpost, gple, wg, wp)


def _gated_proj(h, win_ref, n, br, wbr_ref):
    gate = jax.nn.sigmoid(_dot(h, win_ref[:, OFF_GATE + n * D_MODEL:OFF_GATE + (n + 1) * D_MODEL]))
    return gate * _dot(br, wbr_ref[n])


def _mix_prompt_body(layer, sink_ref, x_ref, gpre_ref, win_ref, convw_ref, bias_ref, dmask_ref, qdec_ref,
                     kdec_ref, cdec_ref, wbr_ref, wout_ref, gpost_ref,
                     o_ref, ret_ref, wk_ref, wv_ref, conv_ref,
                     br_scr, att_scr, u_scr):
    tt = x_ref.shape[1]
    nblk = tt // BLK
    first = pl.program_id(1) == 0

    @pl.when(first)
    def _init():
        ret_ref[...] = jnp.zeros_like(ret_ref)
        wk_ref[...] = jnp.zeros_like(wk_ref)
        wv_ref[...] = jnp.zeros_like(wv_ref)
        u_scr[0:8, :] = jnp.zeros((8, CONV_DIM), F32)

    lane256 = lax.broadcasted_iota(jnp.int32, (1, 256), 1)
    zero_bf = jnp.zeros((), BF16)

    rows = [slice(i * tt // ROW_SPLIT, (i + 1) * tt // ROW_SPLIT) for i in range(ROW_SPLIT)]
    h_parts = [_rms(x_ref[0, rw, :], gpre_ref[...]).astype(BF16) for rw in rows]
    zr = jnp.concatenate([_dot(hp, win_ref[:, OFF_RET:OFF_RET + 1536]) for hp in h_parts], axis=0)
    h = jnp.concatenate(h_parts, axis=0)

    head_masks = [(lane256 // RET_DK) == hd for hd in range(RET_HEADS)]

    def head_stack(a):
        return jnp.concatenate([jnp.where(hm, a, zero_bf) for hm in head_masks], axis=0)

    def inner_scores(c):
        r = slice(c * BLK, (c + 1) * BLK)
        return _dot_nt(head_stack(zr[r, 0:256].astype(BF16)), zr[r, 256:512].astype(BF16))

    gate_w = D_MODEL // nblk
    gate1_off = OFF_GATE + D_MODEL
    gate1_pre = []
    inner_next = inner_scores(0)
    for c in range(nblk):
        r = slice(c * BLK, (c + 1) * BLK)
        inner_raw = inner_next
        if c + 1 < nblk:
            inner_next = inner_scores(c + 1)
        gate1_pre.append(_dot(h, win_ref[:, gate1_off + c * gate_w:gate1_off + (c + 1) * gate_w]))
        qd = (zr[r, 0:256] * qdec_ref[...]).astype(BF16)
        kdt = (zr[r, 256:512] * kdec_ref[...]).T.astype(BF16)
        s_old = ret_ref[0]
        cross = _dot(head_stack(qd), s_old.astype(BF16))
        inner = (inner_raw * dmask_ref[...]).astype(BF16)
        for hd in range(RET_HEADS):
            hr = slice(hd * RET_DK, (hd + 1) * RET_DK)
            hq = slice(hd * BLK, (hd + 1) * BLK)
            vh = zr[r, 512 + hd * RET_DV:512 + (hd + 1) * RET_DV].astype(BF16)
            o = _dot(inner[hq], vh) + cross[hq]
            ret_ref[0, hr, :] = s_old[hr] * cdec_ref[hr, :] + _dot(kdt[hr], vh)
            o = o * lax.rsqrt(jnp.mean(o * o, axis=-1, keepdims=True) + EPS)
            g = zr[r, 1024 + hd * RET_DV:1024 + (hd + 1) * RET_DV]
            br_scr[r, hd * RET_DV:(hd + 1) * RET_DV] = (g * jax.nn.sigmoid(g) * o).astype(BF16)
    proj0 = _dot(br_scr[...], wbr_ref[0])

    za = _dot(h, win_ref[:, OFF_ATT:OFF_ATT + 768])
    low = lax.broadcasted_iota(jnp.int32, (1, 128), 1) < ATT_DH
    key_row = lax.broadcasted_iota(jnp.int32, (2 * BLK, 1), 0)
    no_prev = jnp.where(key_row < BLK, jnp.where(first, NEG, 0.0), 0.0).astype(F32)
    sink_rows = [jnp.where(lane256 < BLK, sink_ref[layer, ATT_GROUP * j + par],
                           sink_ref[layer, ATT_GROUP * j + par + 2])
                 for j in range(ATT_KV_HEADS) for par in range(2)]

    def key_variants(kblk):
        rolled = pltpu.roll(kblk, ATT_DH, 1)
        return [jnp.where(low, kblk, 0.0).astype(BF16), jnp.where(low, 0.0, rolled).astype(BF16),
                jnp.where(low, rolled, 0.0).astype(BF16), jnp.where(low, 0.0, kblk).astype(BF16)]

    def scores(i, k_prev):
        r = slice(i * BLK, (i + 1) * BLK)
        aq = (za[r, 0:512] * (ATT_DH ** -0.5)).astype(BF16)
        k_cur = key_variants(za[r, 512:640])
        out = []
        for j in range(ATT_KV_HEADS):
            wq = jnp.concatenate([aq[:, j * 256:j * 256 + 128], aq[:, j * 256 + 128:j * 256 + 256]], axis=0)
            for par in range(2):
                v = 2 * j + par
                out.append(_dot_nt(jnp.concatenate([k_prev[v], k_cur[v]], axis=0), wq))
        return out, k_cur

    cg_total = 1536 + D_MODEL
    cg_w = cg_total // nblk // 256 * 256
    cg_parts = []
    s_next, k_prev = scores(0, key_variants(wk_ref[0]))
    vt_prev = wv_ref[0].T.astype(BF16)
    for i in range(nblk):
        r = slice(i * BLK, (i + 1) * BLK)
        s_cur = s_next
        if i + 1 < nblk:
            s_next, k_prev = scores(i + 1, k_prev)
        cg_parts.append(_dot(h, win_ref[:, OFF_CONV + i * cg_w:OFF_CONV + (i + 1) * cg_w]))
        ps, rdens = [], []
        for v in range(2 * ATT_KV_HEADS):
            s = s_cur[v] + bias_ref[v]
            if i == 0:
                s = s + no_prev
            m = jnp.maximum(jnp.max(s, axis=0, keepdims=True), sink_rows[v])
            p = jnp.exp(s - m)
            rdens.append(1.0 / (jnp.sum(p, axis=0, keepdims=True) + jnp.exp(sink_rows[v] - m)))
            ps.append(p.astype(BF16))
        vt_cur = za[r, 640:768].T.astype(BF16)
        vt2 = jnp.concatenate([vt_prev, vt_cur], axis=1)
        heads = [None] * ATT_HEADS
        for j in range(ATT_KV_HEADS):
            for par in range(2):
                v = 2 * j + par
                ot = _dot(vt2, ps[v])[j * ATT_DH:(j + 1) * ATT_DH] * rdens[v]
                heads[ATT_GROUP * j + par] = ot[:, 0:BLK]
                heads[ATT_GROUP * j + par + 2] = ot[:, BLK:2 * BLK]
        att_scr[r, :] = jnp.concatenate(heads, axis=0).T.astype(BF16)
        vt_prev = vt_cur
    wk_ref[0] = za[tt - BLK:tt, 512:640]
    wv_ref[0] = za[tt - BLK:tt, 640:768]
    if nblk * cg_w < cg_total:
        cg_parts.append(_dot(h, win_ref[:, OFF_CONV + nblk * cg_w:OFF_CONV + cg_total]))
    cg = jnp.concatenate(cg_parts, axis=1)
    mixed = jax.nn.sigmoid(cg[:, 1536:cg_total]) * proj0
    mixed = mixed + jax.nn.sigmoid(jnp.concatenate(gate1_pre, axis=1)) * _dot(att_scr[...], wbr_ref[1])

    zc = cg[:, 0:1536]
    u = zc[:, 512:1024] * zc[:, 1024:1536]
    u_scr[8:8 + tt, :] = u
    y = (u_scr[6:6 + tt, :] * convw_ref[0:1, :] + u_scr[7:7 + tt, :] * convw_ref[1:2, :]
         + u * convw_ref[2:3, :])
    conv_out = (zc[:, 0:512] * y).astype(BF16)
    tail = u[tt - 2:tt]
    conv_ref[0] = tail
    u_scr[6:8, :] = tail
    mixed = mixed + _gated_proj(h, win_ref, 2, conv_out, wbr_ref)

    mixed = mixed.astype(BF16)
    for rw in rows:
        o_ref[0, rw, :] = x_ref[0, rw, :] + _rms(_dot(mixed[rw], wout_ref[...]), gpost_ref[...])


def _mix_prompt_call(layer, x, sinks, gpre, win, convw, wbr, wout, gpost):
    b, t, _ = x.shape
    tt = min(PROMPT_TILE, t)
    dmask, qdec, kdec, cdec = (jnp.asarray(a, F32) for a in _retention_consts(BLK))
    dmask = dmask.reshape(RET_HEADS * BLK, BLK)
    bias = jnp.asarray(_prompt_attn_bias(), F32)
    per_b = lambda bi, ti: (bi, 0, 0)
    return pl.pallas_call(
        functools.partial(_mix_prompt_body, layer),
        grid=(b, t // tt),
        in_specs=[
            pl.BlockSpec(memory_space=pltpu.SMEM),
            pl.BlockSpec((1, tt, D_MODEL), lambda bi, ti: (bi, ti, 0)),
            _layer_spec(layer, (1, D_MODEL)),
            _layer_spec(layer, (D_MODEL, N_IN)),
            _layer_spec(layer, (CONV_W, CONV_DIM)),
            _const_spec(bias.shape),
            _const_spec(dmask.shape),
            _const_spec(qdec.shape),
            _const_spec(kdec.shape),
            _const_spec(cdec.shape),
            _layer_spec(layer, (N_BRANCH, BRANCH_W, D_MODEL)),
            _layer_spec(layer, (D_MODEL, D_MODEL)),
            _layer_spec(layer, (1, D_MODEL)),
        ],
        out_specs=[
            pl.BlockSpec((1, tt, D_MODEL), lambda bi, ti: (bi, ti, 0)),
            pl.BlockSpec((1, RET_W, RET_DV), per_b),
            pl.BlockSpec((1, WINDOW, KV_W), per_b),
            pl.BlockSpec((1, WINDOW, KV_W), per_b),
            pl.BlockSpec((1, CONV_W - 1, CONV_DIM), per_b),
        ],
        out_shape=[
            jax.ShapeDtypeStruct((b, t, D_MODEL), F32),
            jax.ShapeDtypeStruct((b, RET_W, RET_DV), F32),
            jax.ShapeDtypeStruct((b, WINDOW, KV_W), F32),
            jax.ShapeDtypeStruct((b, WINDOW, KV_W), F32),
            jax.ShapeDtypeStruct((b, CONV_W - 1, CONV_DIM), F32),
        ],
        scratch_shapes=[
            pltpu.VMEM((tt, BRANCH_W), BF16),
            pltpu.VMEM((tt, BRANCH_W), BF16),
            pltpu.VMEM((tt + 8, CONV_DIM), F32),
        ],
        compiler_params=pltpu.CompilerParams(
            dimension_semantics=("arbitrary", "arbitrary"), vmem_limit_bytes=VMEM_LIMIT),
        name="mix_prompt",
    )(sinks, x, gpre, win, convw, bias, dmask, qdec, kdec, cdec, wbr, wout, gpost)


N_SAMPLE_IN = 17
N_SAMPLE_STATE = 4


def _mix_sample_body(n_alias, *refs):
    (x_ref, sret_ref, kt_ref, vt_ref, sconv_ref, gpre_ref, win_ref, convw_ref, bias_ref, sinkrow_ref,
     dmask_ref, qdec_ref, kdec_ref, cdec_ref, wbr_ref, wout_ref, gpost_ref) = refs[:N_SAMPLE_IN]
    o_ref, ret_ref, wk_ref, wv_ref, conv_ref = refs[N_SAMPLE_IN + n_alias:N_SAMPLE_IN + n_alias + 5]
    z_scr, oc_scr, att_scr, cv_scr, br_scr, u_scr = refs[N_SAMPLE_IN + n_alias + 5:]
    rows = x_ref.shape[0]
    nseq = ret_ref.shape[0]
    t = rows // nseq

    x = x_ref[...]
    h = _rms(x, gpre_ref[...]).astype(BF16)
    lane256 = lax.broadcasted_iota(jnp.int32, (1, 256), 1)
    lane128 = lax.broadcasted_iota(jnp.int32, (1, 128), 1)
    zero_bf = jnp.zeros((), BF16)
    head_masks = [(lane256 // RET_DK) == hd for hd in range(RET_HEADS)]
    new_lanes = lane128 >= BLK - t
    zpad = jnp.zeros((BLK - t, KV_W), F32)
    snk = sinkrow_ref[:, 0:1]

    z_scr[...] = _dot(h, win_ref[:, 0:OFF_GATE])

    def new_rows_t(rs, c0):
        return jnp.concatenate([zpad, z_scr[rs, c0:c0 + KV_W]], axis=0).T

    for g0 in range(0, nseq, SEQ_GROUP):
        seqs = list(range(g0, min(g0 + SEQ_GROUP, nseq)))
        rsl = {b: slice(b * t, (b + 1) * t) for b in seqs}

        for b in seqs:
            rs = rsl[b]
            qd = z_scr[rs, 0:256] * qdec_ref[...]
            qm = jnp.concatenate([jnp.where(hm, qd, 0.0) for hm in head_masks], axis=0).astype(BF16)
            s_old = sret_ref[b]
            oc = _dot(qm, s_old.astype(BF16))
            for hd in range(RET_HEADS):
                oc_scr[hd, rs, :] = oc[hd * t:(hd + 1) * t]
            kdt = (z_scr[rs, 256:512] * kdec_ref[...]).T.astype(BF16)
            vb = z_scr[rs, 512:1024].astype(BF16)
            for hd in range(RET_HEADS):
                hr = slice(hd * RET_DK, (hd + 1) * RET_DK)
                ret_ref[b, hr, :] = (s_old[hr] * cdec_ref[hr, :]
                                     + _dot(kdt[hr], vb[:, hd * RET_DV:(hd + 1) * RET_DV]))

        s_raw, nkt = {}, {}
        for b in seqs:
            rs = rsl[b]
            aq = z_scr[rs, 1536:2048] * (ATT_DH ** -0.5)
            pieces = []
            for hd in range(ATT_HEADS):
                piece = aq[:, (hd // 2) * 128:(hd // 2 + 1) * 128]
                if hd % 2 != hd // ATT_GROUP:
                    piece = pltpu.roll(piece, ATT_DH, 1)
                pieces.append(jnp.where((lane128 // ATT_DH) == hd // ATT_GROUP, piece, 0.0))
            qs = jnp.concatenate(pieces, axis=0).astype(BF16)
            nkt[b] = new_rows_t(rs, 2048)
            k2t = jnp.concatenate([kt_ref[b], nkt[b]], axis=1).astype(BF16)
            s_raw[b] = _dot(qs, k2t)

        probs = {}
        for b in seqs:
            s = s_raw[b] + bias_ref[...]
            m = jnp.maximum(jnp.max(s, axis=-1, keepdims=True), snk)
            p = jnp.exp(s - m)
            rden = 1.0 / (jnp.sum(p, axis=-1, keepdims=True) + jnp.exp(snk - m))
            probs[b] = (p * rden).astype(BF16)

        for b in seqs:
            rs = rsl[b]
            nvt = new_rows_t(rs, 2176)
            v2t = jnp.concatenate([vt_ref[b], nvt], axis=1).astype(BF16)
            o = _dot_nt(probs[b], v2t)
            for pr in range(ATT_HEADS // 2):
                lo = o[(2 * pr) * t:(2 * pr + 1) * t]
                hi = o[(2 * pr + 1) * t:(2 * pr + 2) * t]
                if (2 * pr) // ATT_GROUP == 1:
                    lo = pltpu.roll(lo, ATT_DH, 1)
                if (2 * pr + 1) // ATT_GROUP == 0:
                    hi = pltpu.roll(hi, ATT_DH, 1)
                att_scr[rs, pr * 128:(pr + 1) * 128] = jnp.where(lane128 < ATT_DH, lo, hi)
            wk_ref[b] = jnp.where(new_lanes, nkt[b], pltpu.roll(kt_ref[b], BLK - t, 1))
            wv_ref[b] = jnp.where(new_lanes, nvt, pltpu.roll(vt_ref[b], BLK - t, 1))

        for b in seqs:
            rs = rsl[b]
            u = z_scr[rs, 2816:3328] * z_scr[rs, 3328:3840]
            u_scr[b, 6:8, :] = sconv_ref[b]
            u_scr[b, 8:8 + t, :] = u
            y = (u_scr[b, 6:6 + t, :] * convw_ref[0:1, :] + u_scr[b, 7:7 + t, :] * convw_ref[1:2, :]
                 + u * convw_ref[2:3, :])
            cv_scr[rs, :] = z_scr[rs, 2304:2816] * y
            conv_ref[b] = u_scr[b, 6 + t:8 + t, :]

    qb = z_scr[:, 0:256].astype(BF16)
    kb = z_scr[:, 256:512].astype(BF16)
    for hd in range(RET_HEADS):
        inner = _dot_nt(jnp.where(head_masks[hd], qb, zero_bf), kb) * dmask_ref[hd]
        vh = z_scr[:, 512 + hd * RET_DV:512 + (hd + 1) * RET_DV].astype(BF16)
        o = _dot(inner.astype(BF16), vh) + oc_scr[hd]
        o = o * lax.rsqrt(jnp.mean(o * o, axis=-1, keepdims=True) + EPS)
        g = z_scr[:, 1024 + hd * RET_DV:1024 + (hd + 1) * RET_DV]
        br_scr[:, hd * RET_DV:(hd + 1) * RET_DV] = (g * jax.nn.sigmoid(g) * o).astype(BF16)
    mixed = _gated_proj(h, win_ref, 0, br_scr[...], wbr_ref)
    mixed = mixed + _gated_proj(h, win_ref, 1, att_scr[...].astype(BF16), wbr_ref)
    mixed = mixed + _gated_proj(h, win_ref, 2, cv_scr[...].astype(BF16), wbr_ref)
    o_ref[...] = x + _rms(_dot(mixed.astype(BF16), wout_ref[...]), gpost_ref[...])


def _mix_sample_call(layer, x2, t, sinkrow, sret, kt, vt, sconv, gpre, win, convw, wbr, wout, gpost, prev):
    depth, nb = sret.shape[0], sret.shape[1]
    sb = min(SAMPLE_SEQS, nb)
    rows = sb * t
    dm, qd, kd, cdec = _retention_consts(t)
    same_seq = np.kron(np.eye(sb), np.ones((t, t)))
    dmask = jnp.asarray(np.tile(dm, (1, sb, sb)) * same_seq[None], F32)
    qdec = jnp.asarray(qd, F32)
    kdec = jnp.asarray(kd, F32)
    cdec = jnp.asarray(cdec, F32)
    bias = jnp.asarray(_sample_attn_bias(t), F32)
    row = lambda i: (i, 0)
    seq = lambda i: (layer, i, 0, 0)
    state_shapes = [(RET_W, RET_DV), (KV_W, WINDOW), (KV_W, WINDOW), (CONV_W - 1, CONV_DIM)]
    state_specs = [pl.BlockSpec((None, sb) + s, seq) for s in state_shapes]
    n_alias = 0 if prev is None else N_SAMPLE_STATE
    alias_args = [] if prev is None else list(prev)
    return pl.pallas_call(
        functools.partial(_mix_sample_body, n_alias),
        grid=(nb // sb,),
        in_specs=[pl.BlockSpec((rows, D_MODEL), row)] + state_specs + [
            _layer_spec(layer, (1, D_MODEL)),
            _layer_spec(layer, (D_MODEL, N_IN)),
            _layer_spec(layer, (CONV_W, CONV_DIM)),
            _const_spec(bias.shape),
            _layer_spec(layer, (ATT_HEADS * t, 128)),
            _const_spec(dmask.shape),
            _const_spec(qdec.shape),
            _const_spec(kdec.shape),
            _const_spec(cdec.shape),
            _layer_spec(layer, (N_BRANCH, BRANCH_W, D_MODEL)),
            _layer_spec(layer, (D_MODEL, D_MODEL)),
            _layer_spec(layer, (1, D_MODEL)),
        ] + [pl.BlockSpec(memory_space=pl.ANY)] * n_alias,
        out_specs=[pl.BlockSpec((rows, D_MODEL), row)] + state_specs,
        out_shape=[jax.ShapeDtypeStruct((nb * t, D_MODEL), F32)]
        + [jax.ShapeDtypeStruct((depth, nb) + s, F32) for s in state_shapes],
        input_output_aliases={N_SAMPLE_IN + k: 1 + k for k in range(n_alias)},
        scratch_shapes=[
            pltpu.VMEM((rows, OFF_GATE), F32),
            pltpu.VMEM((RET_HEADS, rows, RET_DV), F32),
            pltpu.VMEM((rows, BRANCH_W), F32),
            pltpu.VMEM((rows, BRANCH_W), F32),
            pltpu.VMEM((rows, BRANCH_W), BF16),
            pltpu.VMEM((sb, 8 + t, CONV_DIM), F32),
        ],
        compiler_params=pltpu.CompilerParams(
            dimension_semantics=("arbitrary",), vmem_limit_bytes=VMEM_LIMIT),
        name="mix_sample",
    )(x2, sret, kt, vt, sconv, gpre, win, convw, bias, sinkrow, dmask, qdec, kdec, cdec, wbr, wout, gpost,
      *alias_args)


def kernel(x_prompt, x_sample, p_prompt, p_sample, state_ret, cache_win_k, cache_win_v, state_conv,
           g_mix_pre, w_in, conv_w, attn_sinks, w_branch, w_out, g_mix_post,
           g_ffn_pre, w_ff1, w_ff2, g_ffn_post, g_ple, w_ple_gate, w_ple_proj):
    depth = w_in.shape[0]
    bp, tp, d = x_prompt.shape
    bs, ts, _ = x_sample.shape
    win_w = cache_win_k.shape[2]
    assert d == D_MODEL and w_in.shape[2] == N_IN and tp % BLK == 0 and math.gcd(ts, BLK) == ts
    assert win_w == WINDOW == BLK and ts <= BLK

    gain = lambda a: a.reshape(depth, 1, D_MODEL)
    win, wbr, wout = w_in.astype(BF16), w_branch.astype(BF16), w_out.astype(BF16)
    ffn_w = (gain(g_ffn_pre), w_ff1.astype(BF16), w_ff2.astype(BF16), gain(g_ffn_post), gain(g_ple),
             w_ple_gate.astype(BF16), w_ple_proj.astype(BF16))
    gpre, gpost = gain(g_mix_pre), gain(g_mix_post)
    pp = p_prompt.reshape(depth, bp * tp, D_PLE)
    ps = p_sample.reshape(depth, bs * ts, D_PLE)
    sret = state_ret.reshape(depth, bs, RET_W, RET_DV)
    kt = jnp.transpose(cache_win_k, (0, 1, 3, 4, 2)).reshape(depth, bs, KV_W, win_w)
    vt = jnp.transpose(cache_win_v, (0, 1, 3, 4, 2)).reshape(depth, bs, KV_W, win_w)
    sinkrow = jnp.broadcast_to(jnp.repeat(attn_sinks, ts, axis=1)[:, :, None], (depth, ATT_HEADS * ts, 128))

    yp = x_prompt
    ys = x_sample.reshape(bs * ts, d)
    outs_p, sample_states = [], None
    for l in range(depth):
        yp, r, k, v, c = _mix_prompt_call(l, yp, attn_sinks, gpre, win, conv_w, wbr, wout, gpost)
        yp = _ffn_call(l, yp.reshape(bp * tp, d), pp, *ffn_w).reshape(bp, tp, d)
        outs_p.append((r, k, v, c))

        ys, *sample_states = _mix_sample_call(l, ys, ts, sinkrow, sret, kt, vt, state_conv, gpre, win, conv_w,
                                              wbr, wout, gpost, sample_states)
        ys = _ffn_call(l, ys, ps, *ffn_w)

    def stack(i, shape):
        return jnp.stack([o[i] for o in outs_p]).reshape(shape)

    def untranspose(a):
        return jnp.transpose(a.reshape(depth, bs, ATT_KV_HEADS, ATT_DH, win_w), (0, 1, 4, 2, 3))

    rs, kts, vts, cs = sample_states
    return (yp, ys.reshape(bs, ts, d),
            stack(0, (depth, bp, RET_HEADS, RET_DK, RET_DV)),
            stack(1, (depth, bp, WINDOW, ATT_KV_HEADS, ATT_DH)),
            stack(2, (depth, bp, WINDOW, ATT_KV_HEADS, ATT_DH)),
            stack(3, (depth, bp, CONV_W - 1, CONV_DIM)),
            rs.reshape(depth, bs, RET_HEADS, RET_DK, RET_DV), untranspose(kts), untranspose(vts), cs)
```

```python
import functools
import math

import numpy as np
import jax
import jax.numpy as jnp
from jax import lax
from jax.experimental import pallas as pl
from jax.experimental.pallas import tpu as pltpu

F32 = jnp.float32
BF16 = jnp.bfloat16

D_MODEL = 1024
BRANCH_W = 512
N_BRANCH = 3
RET_HEADS = 4
RET_DK = 64
RET_DV = 128
RET_W = RET_HEADS * RET_DK
ATT_HEADS = 8
ATT_KV_HEADS = 2
ATT_GROUP = ATT_HEADS // ATT_KV_HEADS
ATT_DH = 64
KV_W = ATT_KV_HEADS * ATT_DH
WINDOW = 128
CONV_DIM = 512
CONV_W = 3
D_FF = 4096
D_PLE = 256
EPS = 1e-6

OFF_RET = 0
OFF_ATT = 1536
OFF_CONV = 2304
OFF_GATE = 3840
N_IN = 6912

BLK = 128
BF16_ROWS = 16
NEG = -1e30
VMEM_LIMIT = 56 * 1024 * 1024

PROMPT_TILE = 512
ROW_SPLIT = 4
FFN_TILE = 512
FFN_CHUNK = 1024
SAMPLE_SEQS = 16
SEQ_GROUP = 4


def _dot(a, b):
    return jnp.dot(a, b, preferred_element_type=F32)


def _dot_nt(a, b):
    return lax.dot_general(a, b, (((1,), (1,)), ((), ())), preferred_element_type=F32)


def _rms(x, g):
    return x * lax.rsqrt(jnp.mean(x * x, axis=-1, keepdims=True) + EPS) * g


def _const_spec(shape):
    nd = len(shape)
    return pl.BlockSpec(shape, lambda *_: (0,) * nd, pipeline_mode=pl.Buffered(1))


def _layer_spec(layer, shape):
    nd = len(shape)
    return pl.BlockSpec((None,) + tuple(shape), lambda *_: (layer,) + (0,) * nd, pipeline_mode=pl.Buffered(1))


def _cast_specs(jobs, nsteps, step_of):
    in_specs, out_specs, out_shapes = [], [], []
    for src, layer, nrows in jobs:
        slab = nrows // nsteps
        assert nrows % nsteps == 0 and slab % BF16_ROWS == 0, (nrows, nsteps)
        cols = src.shape[1]
        in_specs.append(pl.BlockSpec((slab, cols), lambda *g, base=layer * nsteps: (base + step_of(*g), 0)))
        out_specs.append(pl.BlockSpec((slab, cols), lambda *g: (step_of(*g), 0)))
        out_shapes.append(jax.ShapeDtypeStruct((nrows, cols), BF16))
    return in_specs, out_specs, out_shapes


def _cast_slabs(src_refs, dst_refs):
    for src, dst in zip(src_refs, dst_refs):
        dst[...] = src[...].astype(BF16)


def _log_gamma():
    return np.log1p(-np.exp2(-5.0 - np.arange(RET_HEADS, dtype=np.float64)))


def _retention_consts(c):
    lg = _log_gamma()
    idx = np.arange(c, dtype=np.float64)
    diff = idx[:, None] - idx[None, :]
    dmask = np.where(diff >= 0, np.exp(lg[:, None, None] * np.maximum(diff, 0.0)), 0.0) * RET_DK ** -0.5
    qdec = np.repeat(np.exp(lg[:, None] * (idx + 1.0)).T, RET_DK, axis=1)
    kdec = np.repeat(np.exp(lg[:, None] * (c - 1.0 - idx)).T, RET_DK, axis=1) * RET_DK ** -0.5
    cdec = np.broadcast_to(np.repeat(np.exp(lg * c), RET_DK)[:, None], (RET_W, RET_DV))
    return dmask, qdec, kdec, cdec


def _slopes():
    return np.exp2(-8.0 * (np.arange(ATT_HEADS, dtype=np.float64) + 1.0) / ATT_HEADS)


def _prompt_attn_bias():
    s = np.arange(2 * BLK)[:, None]
    q = np.arange(BLK)[None, :]
    dist = q + BLK - s
    allowed = (dist >= 0) & (dist < WINDOW)
    per_head = np.where(allowed[None], -_slopes()[:, None, None] * dist[None], NEG)
    out = [np.concatenate([per_head[ATT_GROUP * j + par], per_head[ATT_GROUP * j + par + 2]], axis=1)
           for j in range(ATT_KV_HEADS) for par in range(2)]
    return np.stack(out)


def _sample_attn_bias(t):
    i = np.arange(t)[:, None]
    col = np.arange(2 * BLK)[None, :]
    new = col >= 2 * BLK - t
    dist = np.where(new, i - (col - (2 * BLK - t)), i + WINDOW - col)
    allowed = (dist >= 0) & (dist < WINDOW) & ((col < WINDOW) | new)
    b = np.where(allowed[None], -_slopes()[:, None, None] * dist[None], NEG)
    return b.reshape(ATT_HEADS * t, 2 * BLK)


N_FFN_IN = 9


def _ffn_body(n_cast, *refs):
    x_ref, p_ref, gpre_ref, w1_ref, w2_ref, gpost_ref, gple_ref, wg_ref, wp_ref = refs[:N_FFN_IN]
    o_ref = refs[N_FFN_IN + n_cast]
    _cast_slabs(refs[N_FFN_IN:N_FFN_IN + n_cast], refs[N_FFN_IN + n_cast + 1:])
    tile = x_ref.shape[0]
    nchunk = D_FF // FFN_CHUNK
    rows = [slice(i * tile // ROW_SPLIT, (i + 1) * tile // ROW_SPLIT) for i in range(ROW_SPLIT)]

    def up(hh, c):
        a = _dot(hh, w1_ref[:, c * FFN_CHUNK:(c + 1) * FFN_CHUNK])
        return jnp.square(jnp.maximum(a, 0.0)).astype(BF16)

    def down(a, c):
        return _dot(a, w2_ref[c * FFN_CHUNK:(c + 1) * FFN_CHUNK, :])

    h_parts = [_rms(x_ref[rw, :], gpre_ref[...]).astype(BF16) for rw in rows]
    a0 = jnp.concatenate([up(hp, 0) for hp in h_parts], axis=0)
    h = jnp.concatenate(h_parts, axis=0)
    acc = down(a0, 0)
    for c in range(1, nchunk - 1):
        acc = acc + down(up(h, c), c)
    a_last = up(h, nchunk - 1)
    f_parts = [acc[rw] + down(a_last[rw], nchunk - 1) for rw in rows]
    pp = _dot(p_ref[...].astype(BF16), wp_ref[...])
    for rw, f in zip(rows, f_parts):
        x2 = x_ref[rw, :] + _rms(f, gpost_ref[...])
        gate = jax.nn.sigmoid(_dot(_rms(x2, gple_ref[...]).astype(BF16), wg_ref[...]))
        o_ref[rw, :] = x2 + gate * pp[rw]


def _ffn_call(layer, x, p, gpre, w1, w2, gpost, gple, wg, wp, cast_jobs=()):
    n = x.shape[0]
    tile = min(FFN_TILE, n)
    row = lambda i: (i, 0)
    cast_in, cast_out, cast_shapes = _cast_specs(cast_jobs, n // tile, lambda i: i)
    return pl.pallas_call(
        functools.partial(_ffn_body, len(cast_jobs)),
        grid=(n // tile,),
        in_specs=[
            pl.BlockSpec((tile, D_MODEL), row),
            pl.BlockSpec((None, tile, D_PLE), lambda i: (layer, i, 0)),
            _layer_spec(layer, (1, D_MODEL)),
            _const_spec((D_MODEL, D_FF)),
            _const_spec((D_FF, D_MODEL)),
            _layer_spec(layer, (1, D_MODEL)),
            _layer_spec(layer, (1, D_MODEL)),
            _const_spec((D_MODEL, D_MODEL)),
            _const_spec((D_PLE, D_MODEL)),
        ] + cast_in,
        out_specs=[pl.BlockSpec((tile, D_MODEL), row)] + cast_out,
        out_shape=[jax.ShapeDtypeStruct((n, D_MODEL), F32)] + cast_shapes,
        compiler_params=pltpu.CompilerParams(
            dimension_semantics=("arbitrary",), vmem_limit_bytes=VMEM_LIMIT),
        name="ffn",
    )(x, p, gpre, w1, w2, gpost, gple, wg, wp, *[j[0] for j in cast_jobs])


def _gated_proj(h, win_ref, n, br, wbr_ref):
    gate = jax.nn.sigmoid(_dot(h, win_ref[:, OFF_GATE + n * D_MODEL:OFF_GATE + (n + 1) * D_MODEL]))
    return gate * _dot(br, wbr_ref[n])


N_PROMPT_IN = 13


def _mix_prompt_body(layer, n_cast, *refs):
    (sink_ref, x_ref, gpre_ref, win_ref, convw_ref, bias_ref, dmask_ref, qdec_ref, kdec_ref, cdec_ref,
     wbr_ref, wout_ref, gpost_ref) = refs[:N_PROMPT_IN]
    o_ref, ret_ref, wk_ref, wv_ref, conv_ref = refs[N_PROMPT_IN + n_cast:N_PROMPT_IN + n_cast + 5]
    br_scr, att_scr, u_scr = refs[N_PROMPT_IN + 2 * n_cast + 5:]
    _cast_slabs(refs[N_PROMPT_IN:N_PROMPT_IN + n_cast],
                refs[N_PROMPT_IN + n_cast + 5:N_PROMPT_IN + 2 * n_cast + 5])
    tt = x_ref.shape[1]
    nblk = tt // BLK
    first = pl.program_id(1) == 0

    @pl.when(first)
    def _init():
        ret_ref[...] = jnp.zeros_like(ret_ref)
        wk_ref[...] = jnp.zeros_like(wk_ref)
        wv_ref[...] = jnp.zeros_like(wv_ref)
        u_scr[0:8, :] = jnp.zeros((8, CONV_DIM), F32)

    lane256 = lax.broadcasted_iota(jnp.int32, (1, 256), 1)
    zero_bf = jnp.zeros((), BF16)

    rows = [slice(i * tt // ROW_SPLIT, (i + 1) * tt // ROW_SPLIT) for i in range(ROW_SPLIT)]
    h_parts = [_rms(x_ref[0, rw, :], gpre_ref[...]).astype(BF16) for rw in rows]
    zr = jnp.concatenate([_dot(hp, win_ref[:, OFF_RET:OFF_RET + 1536]) for hp in h_parts], axis=0)
    h = jnp.concatenate(h_parts, axis=0)

    head_masks = [(lane256 // RET_DK) == hd for hd in range(RET_HEADS)]

    def head_stack(a):
        return jnp.concatenate([jnp.where(hm, a, zero_bf) for hm in head_masks], axis=0)

    def inner_scores(c):
        r = slice(c * BLK, (c + 1) * BLK)
        return _dot_nt(head_stack(zr[r, 0:256].astype(BF16)), zr[r, 256:512].astype(BF16))

    gate_w = D_MODEL // nblk
    gate1_off = OFF_GATE + D_MODEL
    gate1_pre = []
    inner_next = inner_scores(0)
    for c in range(nblk):
        r = slice(c * BLK, (c + 1) * BLK)
        inner_raw = inner_next
        if c + 1 < nblk:
            inner_next = inner_scores(c + 1)
        gate1_pre.append(_dot(h, win_ref[:, gate1_off + c * gate_w:gate1_off + (c + 1) * gate_w]))
        qd = (zr[r, 0:256] * qdec_ref[...]).astype(BF16)
        kdt = (zr[r, 256:512] * kdec_ref[...]).T.astype(BF16)
        s_old = ret_ref[0]
        cross = _dot(head_stack(qd), s_old.astype(BF16))
        inner = (inner_raw * dmask_ref[...]).astype(BF16)
        for hd in range(RET_HEADS):
            hr = slice(hd * RET_DK, (hd + 1) * RET_DK)
            hq = slice(hd * BLK, (hd + 1) * BLK)
            vh = zr[r, 512 + hd * RET_DV:512 + (hd + 1) * RET_DV].astype(BF16)
            o = _dot(inner[hq], vh) + cross[hq]
            ret_ref[0, hr, :] = s_old[hr] * cdec_ref[hr, :] + _dot(kdt[hr], vh)
            o = o * lax.rsqrt(jnp.mean(o * o, axis=-1, keepdims=True) + EPS)
            g = zr[r, 1024 + hd * RET_DV:1024 + (hd + 1) * RET_DV]
            br_scr[r, hd * RET_DV:(hd + 1) * RET_DV] = (g * jax.nn.sigmoid(g) * o).astype(BF16)
    proj0 = _dot(br_scr[...], wbr_ref[0])

    za = _dot(h, win_ref[:, OFF_ATT:OFF_ATT + 768])
    low = lax.broadcasted_iota(jnp.int32, (1, 128), 1) < ATT_DH
    key_row = lax.broadcasted_iota(jnp.int32, (2 * BLK, 1), 0)
    no_prev = jnp.where(key_row < BLK, jnp.where(first, NEG, 0.0), 0.0).astype(F32)
    sink_rows = [jnp.where(lane256 < BLK, sink_ref[layer, ATT_GROUP * j + par],
                           sink_ref[layer, ATT_GROUP * j + par + 2])
                 for j in range(ATT_KV_HEADS) for par in range(2)]

    def key_variants(kblk):
        rolled = pltpu.roll(kblk, ATT_DH, 1)
        return [jnp.where(low, kblk, 0.0).astype(BF16), jnp.where(low, 0.0, rolled).astype(BF16),
                jnp.where(low, rolled, 0.0).astype(BF16), jnp.where(low, 0.0, kblk).astype(BF16)]

    def scores(i, k_prev):
        r = slice(i * BLK, (i + 1) * BLK)
        aq = (za[r, 0:512] * (ATT_DH ** -0.5)).astype(BF16)
        k_cur = key_variants(za[r, 512:640])
        out = []
        for j in range(ATT_KV_HEADS):
            wq = jnp.concatenate([aq[:, j * 256:j * 256 + 128], aq[:, j * 256 + 128:j * 256 + 256]], axis=0)
            for par in range(2):
                v = 2 * j + par
                out.append(_dot_nt(jnp.concatenate([k_prev[v], k_cur[v]], axis=0), wq))
        return out, k_cur

    cg_total = 1536 + D_MODEL
    cg_w = cg_total // nblk // 256 * 256
    cg_parts = []
    s_next, k_prev = scores(0, key_variants(wk_ref[0]))
    vt_prev = wv_ref[0].T.astype(BF16)
    for i in range(nblk):
        r = slice(i * BLK, (i + 1) * BLK)
        s_cur = s_next
        if i + 1 < nblk:
            s_next, k_prev = scores(i + 1, k_prev)
        cg_parts.append(_dot(h, win_ref[:, OFF_CONV + i * cg_w:OFF_CONV + (i + 1) * cg_w]))
        ps, rdens = [], []
        for v in range(2 * ATT_KV_HEADS):
            s = s_cur[v] + bias_ref[v]
            if i == 0:
                s = s + no_prev
            m = jnp.maximum(jnp.max(s, axis=0, keepdims=True), sink_rows[v])
            p = jnp.exp(s - m)
            rdens.append(1.0 / (jnp.sum(p, axis=0, keepdims=True) + jnp.exp(sink_rows[v] - m)))
            ps.append(p.astype(BF16))
        vt_cur = za[r, 640:768].T.astype(BF16)
        vt2 = jnp.concatenate([vt_prev, vt_cur], axis=1)
        heads = [None] * ATT_HEADS
        for j in range(ATT_KV_HEADS):
            for par in range(2):
                v = 2 * j + par
                ot = _dot(vt2, ps[v])[j * ATT_DH:(j + 1) * ATT_DH] * rdens[v]
                heads[ATT_GROUP * j + par] = ot[:, 0:BLK]
                heads[ATT_GROUP * j + par + 2] = ot[:, BLK:2 * BLK]
        att_scr[r, :] = jnp.concatenate(heads, axis=0).T.astype(BF16)
        vt_prev = vt_cur
    wk_ref[0] = za[tt - BLK:tt, 512:640]
    wv_ref[0] = za[tt - BLK:tt, 640:768]
    if nblk * cg_w < cg_total:
        cg_parts.append(_dot(h, win_ref[:, OFF_CONV + nblk * cg_w:OFF_CONV + cg_total]))
    cg = jnp.concatenate(cg_parts, axis=1)
    mixed = jax.nn.sigmoid(cg[:, 1536:cg_total]) * proj0
    mixed = mixed + jax.nn.sigmoid(jnp.concatenate(gate1_pre, axis=1)) * _dot(att_scr[...], wbr_ref[1])

    zc = cg[:, 0:1536]
    u = zc[:, 512:1024] * zc[:, 1024:1536]
    u_scr[8:8 + tt, :] = u
    y = (u_scr[6:6 + tt, :] * convw_ref[0:1, :] + u_scr[7:7 + tt, :] * convw_ref[1:2, :]
         + u * convw_ref[2:3, :])
    conv_out = (zc[:, 0:512] * y).astype(BF16)
    tail = u[tt - 2:tt]
    conv_ref[0] = tail
    u_scr[6:8, :] = tail
    mixed = mixed + _gated_proj(h, win_ref, 2, conv_out, wbr_ref)

    mixed = mixed.astype(BF16)
    for rw in rows:
        o_ref[0, rw, :] = x_ref[0, rw, :] + _rms(_dot(mixed[rw], wout_ref[...]), gpost_ref[...])


def _mix_prompt_call(layer, x, sinks, gpre, win, convw, wbr, wout, gpost, cast_jobs=()):
    b, t, _ = x.shape
    tt = min(PROMPT_TILE, t)
    nt = t // tt
    cast_in, cast_out, cast_shapes = _cast_specs(cast_jobs, b * nt, lambda bi, ti: bi * nt + ti)
    dmask, qdec, kdec, cdec = (jnp.asarray(a, F32) for a in _retention_consts(BLK))
    dmask = dmask.reshape(RET_HEADS * BLK, BLK)
    bias = jnp.asarray(_prompt_attn_bias(), F32)
    per_b = lambda bi, ti: (bi, 0, 0)
    return pl.pallas_call(
        functools.partial(_mix_prompt_body, layer, len(cast_jobs)),
        grid=(b, nt),
        in_specs=[
            pl.BlockSpec(memory_space=pltpu.SMEM),
            pl.BlockSpec((1, tt, D_MODEL), lambda bi, ti: (bi, ti, 0)),
            _layer_spec(layer, (1, D_MODEL)),
            _const_spec((D_MODEL, N_IN)),
            _layer_spec(layer, (CONV_W, CONV_DIM)),
            _const_spec(bias.shape),
            _const_spec(dmask.shape),
            _const_spec(qdec.shape),
            _const_spec(kdec.shape),
            _const_spec(cdec.shape),
            _const_spec((N_BRANCH, BRANCH_W, D_MODEL)),
            _const_spec((D_MODEL, D_MODEL)),
            _layer_spec(layer, (1, D_MODEL)),
        ] + cast_in,
        out_specs=[
            pl.BlockSpec((1, tt, D_MODEL), lambda bi, ti: (bi, ti, 0)),
            pl.BlockSpec((1, RET_W, RET_DV), per_b),
            pl.BlockSpec((1, WINDOW, KV_W), per_b),
            pl.BlockSpec((1, WINDOW, KV_W), per_b),
            pl.BlockSpec((1, CONV_W - 1, CONV_DIM), per_b),
        ] + cast_out,
        out_shape=[
            jax.ShapeDtypeStruct((b, t, D_MODEL), F32),
            jax.ShapeDtypeStruct((b, RET_W, RET_DV), F32),
            jax.ShapeDtypeStruct((b, WINDOW, KV_W), F32),
            jax.ShapeDtypeStruct((b, WINDOW, KV_W), F32),
            jax.ShapeDtypeStruct((b, CONV_W - 1, CONV_DIM), F32),
        ] + cast_shapes,
        scratch_shapes=[
            pltpu.VMEM((tt, BRANCH_W), BF16),
            pltpu.VMEM((tt, BRANCH_W), BF16),
            pltpu.VMEM((tt + 8, CONV_DIM), F32),
        ],
        compiler_params=pltpu.CompilerParams(
            dimension_semantics=("arbitrary", "arbitrary"), vmem_limit_bytes=VMEM_LIMIT),
        name="mix_prompt",
    )(sinks, x, gpre, win, convw, bias, dmask, qdec, kdec, cdec, wbr, wout, gpost, *[j[0] for j in cast_jobs])


N_SAMPLE_IN = 17
N_SAMPLE_STATE = 4


def _mix_sample_body(n_alias, *refs):
    (x_ref, sret_ref, kt_ref, vt_ref, sconv_ref, gpre_ref, win_ref, convw_ref, bias_ref, sinkrow_ref,
     dmask_ref, qdec_ref, kdec_ref, cdec_ref, wbr_ref, wout_ref, gpost_ref) = refs[:N_SAMPLE_IN]
    o_ref, ret_ref, wk_ref, wv_ref, conv_ref = refs[N_SAMPLE_IN + n_alias:N_SAMPLE_IN + n_alias + 5]
    z_scr, oc_scr, att_scr, cv_scr, br_scr, u_scr = refs[N_SAMPLE_IN + n_alias + 5:]
    rows = x_ref.shape[0]
    nseq = ret_ref.shape[0]
    t = rows // nseq

    x = x_ref[...]
    h = _rms(x, gpre_ref[...]).astype(BF16)
    lane256 = lax.broadcasted_iota(jnp.int32, (1, 256), 1)
    lane128 = lax.broadcasted_iota(jnp.int32, (1, 128), 1)
    zero_bf = jnp.zeros((), BF16)
    head_masks = [(lane256 // RET_DK) == hd for hd in range(RET_HEADS)]
    new_lanes = lane128 >= BLK - t
    zpad = jnp.zeros((BLK - t, KV_W), F32)
    snk = sinkrow_ref[:, 0:1]

    z_scr[...] = _dot(h, win_ref[:, 0:OFF_GATE])

    def new_rows_t(rs, c0):
        return jnp.concatenate([zpad, z_scr[rs, c0:c0 + KV_W]], axis=0).T

    for g0 in range(0, nseq, SEQ_GROUP):
        seqs = list(range(g0, min(g0 + SEQ_GROUP, nseq)))
        rsl = {b: slice(b * t, (b + 1) * t) for b in seqs}

        for b in seqs:
            rs = rsl[b]
            qd = z_scr[rs, 0:256] * qdec_ref[...]
            qm = jnp.concatenate([jnp.where(hm, qd, 0.0) for hm in head_masks], axis=0).astype(BF16)
            s_old = sret_ref[b]
            oc = _dot(qm, s_old.astype(BF16))
            for hd in range(RET_HEADS):
                oc_scr[hd, rs, :] = oc[hd * t:(hd + 1) * t]
            kdt = (z_scr[rs, 256:512] * kdec_ref[...]).T.astype(BF16)
            vb = z_scr[rs, 512:1024].astype(BF16)
            for hd in range(RET_HEADS):
                hr = slice(hd * RET_DK, (hd + 1) * RET_DK)
                ret_ref[b, hr, :] = (s_old[hr] * cdec_ref[hr, :]
                                     + _dot(kdt[hr], vb[:, hd * RET_DV:(hd + 1) * RET_DV]))

        s_raw, nkt = {}, {}
        for b in seqs:
            rs = rsl[b]
            aq = z_scr[rs, 1536:2048] * (ATT_DH ** -0.5)
            pieces = []
            for hd in range(ATT_HEADS):
                piece = aq[:, (hd // 2) * 128:(hd // 2 + 1) * 128]
                if hd % 2 != hd // ATT_GROUP:
                    piece = pltpu.roll(piece, ATT_DH, 1)
                pieces.append(jnp.where((lane128 // ATT_DH) == hd // ATT_GROUP, piece, 0.0))
            qs = jnp.concatenate(pieces, axis=0).astype(BF16)
            nkt[b] = new_rows_t(rs, 2048)
            k2t = jnp.concatenate([kt_ref[b], nkt[b]], axis=1).astype(BF16)
            s_raw[b] = _dot(qs, k2t)

        probs = {}
        for b in seqs:
            s = s_raw[b] + bias_ref[...]
            m = jnp.maximum(jnp.max(s, axis=-1, keepdims=True), snk)
            p = jnp.exp(s - m)
            rden = 1.0 / (jnp.sum(p, axis=-1, keepdims=True) + jnp.exp(snk - m))
            probs[b] = (p * rden).astype(BF16)

        for b in seqs:
            rs = rsl[b]
            nvt = new_rows_t(rs, 2176)
            v2t = jnp.concatenate([vt_ref[b], nvt], axis=1).astype(BF16)
            o = _dot_nt(probs[b], v2t)
            for pr in range(ATT_HEADS // 2):
                lo = o[(2 * pr) * t:(2 * pr + 1) * t]
                hi = o[(2 * pr + 1) * t:(2 * pr + 2) * t]
                if (2 * pr) // ATT_GROUP == 1:
                    lo = pltpu.roll(lo, ATT_DH, 1)
                if (2 * pr + 1) // ATT_GROUP == 0:
                    hi = pltpu.roll(hi, ATT_DH, 1)
                att_scr[rs, pr * 128:(pr + 1) * 128] = jnp.where(lane128 < ATT_DH, lo, hi)
            wk_ref[b] = jnp.where(new_lanes, nkt[b], pltpu.roll(kt_ref[b], BLK - t, 1))
            wv_ref[b] = jnp.where(new_lanes, nvt, pltpu.roll(vt_ref[b], BLK - t, 1))

        for b in seqs:
            rs = rsl[b]
            u = z_scr[rs, 2816:3328] * z_scr[rs, 3328:3840]
            u_scr[b, 6:8, :] = sconv_ref[b]
            u_scr[b, 8:8 + t, :] = u
            y = (u_scr[b, 6:6 + t, :] * convw_ref[0:1, :] + u_scr[b, 7:7 + t, :] * convw_ref[1:2, :]
                 + u * convw_ref[2:3, :])
            cv_scr[rs, :] = z_scr[rs, 2304:2816] * y
            conv_ref[b] = u_scr[b, 6 + t:8 + t, :]

    qb = z_scr[:, 0:256].astype(BF16)
    kb = z_scr[:, 256:512].astype(BF16)
    for hd in range(RET_HEADS):
        inner = _dot_nt(jnp.where(head_masks[hd], qb, zero_bf), kb) * dmask_ref[hd]
        vh = z_scr[:, 512 + hd * RET_DV:512 + (hd + 1) * RET_DV].astype(BF16)
        o = _dot(inner.astype(BF16), vh) + oc_scr[hd]
        o = o * lax.rsqrt(jnp.mean(o * o, axis=-1, keepdims=True) + EPS)
        g = z_scr[:, 1024 + hd * RET_DV:1024 + (hd + 1) * RET_DV]
        br_scr[:, hd * RET_DV:(hd + 1) * RET_DV] = (g * jax.nn.sigmoid(g) * o).astype(BF16)
    mixed = _gated_proj(h, win_ref, 0, br_scr[...], wbr_ref)
    mixed = mixed + _gated_proj(h, win_ref, 1, att_scr[...].astype(BF16), wbr_ref)
    mixed = mixed + _gated_proj(h, win_ref, 2, cv_scr[...].astype(BF16), wbr_ref)
    o_ref[...] = x + _rms(_dot(mixed.astype(BF16), wout_ref[...]), gpost_ref[...])


def _mix_sample_call(layer, x2, t, sinkrow, sret, kt, vt, sconv, gpre, win, convw, wbr, wout, gpost, prev):
    depth, nb = sret.shape[0], sret.shape[1]
    sb = min(SAMPLE_SEQS, nb)
    rows = sb * t
    dm, qd, kd, cdec = _retention_consts(t)
    same_seq = np.kron(np.eye(sb), np.ones((t, t)))
    dmask = jnp.asarray(np.tile(dm, (1, sb, sb)) * same_seq[None], F32)
    qdec = jnp.asarray(qd, F32)
    kdec = jnp.asarray(kd, F32)
    cdec = jnp.asarray(cdec, F32)
    bias = jnp.asarray(_sample_attn_bias(t), F32)
    row = lambda i: (i, 0)
    seq = lambda i: (layer, i, 0, 0)
    state_shapes = [(RET_W, RET_DV), (KV_W, WINDOW), (KV_W, WINDOW), (CONV_W - 1, CONV_DIM)]
    state_specs = [pl.BlockSpec((None, sb) + s, seq) for s in state_shapes]
    n_alias = 0 if prev is None else N_SAMPLE_STATE
    alias_args = [] if prev is None else list(prev)
    return pl.pallas_call(
        functools.partial(_mix_sample_body, n_alias),
        grid=(nb // sb,),
        in_specs=[pl.BlockSpec((rows, D_MODEL), row)] + state_specs + [
            _layer_spec(layer, (1, D_MODEL)),
            _const_spec((D_MODEL, N_IN)),
            _layer_spec(layer, (CONV_W, CONV_DIM)),
            _const_spec(bias.shape),
            _layer_spec(layer, (ATT_HEADS * t, 128)),
            _const_spec(dmask.shape),
            _const_spec(qdec.shape),
            _const_spec(kdec.shape),
            _const_spec(cdec.shape),
            _const_spec((N_BRANCH, BRANCH_W, D_MODEL)),
            _const_spec((D_MODEL, D_MODEL)),
            _layer_spec(layer, (1, D_MODEL)),
        ] + [pl.BlockSpec(memory_space=pl.ANY)] * n_alias,
        out_specs=[pl.BlockSpec((rows, D_MODEL), row)] + state_specs,
        out_shape=[jax.ShapeDtypeStruct((nb * t, D_MODEL), F32)]
        + [jax.ShapeDtypeStruct((depth, nb) + s, F32) for s in state_shapes],
        input_output_aliases={N_SAMPLE_IN + k: 1 + k for k in range(n_alias)},
        scratch_shapes=[
            pltpu.VMEM((rows, OFF_GATE), F32),
            pltpu.VMEM((RET_HEADS, rows, RET_DV), F32),
            pltpu.VMEM((rows, BRANCH_W), F32),
            pltpu.VMEM((rows, BRANCH_W), F32),
            pltpu.VMEM((rows, BRANCH_W), BF16),
            pltpu.VMEM((sb, 8 + t, CONV_DIM), F32),
        ],
        compiler_params=pltpu.CompilerParams(
            dimension_semantics=("arbitrary",), vmem_limit_bytes=VMEM_LIMIT),
        name="mix_sample",
    )(x2, sret, kt, vt, sconv, gpre, win, convw, bias, sinkrow, dmask, qdec, kdec, cdec, wbr, wout, gpost,
      *alias_args)


def kernel(x_prompt, x_sample, p_prompt, p_sample, state_ret, cache_win_k, cache_win_v, state_conv,
           g_mix_pre, w_in, conv_w, attn_sinks, w_branch, w_out, g_mix_post,
           g_ffn_pre, w_ff1, w_ff2, g_ffn_post, g_ple, w_ple_gate, w_ple_proj):
    depth = w_in.shape[0]
    bp, tp, d = x_prompt.shape
    bs, ts, _ = x_sample.shape
    win_w = cache_win_k.shape[2]
    assert d == D_MODEL and w_in.shape[2] == N_IN and tp % BLK == 0 and math.gcd(ts, BLK) == ts
    assert win_w == WINDOW == BLK and ts <= BLK

    gain = lambda a: a.reshape(depth, 1, D_MODEL)
    gpre, gpost = gain(g_mix_pre), gain(g_mix_post)
    gfpre, gfpost, gple = gain(g_ffn_pre), gain(g_ffn_post), gain(g_ple)
    rows2d = lambda a: a.reshape(-1, a.shape[-1])
    mix_w = (w_in[0].astype(BF16), w_branch[0].astype(BF16), w_out[0].astype(BF16))
    wp_all = w_ple_proj.astype(BF16)
    pp = p_prompt.reshape(depth, bp * tp, D_PLE)
    ps = p_sample.reshape(depth, bs * ts, D_PLE)
    sret = state_ret.reshape(depth, bs, RET_W, RET_DV)
    kt = jnp.transpose(cache_win_k, (0, 1, 3, 4, 2)).reshape(depth, bs, KV_W, win_w)
    vt = jnp.transpose(cache_win_v, (0, 1, 3, 4, 2)).reshape(depth, bs, KV_W, win_w)
    sinkrow = jnp.broadcast_to(jnp.repeat(attn_sinks, ts, axis=1)[:, :, None], (depth, ATT_HEADS * ts, 128))

    yp = x_prompt
    ys = x_sample.reshape(bs * ts, d)
    outs_p, sample_states = [], None
    for l in range(depth):
        win, wbr, wout = mix_w
        ffn_jobs = [(rows2d(w_ff1), l, D_MODEL), (rows2d(w_ff2), l, D_FF), (rows2d(w_ple_gate), l, D_MODEL)]
        yp, r, k, v, c, w1, w2, wg = _mix_prompt_call(l, yp, attn_sinks, gpre, win, conv_w, wbr, wout, gpost,
                                                      ffn_jobs)
        outs_p.append((r, k, v, c))
        ffn_w = (gfpre, w1, w2, gfpost, gple, wg, wp_all[l])
        next_jobs = [] if l + 1 == depth else [(rows2d(w_in), l + 1, D_MODEL),
                                               (rows2d(w_branch), l + 1, N_BRANCH * BRANCH_W),
                                               (rows2d(w_out), l + 1, D_MODEL)]
        yp, *next_w = _ffn_call(l, yp.reshape(bp * tp, d), pp, *ffn_w, next_jobs)
        yp = yp.reshape(bp, tp, d)

        ys, *sample_states = _mix_sample_call(l, ys, ts, sinkrow, sret, kt, vt, state_conv, gpre, win, conv_w,
                                              wbr, wout, gpost, sample_states)
        ys, = _ffn_call(l, ys, ps, *ffn_w)
        if next_w:
            mix_w = (next_w[0], next_w[1].reshape(N_BRANCH, BRANCH_W, D_MODEL), next_w[2])

    def stack(i, shape):
        return jnp.stack([o[i] for o in outs_p]).reshape(shape)

    def untranspose(a):
        return jnp.transpose(a.reshape(depth, bs, ATT_KV_HEADS, ATT_DH, win_w), (0, 1, 4, 2, 3))

    rs, kts, vts, cs = sample_states
    return (yp, ys.reshape(bs, ts, d),
            stack(0, (depth, bp, RET_HEADS, RET_DK, RET_DV)),
            stack(1, (depth, bp, WINDOW, ATT_KV_HEADS, ATT_DH)),
            stack(2, (depth, bp, WINDOW, ATT_KV_HEADS, ATT_DH)),
            stack(3, (depth, bp, CONV_W - 1, CONV_DIM)),
            rs.reshape(depth, bs, RET_HEADS, RET_DK, RET_DV), untranspose(kts), untranspose(vts), cs)
```

```python
import functools
import math

import numpy as np
import jax
import jax.numpy as jnp
from jax import lax
from jax.experimental import pallas as pl
from jax.experimental.pallas import tpu as pltpu

F32 = jnp.float32
BF16 = jnp.bfloat16

D_MODEL = 1024
BRANCH_W = 512
N_BRANCH = 3
RET_HEADS = 4
RET_DK = 64
RET_DV = 128
RET_W = RET_HEADS * RET_DK
ATT_HEADS = 8
ATT_KV_HEADS = 2
ATT_GROUP = ATT_HEADS // ATT_KV_HEADS
ATT_DH = 64
KV_W = ATT_KV_HEADS * ATT_DH
WINDOW = 128
CONV_DIM = 512
CONV_W = 3
D_FF = 4096
D_PLE = 256
EPS = 1e-6

OFF_RET = 0
OFF_ATT = 1536
OFF_CONV = 2304
OFF_GATE = 3840
N_IN = 6912

BLK = 128
BF16_ROWS = 16
LOG2E = math.log2(math.e)
NEG = -1e30
VMEM_LIMIT = 56 * 1024 * 1024

PROMPT_TILE = 512
ROW_SPLIT = 4
FFN_TILE = 512
FFN_CHUNK = 1024
SAMPLE_SEQS = 16
SEQ_GROUP = 4


def _dot(a, b):
    return jnp.dot(a, b, preferred_element_type=F32)


def _dot_nt(a, b):
    return lax.dot_general(a, b, (((1,), (1,)), ((), ())), preferred_element_type=F32)


def _rms(x, g):
    return x * lax.rsqrt(jnp.mean(x * x, axis=-1, keepdims=True) + EPS) * g


def _const_spec(shape):
    nd = len(shape)
    return pl.BlockSpec(shape, lambda *_: (0,) * nd, pipeline_mode=pl.Buffered(1))


def _layer_spec(layer, shape):
    nd = len(shape)
    return pl.BlockSpec((None,) + tuple(shape), lambda *_: (layer,) + (0,) * nd, pipeline_mode=pl.Buffered(1))


def _cast_specs(jobs, nsteps, step_of):
    in_specs, out_specs, out_shapes = [], [], []
    for src, layer, nrows in jobs:
        slab = nrows // nsteps
        assert nrows % nsteps == 0 and slab % BF16_ROWS == 0, (nrows, nsteps)
        cols = src.shape[1]
        in_specs.append(pl.BlockSpec((slab, cols), lambda *g, base=layer * nsteps: (base + step_of(*g), 0)))
        out_specs.append(pl.BlockSpec((slab, cols), lambda *g: (step_of(*g), 0)))
        out_shapes.append(jax.ShapeDtypeStruct((nrows, cols), BF16))
    return in_specs, out_specs, out_shapes


def _cast_slabs(src_refs, dst_refs):
    for src, dst in zip(src_refs, dst_refs):
        dst[...] = src[...].astype(BF16)


def _log_gamma():
    return np.log1p(-np.exp2(-5.0 - np.arange(RET_HEADS, dtype=np.float64)))


def _retention_consts(c):
    lg = _log_gamma()
    idx = np.arange(c, dtype=np.float64)
    diff = idx[:, None] - idx[None, :]
    dmask = np.where(diff >= 0, np.exp(lg[:, None, None] * np.maximum(diff, 0.0)), 0.0) * RET_DK ** -0.5
    qdec = np.repeat(np.exp(lg[:, None] * (idx + 1.0)).T, RET_DK, axis=1)
    kdec = np.repeat(np.exp(lg[:, None] * (c - 1.0 - idx)).T, RET_DK, axis=1) * RET_DK ** -0.5
    cdec = np.broadcast_to(np.repeat(np.exp(lg * c), RET_DK)[:, None], (RET_W, RET_DV))
    return dmask, qdec, kdec, cdec


def _slopes():
    return np.exp2(-8.0 * (np.arange(ATT_HEADS, dtype=np.float64) + 1.0) / ATT_HEADS)


def _prompt_attn_bias():
    s = np.arange(2 * BLK)[:, None]
    q = np.arange(BLK)[None, :]
    dist = q + BLK - s
    allowed = (dist >= 0) & (dist < WINDOW)
    per_head = np.where(allowed[None], -_slopes()[:, None, None] * dist[None] * LOG2E, NEG)
    out = [np.concatenate([per_head[ATT_GROUP * j + par], per_head[ATT_GROUP * j + par + 2]], axis=1)
           for j in range(ATT_KV_HEADS) for par in range(2)]
    return np.stack(out)


def _sample_attn_bias(t):
    i = np.arange(t)[:, None]
    col = np.arange(2 * BLK)[None, :]
    new = col >= 2 * BLK - t
    dist = np.where(new, i - (col - (2 * BLK - t)), i + WINDOW - col)
    allowed = (dist >= 0) & (dist < WINDOW) & ((col < WINDOW) | new)
    b = np.where(allowed[None], -_slopes()[:, None, None] * dist[None], NEG)
    return b.reshape(ATT_HEADS * t, 2 * BLK)


N_FFN_IN = 9


def _ffn_body(n_cast, *refs):
    x_ref, p_ref, gpre_ref, w1_ref, w2_ref, gpost_ref, gple_ref, wg_ref, wp_ref = refs[:N_FFN_IN]
    o_ref = refs[N_FFN_IN + n_cast]
    _cast_slabs(refs[N_FFN_IN:N_FFN_IN + n_cast], refs[N_FFN_IN + n_cast + 1:])
    tile = x_ref.shape[0]
    nchunk = D_FF // FFN_CHUNK
    rows = [slice(i * tile // ROW_SPLIT, (i + 1) * tile // ROW_SPLIT) for i in range(ROW_SPLIT)]

    def up(hh, c):
        a = _dot(hh, w1_ref[:, c * FFN_CHUNK:(c + 1) * FFN_CHUNK])
        return jnp.square(jnp.maximum(a, 0.0)).astype(BF16)

    def down(a, c):
        return _dot(a, w2_ref[c * FFN_CHUNK:(c + 1) * FFN_CHUNK, :])

    h_parts = [_rms(x_ref[rw, :], gpre_ref[...]).astype(BF16) for rw in rows]
    a0 = jnp.concatenate([up(hp, 0) for hp in h_parts], axis=0)
    h = jnp.concatenate(h_parts, axis=0)
    acc = down(a0, 0)
    for c in range(1, nchunk - 1):
        acc = acc + down(up(h, c), c)
    a_last = up(h, nchunk - 1)
    f_parts = [acc[rw] + down(a_last[rw], nchunk - 1) for rw in rows]
    pp = _dot(p_ref[...].astype(BF16), wp_ref[...])
    for rw, f in zip(rows, f_parts):
        x2 = x_ref[rw, :] + _rms(f, gpost_ref[...])
        gate = jax.nn.sigmoid(_dot(_rms(x2, gple_ref[...]).astype(BF16), wg_ref[...]))
        o_ref[rw, :] = x2 + gate * pp[rw]


def _ffn_call(layer, x, p, gpre, w1, w2, gpost, gple, wg, wp, cast_jobs=()):
    n = x.shape[0]
    tile = min(FFN_TILE, n)
    row = lambda i: (i, 0)
    cast_in, cast_out, cast_shapes = _cast_specs(cast_jobs, n // tile, lambda i: i)
    return pl.pallas_call(
        functools.partial(_ffn_body, len(cast_jobs)),
        grid=(n // tile,),
        in_specs=[
            pl.BlockSpec((tile, D_MODEL), row),
            pl.BlockSpec((None, tile, D_PLE), lambda i: (layer, i, 0)),
            _layer_spec(layer, (1, D_MODEL)),
            _const_spec((D_MODEL, D_FF)),
            _const_spec((D_FF, D_MODEL)),
            _layer_spec(layer, (1, D_MODEL)),
            _layer_spec(layer, (1, D_MODEL)),
            _const_spec((D_MODEL, D_MODEL)),
            _const_spec((D_PLE, D_MODEL)),
        ] + cast_in,
        out_specs=[pl.BlockSpec((tile, D_MODEL), row)] + cast_out,
        out_shape=[jax.ShapeDtypeStruct((n, D_MODEL), F32)] + cast_shapes,
        compiler_params=pltpu.CompilerParams(
            dimension_semantics=("arbitrary",), vmem_limit_bytes=VMEM_LIMIT),
        name="ffn",
    )(x, p, gpre, w1, w2, gpost, gple, wg, wp, *[j[0] for j in cast_jobs])


def _gated_proj(h, win_ref, n, br, wbr_ref):
    gate = jax.nn.sigmoid(_dot(h, win_ref[:, OFF_GATE + n * D_MODEL:OFF_GATE + (n + 1) * D_MODEL]))
    return gate * _dot(br, wbr_ref[n])


N_PROMPT_IN = 13


def _mix_prompt_body(layer, n_cast, *refs):
    (sink_ref, x_ref, gpre_ref, win_ref, convw_ref, bias_ref, dmask_ref, qdec_ref, kdec_ref, cdec_ref,
     wbr_ref, wout_ref, gpost_ref) = refs[:N_PROMPT_IN]
    o_ref, ret_ref, wk_ref, wv_ref, conv_ref = refs[N_PROMPT_IN + n_cast:N_PROMPT_IN + n_cast + 5]
    br_scr, att_scr, u_scr = refs[N_PROMPT_IN + 2 * n_cast + 5:]
    _cast_slabs(refs[N_PROMPT_IN:N_PROMPT_IN + n_cast],
                refs[N_PROMPT_IN + n_cast + 5:N_PROMPT_IN + 2 * n_cast + 5])
    tt = x_ref.shape[1]
    nblk = tt // BLK
    first = pl.program_id(1) == 0

    @pl.when(first)
    def _init():
        ret_ref[...] = jnp.zeros_like(ret_ref)
        wk_ref[...] = jnp.zeros_like(wk_ref)
        wv_ref[...] = jnp.zeros_like(wv_ref)
        u_scr[0:8, :] = jnp.zeros((8, CONV_DIM), F32)

    lane256 = lax.broadcasted_iota(jnp.int32, (1, 256), 1)
    zero_bf = jnp.zeros((), BF16)

    rows = [slice(i * tt // ROW_SPLIT, (i + 1) * tt // ROW_SPLIT) for i in range(ROW_SPLIT)]
    h_parts = [_rms(x_ref[0, rw, :], gpre_ref[...]).astype(BF16) for rw in rows]
    zr = jnp.concatenate([_dot(hp, win_ref[:, OFF_RET:OFF_RET + 1536]) for hp in h_parts], axis=0)
    h = jnp.concatenate(h_parts, axis=0)

    def gate_slice(n, c, width):
        c0 = OFF_GATE + n * D_MODEL + c * width
        return jax.nn.sigmoid(_dot(h, win_ref[:, c0:c0 + width]))

    zc = _dot(h, win_ref[:, OFF_CONV:OFF_CONV + 1536])
    u = zc[:, 512:1024] * zc[:, 1024:1536]
    u_scr[8:8 + tt, :] = u
    y = (u_scr[6:6 + tt, :] * convw_ref[0:1, :] + u_scr[7:7 + tt, :] * convw_ref[1:2, :]
         + u * convw_ref[2:3, :])
    conv_out = (zc[:, 0:512] * y).astype(BF16)
    tail = u[tt - 2:tt]
    conv_ref[0] = tail
    u_scr[6:8, :] = tail

    head_masks = [(lane256 // RET_DK) == hd for hd in range(RET_HEADS)]

    def head_stack(a):
        return jnp.concatenate([jnp.where(hm, a, zero_bf) for hm in head_masks], axis=0)

    def inner_scores(c):
        r = slice(c * BLK, (c + 1) * BLK)
        return _dot_nt(head_stack(zr[r, 0:256].astype(BF16)), zr[r, 256:512].astype(BF16))

    gate_w = D_MODEL // nblk
    gate0, gate2 = [], []
    inner_next = inner_scores(0)
    for c in range(nblk):
        r = slice(c * BLK, (c + 1) * BLK)
        inner_raw = inner_next
        if c + 1 < nblk:
            inner_next = inner_scores(c + 1)
        gate0.append(gate_slice(0, c, gate_w))
        gate2.append(gate_slice(2, c, gate_w))
        qd = (zr[r, 0:256] * qdec_ref[...]).astype(BF16)
        kdt = (zr[r, 256:512] * kdec_ref[...]).T.astype(BF16)
        s_old = ret_ref[0]
        cross = _dot(head_stack(qd), s_old.astype(BF16))
        inner = (inner_raw * dmask_ref[...]).astype(BF16)
        for hd in range(RET_HEADS):
            hr = slice(hd * RET_DK, (hd + 1) * RET_DK)
            hq = slice(hd * BLK, (hd + 1) * BLK)
            vh = zr[r, 512 + hd * RET_DV:512 + (hd + 1) * RET_DV].astype(BF16)
            o = _dot(inner[hq], vh) + cross[hq]
            ret_ref[0, hr, :] = s_old[hr] * cdec_ref[hr, :] + _dot(kdt[hr], vh)
            o = o * lax.rsqrt(jnp.mean(o * o, axis=-1, keepdims=True) + EPS)
            g = zr[r, 1024 + hd * RET_DV:1024 + (hd + 1) * RET_DV]
            br_scr[r, hd * RET_DV:(hd + 1) * RET_DV] = (g * jax.nn.sigmoid(g) * o).astype(BF16)
    mixed = (jnp.concatenate(gate0, axis=1) * _dot(br_scr[...], wbr_ref[0])
             + jnp.concatenate(gate2, axis=1) * _dot(conv_out, wbr_ref[2]))

    za = _dot(h, win_ref[:, OFF_ATT:OFF_ATT + 768])
    low = lax.broadcasted_iota(jnp.int32, (1, 128), 1) < ATT_DH
    key_row = lax.broadcasted_iota(jnp.int32, (2 * BLK, 1), 0)
    no_prev = jnp.where(key_row < BLK, jnp.where(first, NEG, 0.0), 0.0).astype(F32)
    sink_rows = [jnp.where(lane256 < BLK, sink_ref[layer, ATT_GROUP * j + par] * LOG2E,
                           sink_ref[layer, ATT_GROUP * j + par + 2] * LOG2E)
                 for j in range(ATT_KV_HEADS) for par in range(2)]
    top = lax.broadcasted_iota(jnp.int32, (KV_W, 1), 0) < ATT_DH

    def value_variants(vblk):
        vt = vblk.T
        return [jnp.where(top, vt, 1.0).astype(BF16), jnp.where(top, 1.0, vt).astype(BF16)]

    def key_variants(kblk):
        rolled = pltpu.roll(kblk, ATT_DH, 1)
        return [jnp.where(low, kblk, 0.0).astype(BF16), jnp.where(low, 0.0, rolled).astype(BF16),
                jnp.where(low, rolled, 0.0).astype(BF16), jnp.where(low, 0.0, kblk).astype(BF16)]

    def scores(i, k_prev):
        r = slice(i * BLK, (i + 1) * BLK)
        aq = (za[r, 0:512] * (ATT_DH ** -0.5 * LOG2E)).astype(BF16)
        k_cur = key_variants(za[r, 512:640])
        out = []
        for j in range(ATT_KV_HEADS):
            wq = jnp.concatenate([aq[:, j * 256:j * 256 + 128], aq[:, j * 256 + 128:j * 256 + 256]], axis=0)
            for par in range(2):
                v = 2 * j + par
                out.append(_dot_nt(jnp.concatenate([k_prev[v], k_cur[v]], axis=0), wq))
        return out, k_cur

    gate1 = []
    s_next, k_prev = scores(0, key_variants(wk_ref[0]))
    vt_prev = value_variants(wv_ref[0])
    for i in range(nblk):
        r = slice(i * BLK, (i + 1) * BLK)
        s_cur = s_next
        if i + 1 < nblk:
            s_next, k_prev = scores(i + 1, k_prev)
        gate1.append(gate_slice(1, i, gate_w))
        ps, sink_ps = [], []
        for v in range(2 * ATT_KV_HEADS):
            s = s_cur[v] + bias_ref[v]
            if i == 0:
                s = s + no_prev
            m = jnp.maximum(jnp.max(s, axis=0, keepdims=True), sink_rows[v])
            ps.append(jnp.exp2(s - m).astype(BF16))
            sink_ps.append(jnp.exp2(sink_rows[v] - m))
        vt_cur = value_variants(za[r, 640:768])
        heads = [None] * ATT_HEADS
        for j in range(ATT_KV_HEADS):
            vt2 = jnp.concatenate([vt_prev[j], vt_cur[j]], axis=1)
            ones_row = (1 - j) * ATT_DH
            for par in range(2):
                v = 2 * j + par
                pv = _dot(vt2, ps[v])
                rden = 1.0 / (pv[ones_row:ones_row + 1] + sink_ps[v])
                ot = pv[j * ATT_DH:(j + 1) * ATT_DH] * rden
                heads[ATT_GROUP * j + par] = ot[:, 0:BLK]
                heads[ATT_GROUP * j + par + 2] = ot[:, BLK:2 * BLK]
        att_scr[r, :] = jnp.concatenate(heads, axis=0).T.astype(BF16)
        vt_prev = vt_cur
    wk_ref[0] = za[tt - BLK:tt, 512:640]
    wv_ref[0] = za[tt - BLK:tt, 640:768]
    mixed = mixed + jnp.concatenate(gate1, axis=1) * _dot(att_scr[...], wbr_ref[1])

    mixed = mixed.astype(BF16)
    for rw in rows:
        o_ref[0, rw, :] = x_ref[0, rw, :] + _rms(_dot(mixed[rw], wout_ref[...]), gpost_ref[...])


def _mix_prompt_call(layer, x, sinks, gpre, win, convw, wbr, wout, gpost, cast_jobs=()):
    b, t, _ = x.shape
    tt = min(PROMPT_TILE, t)
    nt = t // tt
    cast_in, cast_out, cast_shapes = _cast_specs(cast_jobs, b * nt, lambda bi, ti: bi * nt + ti)
    dmask, qdec, kdec, cdec = (jnp.asarray(a, F32) for a in _retention_consts(BLK))
    dmask = dmask.reshape(RET_HEADS * BLK, BLK)
    bias = jnp.asarray(_prompt_attn_bias(), F32)
    per_b = lambda bi, ti: (bi, 0, 0)
    return pl.pallas_call(
        functools.partial(_mix_prompt_body, layer, len(cast_jobs)),
        grid=(b, nt),
        in_specs=[
            pl.BlockSpec(memory_space=pltpu.SMEM),
            pl.BlockSpec((1, tt, D_MODEL), lambda bi, ti: (bi, ti, 0)),
            _layer_spec(layer, (1, D_MODEL)),
            _const_spec((D_MODEL, N_IN)),
            _layer_spec(layer, (CONV_W, CONV_DIM)),
            _const_spec(bias.shape),
            _const_spec(dmask.shape),
            _const_spec(qdec.shape),
            _const_spec(kdec.shape),
            _const_spec(cdec.shape),
            _const_spec((N_BRANCH, BRANCH_W, D_MODEL)),
            _const_spec((D_MODEL, D_MODEL)),
            _layer_spec(layer, (1, D_MODEL)),
        ] + cast_in,
        out_specs=[
            pl.BlockSpec((1, tt, D_MODEL), lambda bi, ti: (bi, ti, 0)),
            pl.BlockSpec((1, RET_W, RET_DV), per_b),
            pl.BlockSpec((1, WINDOW, KV_W), per_b),
            pl.BlockSpec((1, WINDOW, KV_W), per_b),
            pl.BlockSpec((1, CONV_W - 1, CONV_DIM), per_b),
        ] + cast_out,
        out_shape=[
            jax.ShapeDtypeStruct((b, t, D_MODEL), F32),
            jax.ShapeDtypeStruct((b, RET_W, RET_DV), F32),
            jax.ShapeDtypeStruct((b, WINDOW, KV_W), F32),
            jax.ShapeDtypeStruct((b, WINDOW, KV_W), F32),
            jax.ShapeDtypeStruct((b, CONV_W - 1, CONV_DIM), F32),
        ] + cast_shapes,
        scratch_shapes=[
            pltpu.VMEM((tt, BRANCH_W), BF16),
            pltpu.VMEM((tt, BRANCH_W), BF16),
            pltpu.VMEM((tt + 8, CONV_DIM), F32),
        ],
        compiler_params=pltpu.CompilerParams(
            dimension_semantics=("arbitrary", "arbitrary"), vmem_limit_bytes=VMEM_LIMIT),
        name="mix_prompt",
    )(sinks, x, gpre, win, convw, bias, dmask, qdec, kdec, cdec, wbr, wout, gpost, *[j[0] for j in cast_jobs])


N_SAMPLE_IN = 17
N_SAMPLE_STATE = 4


def _mix_sample_body(n_alias, *refs):
    (x_ref, sret_ref, kt_ref, vt_ref, sconv_ref, gpre_ref, win_ref, convw_ref, bias_ref, sinkrow_ref,
     dmask_ref, qdec_ref, kdec_ref, cdec_ref, wbr_ref, wout_ref, gpost_ref) = refs[:N_SAMPLE_IN]
    o_ref, ret_ref, wk_ref, wv_ref, conv_ref = refs[N_SAMPLE_IN + n_alias:N_SAMPLE_IN + n_alias + 5]
    z_scr, oc_scr, att_scr, cv_scr, br_scr, u_scr = refs[N_SAMPLE_IN + n_alias + 5:]
    rows = x_ref.shape[0]
    nseq = ret_ref.shape[0]
    t = rows // nseq

    x = x_ref[...]
    h = _rms(x, gpre_ref[...]).astype(BF16)
    lane256 = lax.broadcasted_iota(jnp.int32, (1, 256), 1)
    lane128 = lax.broadcasted_iota(jnp.int32, (1, 128), 1)
    zero_bf = jnp.zeros((), BF16)
    head_masks = [(lane256 // RET_DK) == hd for hd in range(RET_HEADS)]
    new_lanes = lane128 >= BLK - t
    zpad = jnp.zeros((BLK - t, KV_W), F32)
    snk = sinkrow_ref[:, 0:1]

    z_scr[...] = _dot(h, win_ref[:, 0:OFF_GATE])

    def new_rows_t(rs, c0):
        return jnp.concatenate([zpad, z_scr[rs, c0:c0 + KV_W]], axis=0).T

    for g0 in range(0, nseq, SEQ_GROUP):
        seqs = list(range(g0, min(g0 + SEQ_GROUP, nseq)))
        rsl = {b: slice(b * t, (b + 1) * t) for b in seqs}

        for b in seqs:
            rs = rsl[b]
            qd = z_scr[rs, 0:256] * qdec_ref[...]
            qm = jnp.concatenate([jnp.where(hm, qd, 0.0) for hm in head_masks], axis=0).astype(BF16)
            s_old = sret_ref[b]
            oc = _dot(qm, s_old.astype(BF16))
            for hd in range(RET_HEADS):
                oc_scr[hd, rs, :] = oc[hd * t:(hd + 1) * t]
            kdt = (z_scr[rs, 256:512] * kdec_ref[...]).T.astype(BF16)
            vb = z_scr[rs, 512:1024].astype(BF16)
            for hd in range(RET_HEADS):
                hr = slice(hd * RET_DK, (hd + 1) * RET_DK)
                ret_ref[b, hr, :] = (s_old[hr] * cdec_ref[hr, :]
                                     + _dot(kdt[hr], vb[:, hd * RET_DV:(hd + 1) * RET_DV]))

        s_raw, nkt = {}, {}
        for b in seqs:
            rs = rsl[b]
            aq = z_scr[rs, 1536:2048] * (ATT_DH ** -0.5)
            pieces = []
            for hd in range(ATT_HEADS):
                piece = aq[:, (hd // 2) * 128:(hd // 2 + 1) * 128]
                if hd % 2 != hd // ATT_GROUP:
                    piece = pltpu.roll(piece, ATT_DH, 1)
                pieces.append(jnp.where((lane128 // ATT_DH) == hd // ATT_GROUP, piece, 0.0))
            qs = jnp.concatenate(pieces, axis=0).astype(BF16)
            nkt[b] = new_rows_t(rs, 2048)
            k2t = jnp.concatenate([kt_ref[b], nkt[b]], axis=1).astype(BF16)
            s_raw[b] = _dot(qs, k2t)

        probs = {}
        for b in seqs:
            s = s_raw[b] + bias_ref[...]
            m = jnp.maximum(jnp.max(s, axis=-1, keepdims=True), snk)
            p = jnp.exp(s - m)
            rden = 1.0 / (jnp.sum(p, axis=-1, keepdims=True) + jnp.exp(snk - m))
            probs[b] = (p * rden).astype(BF16)

        for b in seqs:
            rs = rsl[b]
            nvt = new_rows_t(rs, 2176)
            v2t = jnp.concatenate([vt_ref[b], nvt], axis=1).astype(BF16)
            o = _dot_nt(probs[b], v2t)
            for pr in range(ATT_HEADS // 2):
                lo = o[(2 * pr) * t:(2 * pr + 1) * t]
                hi = o[(2 * pr + 1) * t:(2 * pr + 2) * t]
                if (2 * pr) // ATT_GROUP == 1:
                    lo = pltpu.roll(lo, ATT_DH, 1)
                if (2 * pr + 1) // ATT_GROUP == 0:
                    hi = pltpu.roll(hi, ATT_DH, 1)
                att_scr[rs, pr * 128:(pr + 1) * 128] = jnp.where(lane128 < ATT_DH, lo, hi)
            wk_ref[b] = jnp.where(new_lanes, nkt[b], pltpu.roll(kt_ref[b], BLK - t, 1))
            wv_ref[b] = jnp.where(new_lanes, nvt, pltpu.roll(vt_ref[b], BLK - t, 1))

        for b in seqs:
            rs = rsl[b]
            u = z_scr[rs, 2816:3328] * z_scr[rs, 3328:3840]
            u_scr[b, 6:8, :] = sconv_ref[b]
            u_scr[b, 8:8 + t, :] = u
            y = (u_scr[b, 6:6 + t, :] * convw_ref[0:1, :] + u_scr[b, 7:7 + t, :] * convw_ref[1:2, :]
                 + u * convw_ref[2:3, :])
            cv_scr[rs, :] = z_scr[rs, 2304:2816] * y
            conv_ref[b] = u_scr[b, 6 + t:8 + t, :]

    qb = z_scr[:, 0:256].astype(BF16)
    kb = z_scr[:, 256:512].astype(BF16)
    for hd in range(RET_HEADS):
        inner = _dot_nt(jnp.where(head_masks[hd], qb, zero_bf), kb) * dmask_ref[hd]
        vh = z_scr[:, 512 + hd * RET_DV:512 + (hd + 1) * RET_DV].astype(BF16)
        o = _dot(inner.astype(BF16), vh) + oc_scr[hd]
        o = o * lax.rsqrt(jnp.mean(o * o, axis=-1, keepdims=True) + EPS)
        g = z_scr[:, 1024 + hd * RET_DV:1024 + (hd + 1) * RET_DV]
        br_scr[:, hd * RET_DV:(hd + 1) * RET_DV] = (g * jax.nn.sigmoid(g) * o).astype(BF16)
    mixed = _gated_proj(h, win_ref, 0, br_scr[...], wbr_ref)
    mixed = mixed + _gated_proj(h, win_ref, 1, att_scr[...].astype(BF16), wbr_ref)
    mixed = mixed + _gated_proj(h, win_ref, 2, cv_scr[...].astype(BF16), wbr_ref)
    o_ref[...] = x + _rms(_dot(mixed.astype(BF16), wout_ref[...]), gpost_ref[...])


def _mix_sample_call(layer, x2, t, sinkrow, sret, kt, vt, sconv, gpre, win, convw, wbr, wout, gpost, prev):
    depth, nb = sret.shape[0], sret.shape[1]
    sb = min(SAMPLE_SEQS, nb)
    rows = sb * t
    dm, qd, kd, cdec = _retention_consts(t)
    same_seq = np.kron(np.eye(sb), np.ones((t, t)))
    dmask = jnp.asarray(np.tile(dm, (1, sb, sb)) * same_seq[None], F32)
    qdec = jnp.asarray(qd, F32)
    kdec = jnp.asarray(kd, F32)
    cdec = jnp.asarray(cdec, F32)
    bias = jnp.asarray(_sample_attn_bias(t), F32)
    row = lambda i: (i, 0)
    seq = lambda i: (layer, i, 0, 0)
    state_shapes = [(RET_W, RET_DV), (KV_W, WINDOW), (KV_W, WINDOW), (CONV_W - 1, CONV_DIM)]
    state_specs = [pl.BlockSpec((None, sb) + s, seq) for s in state_shapes]
    n_alias = 0 if prev is None else N_SAMPLE_STATE
    alias_args = [] if prev is None else list(prev)
    return pl.pallas_call(
        functools.partial(_mix_sample_body, n_alias),
        grid=(nb // sb,),
        in_specs=[pl.BlockSpec((rows, D_MODEL), row)] + state_specs + [
            _layer_spec(layer, (1, D_MODEL)),
            _const_spec((D_MODEL, N_IN)),
            _layer_spec(layer, (CONV_W, CONV_DIM)),
            _const_spec(bias.shape),
            _layer_spec(layer, (ATT_HEADS * t, 128)),
            _const_spec(dmask.shape),
            _const_spec(qdec.shape),
            _const_spec(kdec.shape),
            _const_spec(cdec.shape),
            _const_spec((N_BRANCH, BRANCH_W, D_MODEL)),
            _const_spec((D_MODEL, D_MODEL)),
            _layer_spec(layer, (1, D_MODEL)),
        ] + [pl.BlockSpec(memory_space=pl.ANY)] * n_alias,
        out_specs=[pl.BlockSpec((rows, D_MODEL), row)] + state_specs,
        out_shape=[jax.ShapeDtypeStruct((nb * t, D_MODEL), F32)]
        + [jax.ShapeDtypeStruct((depth, nb) + s, F32) for s in state_shapes],
        input_output_aliases={N_SAMPLE_IN + k: 1 + k for k in range(n_alias)},
        scratch_shapes=[
            pltpu.VMEM((rows, OFF_GATE), F32),
            pltpu.VMEM((RET_HEADS, rows, RET_DV), F32),
            pltpu.VMEM((rows, BRANCH_W), F32),
            pltpu.VMEM((rows, BRANCH_W), F32),
            pltpu.VMEM((rows, BRANCH_W), BF16),
            pltpu.VMEM((sb, 8 + t, CONV_DIM), F32),
        ],
        compiler_params=pltpu.CompilerParams(
            dimension_semantics=("arbitrary",), vmem_limit_bytes=VMEM_LIMIT),
        name="mix_sample",
    )(x2, sret, kt, vt, sconv, gpre, win, convw, bias, sinkrow, dmask, qdec, kdec, cdec, wbr, wout, gpost,
      *alias_args)


def kernel(x_prompt, x_sample, p_prompt, p_sample, state_ret, cache_win_k, cache_win_v, state_conv,
           g_mix_pre, w_in, conv_w, attn_sinks, w_branch, w_out, g_mix_post,
           g_ffn_pre, w_ff1, w_ff2, g_ffn_post, g_ple, w_ple_gate, w_ple_proj):
    depth = w_in.shape[0]
    bp, tp, d = x_prompt.shape
    bs, ts, _ = x_sample.shape
    win_w = cache_win_k.shape[2]
    assert d == D_MODEL and w_in.shape[2] == N_IN and tp % BLK == 0 and math.gcd(ts, BLK) == ts
    assert win_w == WINDOW == BLK and ts <= BLK

    gain = lambda a: a.reshape(depth, 1, D_MODEL)
    gpre, gpost = gain(g_mix_pre), gain(g_mix_post)
    gfpre, gfpost, gple = gain(g_ffn_pre), gain(g_ffn_post), gain(g_ple)
    rows2d = lambda a: a.reshape(-1, a.shape[-1])
    mix_w = (w_in[0].astype(BF16), w_branch[0].astype(BF16), w_out[0].astype(BF16))
    wp_all = w_ple_proj.astype(BF16)
    pp = p_prompt.reshape(depth, bp * tp, D_PLE)
    ps = p_sample.reshape(depth, bs * ts, D_PLE)
    sret = state_ret.reshape(depth, bs, RET_W, RET_DV)
    kt = jnp.transpose(cache_win_k, (0, 1, 3, 4, 2)).reshape(depth, bs, KV_W, win_w)
    vt = jnp.transpose(cache_win_v, (0, 1, 3, 4, 2)).reshape(depth, bs, KV_W, win_w)
    sinkrow = jnp.broadcast_to(jnp.repeat(attn_sinks, ts, axis=1)[:, :, None], (depth, ATT_HEADS * ts, 128))

    yp = x_prompt
    ys = x_sample.reshape(bs * ts, d)
    outs_p, sample_states = [], None
    for l in range(depth):
        win, wbr, wout = mix_w
        ffn_jobs = [(rows2d(w_ff1), l, D_MODEL), (rows2d(w_ff2), l, D_FF), (rows2d(w_ple_gate), l, D_MODEL)]
        yp, r, k, v, c, w1, w2, wg = _mix_prompt_call(l, yp, attn_sinks, gpre, win, conv_w, wbr, wout, gpost,
                                                      ffn_jobs)
        outs_p.append((r, k, v, c))
        ffn_w = (gfpre, w1, w2, gfpost, gple, wg, wp_all[l])
        next_jobs = [] if l + 1 == depth else [(rows2d(w_in), l + 1, D_MODEL),
                                               (rows2d(w_branch), l + 1, N_BRANCH * BRANCH_W),
                                               (rows2d(w_out), l + 1, D_MODEL)]
        yp, *next_w = _ffn_call(l, yp.reshape(bp * tp, d), pp, *ffn_w, next_jobs)
        yp = yp.reshape(bp, tp, d)

        ys, *sample_states = _mix_sample_call(l, ys, ts, sinkrow, sret, kt, vt, state_conv, gpre, win, conv_w,
                                              wbr, wout, gpost, sample_states)
        ys, = _ffn_call(l, ys, ps, *ffn_w)
        if next_w:
            mix_w = (next_w[0], next_w[1].reshape(N_BRANCH, BRANCH_W, D_MODEL), next_w[2])

    def stack(i, shape):
        return jnp.stack([o[i] for o in outs_p]).reshape(shape)

    def untranspose(a):
        return jnp.transpose(a.reshape(depth, bs, ATT_KV_HEADS, ATT_DH, win_w), (0, 1, 4, 2, 3))

    rs, kts, vts, cs = sample_states
    return (yp, ys.reshape(bs, ts, d),
            stack(0, (depth, bp, RET_HEADS, RET_DK, RET_DV)),
            stack(1, (depth, bp, WINDOW, ATT_KV_HEADS, ATT_DH)),
            stack(2, (depth, bp, WINDOW, ATT_KV_HEADS, ATT_DH)),
            stack(3, (depth, bp, CONV_W - 1, CONV_DIM)),
            rs.reshape(depth, bs, RET_HEADS, RET_DK, RET_DV), untranspose(kts), untranspose(vts), cs)
```

```python
import functools
import math

import numpy as np
import jax
import jax.numpy as jnp
from jax import lax
from jax.experimental import pallas as pl
from jax.experimental.pallas import tpu as pltpu

F32 = jnp.float32
BF16 = jnp.bfloat16

D_MODEL = 1024
BRANCH_W = 512
N_BRANCH = 3
RET_HEADS = 4
RET_DK = 64
RET_DV = 128
RET_W = RET_HEADS * RET_DK
ATT_HEADS = 8
ATT_KV_HEADS = 2
ATT_GROUP = ATT_HEADS // ATT_KV_HEADS
ATT_DH = 64
KV_W = ATT_KV_HEADS * ATT_DH
WINDOW = 128
CONV_DIM = 512
CONV_W = 3
D_FF = 4096
D_PLE = 256
EPS = 1e-6

OFF_RET = 0
OFF_ATT = 1536
OFF_CONV = 2304
OFF_GATE = 3840
N_IN = 6912

BLK = 128
BF16_ROWS = 16
LOG2E = math.log2(math.e)
NEG = -1e30
VMEM_LIMIT = 56 * 1024 * 1024

PROMPT_TILE = 512
ROW_SPLIT = 4
FFN_TILE = 512
FFN_CHUNK = 1024
SAMPLE_SEQS = 16
SEQ_GROUP = 16


def _dot(a, b):
    return jnp.dot(a, b, preferred_element_type=F32)


def _dot_nt(a, b):
    return lax.dot_general(a, b, (((1,), (1,)), ((), ())), preferred_element_type=F32)


def _rms(x, g):
    return x * lax.rsqrt(jnp.mean(x * x, axis=-1, keepdims=True) + EPS) * g


def _const_spec(shape):
    nd = len(shape)
    return pl.BlockSpec(shape, lambda *_: (0,) * nd, pipeline_mode=pl.Buffered(1))


def _layer_spec(layer, shape):
    nd = len(shape)
    return pl.BlockSpec((None,) + tuple(shape), lambda *_: (layer,) + (0,) * nd, pipeline_mode=pl.Buffered(1))


def _cast_specs(jobs, nsteps, step_of):
    in_specs, out_specs, out_shapes = [], [], []
    for src, layer, nrows in jobs:
        slab = nrows // nsteps
        assert nrows % nsteps == 0 and slab % BF16_ROWS == 0, (nrows, nsteps)
        cols = src.shape[1]
        in_specs.append(pl.BlockSpec((slab, cols), lambda *g, base=layer * nsteps: (base + step_of(*g), 0)))
        out_specs.append(pl.BlockSpec((slab, cols), lambda *g: (step_of(*g), 0)))
        out_shapes.append(jax.ShapeDtypeStruct((nrows, cols), BF16))
    return in_specs, out_specs, out_shapes


def _cast_slabs(src_refs, dst_refs):
    for src, dst in zip(src_refs, dst_refs):
        dst[...] = src[...].astype(BF16)


def _log_gamma():
    return np.log1p(-np.exp2(-5.0 - np.arange(RET_HEADS, dtype=np.float64)))


def _retention_consts(c):
    lg = _log_gamma()
    idx = np.arange(c, dtype=np.float64)
    diff = idx[:, None] - idx[None, :]
    dmask = np.where(diff >= 0, np.exp(lg[:, None, None] * np.maximum(diff, 0.0)), 0.0) * RET_DK ** -0.5
    qdec = np.repeat(np.exp(lg[:, None] * (idx + 1.0)).T, RET_DK, axis=1)
    kdec = np.repeat(np.exp(lg[:, None] * (c - 1.0 - idx)).T, RET_DK, axis=1) * RET_DK ** -0.5
    cdec = np.broadcast_to(np.repeat(np.exp(lg * c), RET_DK)[:, None], (RET_W, RET_DV))
    return dmask, qdec, kdec, cdec


def _slopes():
    return np.exp2(-8.0 * (np.arange(ATT_HEADS, dtype=np.float64) + 1.0) / ATT_HEADS)


def _prompt_attn_bias():
    s = np.arange(2 * BLK)[:, None]
    q = np.arange(BLK)[None, :]
    dist = q + BLK - s
    allowed = (dist >= 0) & (dist < WINDOW)
    per_head = np.where(allowed[None], -_slopes()[:, None, None] * dist[None] * LOG2E, NEG)
    out = [np.concatenate([per_head[ATT_GROUP * j + par], per_head[ATT_GROUP * j + par + 2]], axis=1)
           for j in range(ATT_KV_HEADS) for par in range(2)]
    return np.stack(out)


def _sample_attn_bias(t):
    i = np.arange(t)[:, None]
    col = np.arange(2 * BLK)[None, :]
    new = col >= 2 * BLK - t
    dist = np.where(new, i - (col - (2 * BLK - t)), i + WINDOW - col)
    allowed = (dist >= 0) & (dist < WINDOW) & ((col < WINDOW) | new)
    b = np.where(allowed[None], -_slopes()[:, None, None] * dist[None], NEG)
    return b.reshape(ATT_HEADS * t, 2 * BLK)


N_FFN_IN = 9


def _ffn_body(n_cast, *refs):
    x_ref, p_ref, gpre_ref, w1_ref, w2_ref, gpost_ref, gple_ref, wg_ref, wp_ref = refs[:N_FFN_IN]
    o_ref = refs[N_FFN_IN + n_cast]
    _cast_slabs(refs[N_FFN_IN:N_FFN_IN + n_cast], refs[N_FFN_IN + n_cast + 1:])
    tile = x_ref.shape[0]
    nchunk = D_FF // FFN_CHUNK
    rows = [slice(i * tile // ROW_SPLIT, (i + 1) * tile // ROW_SPLIT) for i in range(ROW_SPLIT)]

    def up(hh, c):
        a = _dot(hh, w1_ref[:, c * FFN_CHUNK:(c + 1) * FFN_CHUNK])
        return jnp.square(jnp.maximum(a, 0.0)).astype(BF16)

    def down(a, c):
        return _dot(a, w2_ref[c * FFN_CHUNK:(c + 1) * FFN_CHUNK, :])

    h_parts = [_rms(x_ref[rw, :], gpre_ref[...]).astype(BF16) for rw in rows]
    a0 = jnp.concatenate([up(hp, 0) for hp in h_parts], axis=0)
    h = jnp.concatenate(h_parts, axis=0)
    acc = down(a0, 0)
    for c in range(1, nchunk - 1):
        acc = acc + down(up(h, c), c)
    a_last = up(h, nchunk - 1)
    f_parts = [acc[rw] + down(a_last[rw], nchunk - 1) for rw in rows]
    pp = _dot(p_ref[...].astype(BF16), wp_ref[...])
    for rw, f in zip(rows, f_parts):
        x2 = x_ref[rw, :] + _rms(f, gpost_ref[...])
        gate = jax.nn.sigmoid(_dot(_rms(x2, gple_ref[...]).astype(BF16), wg_ref[...]))
        o_ref[rw, :] = x2 + gate * pp[rw]


def _ffn_call(layer, x, p, gpre, w1, w2, gpost, gple, wg, wp, cast_jobs=()):
    n = x.shape[0]
    tile = min(FFN_TILE, n)
    row = lambda i: (i, 0)
    cast_in, cast_out, cast_shapes = _cast_specs(cast_jobs, n // tile, lambda i: i)
    return pl.pallas_call(
        functools.partial(_ffn_body, len(cast_jobs)),
        grid=(n // tile,),
        in_specs=[
            pl.BlockSpec((tile, D_MODEL), row),
            pl.BlockSpec((None, tile, D_PLE), lambda i: (layer, i, 0)),
            _layer_spec(layer, (1, D_MODEL)),
            _const_spec((D_MODEL, D_FF)),
            _const_spec((D_FF, D_MODEL)),
            _layer_spec(layer, (1, D_MODEL)),
            _layer_spec(layer, (1, D_MODEL)),
            _const_spec((D_MODEL, D_MODEL)),
            _const_spec((D_PLE, D_MODEL)),
        ] + cast_in,
        out_specs=[pl.BlockSpec((tile, D_MODEL), row)] + cast_out,
        out_shape=[jax.ShapeDtypeStruct((n, D_MODEL), F32)] + cast_shapes,
        compiler_params=pltpu.CompilerParams(
            dimension_semantics=("arbitrary",), vmem_limit_bytes=VMEM_LIMIT),
        name="ffn",
    )(x, p, gpre, w1, w2, gpost, gple, wg, wp, *[j[0] for j in cast_jobs])


N_PROMPT_IN = 13


def _mix_prompt_body(layer, n_cast, *refs):
    (sink_ref, x_ref, gpre_ref, win_ref, convw_ref, bias_ref, dmask_ref, qdec_ref, kdec_ref, cdec_ref,
     wbr_ref, wout_ref, gpost_ref) = refs[:N_PROMPT_IN]
    o_ref, ret_ref, wk_ref, wv_ref, conv_ref = refs[N_PROMPT_IN + n_cast:N_PROMPT_IN + n_cast + 5]
    br_scr, att_scr, u_scr = refs[N_PROMPT_IN + 2 * n_cast + 5:]
    _cast_slabs(refs[N_PROMPT_IN:N_PROMPT_IN + n_cast],
                refs[N_PROMPT_IN + n_cast + 5:N_PROMPT_IN + 2 * n_cast + 5])
    tt = x_ref.shape[1]
    nblk = tt // BLK
    first = pl.program_id(1) == 0

    @pl.when(first)
    def _init():
        ret_ref[...] = jnp.zeros_like(ret_ref)
        wk_ref[...] = jnp.zeros_like(wk_ref)
        wv_ref[...] = jnp.zeros_like(wv_ref)
        u_scr[0:8, :] = jnp.zeros((8, CONV_DIM), F32)

    lane256 = lax.broadcasted_iota(jnp.int32, (1, 256), 1)
    zero_bf = jnp.zeros((), BF16)

    rows = [slice(i * tt // ROW_SPLIT, (i + 1) * tt // ROW_SPLIT) for i in range(ROW_SPLIT)]
    h_parts = [_rms(x_ref[0, rw, :], gpre_ref[...]).astype(BF16) for rw in rows]
    zr = jnp.concatenate([_dot(hp, win_ref[:, OFF_RET:OFF_RET + 1536]) for hp in h_parts], axis=0)
    h = jnp.concatenate(h_parts, axis=0)

    def gate_slice(n, c, width):
        c0 = OFF_GATE + n * D_MODEL + c * width
        return jax.nn.sigmoid(_dot(h, win_ref[:, c0:c0 + width]))

    zc = _dot(h, win_ref[:, OFF_CONV:OFF_CONV + 1536])
    u_scr[8:8 + tt, :] = zc[:, 512:1024] * zc[:, 1024:1536]

    def conv_rows(c):
        r0 = c * BLK
        y = (u_scr[6 + r0:6 + r0 + BLK, :] * convw_ref[0:1, :] + u_scr[7 + r0:7 + r0 + BLK, :] * convw_ref[1:2, :]
             + u_scr[8 + r0:8 + r0 + BLK, :] * convw_ref[2:3, :])
        return (zc[r0:r0 + BLK, 0:512] * y).astype(BF16)

    head_masks = [(lane256 // RET_DK) == hd for hd in range(RET_HEADS)]

    def head_stack(a):
        return jnp.concatenate([jnp.where(hm, a, zero_bf) for hm in head_masks], axis=0)

    def values(c, hd):
        return zr[c * BLK:(c + 1) * BLK, 512 + hd * RET_DV:512 + (hd + 1) * RET_DV].astype(BF16)

    kts, incs = [], []
    for c in range(nblk):
        kt = zr[c * BLK:(c + 1) * BLK, 256:512].T
        kdt = (kt * kdec_ref[...]).astype(BF16)
        kts.append(kt.astype(BF16))
        incs.append(jnp.concatenate(
            [_dot(kdt[hd * RET_DK:(hd + 1) * RET_DK], values(c, hd)) for hd in range(RET_HEADS)], axis=0))

    def scores_and_cross(c, state):
        q_st = head_stack(zr[c * BLK:(c + 1) * BLK, 0:256].astype(BF16))
        return _dot(q_st, jnp.concatenate([kts[c], state.astype(BF16)], axis=1))

    gate_w = D_MODEL // nblk
    gate0, gate2, conv_out = [], [], []
    state = ret_ref[0]
    fused_next = scores_and_cross(0, state)
    for c in range(nblk):
        r = slice(c * BLK, (c + 1) * BLK)
        fused = fused_next
        state = state * cdec_ref[...] + incs[c]
        if c + 1 < nblk:
            fused_next = scores_and_cross(c + 1, state)
        gate0.append(gate_slice(0, c, gate_w))
        gate2.append(gate_slice(2, c, gate_w))
        conv_out.append(conv_rows(c))
        inner = (fused[:, 0:BLK] * dmask_ref[...]).astype(BF16)
        cross = fused[:, BLK:2 * BLK] * qdec_ref[...]
        for hd in range(RET_HEADS):
            hq = slice(hd * BLK, (hd + 1) * BLK)
            o = _dot(inner[hq], values(c, hd)) + cross[hq]
            o = o * lax.rsqrt(jnp.mean(o * o, axis=-1, keepdims=True) + EPS)
            g = zr[r, 1024 + hd * RET_DV:1024 + (hd + 1) * RET_DV]
            br_scr[r, hd * RET_DV:(hd + 1) * RET_DV] = (g * jax.nn.sigmoid(g) * o).astype(BF16)
    ret_ref[0] = state
    tail = u_scr[6 + tt:8 + tt, :]
    conv_ref[0] = tail
    u_scr[6:8, :] = tail
    mixed = (jnp.concatenate(gate0, axis=1) * _dot(br_scr[...], wbr_ref[0])
             + jnp.concatenate(gate2, axis=1) * _dot(jnp.concatenate(conv_out, axis=0), wbr_ref[2]))

    za = _dot(h, win_ref[:, OFF_ATT:OFF_ATT + 768])
    low = lax.broadcasted_iota(jnp.int32, (1, 128), 1) < ATT_DH
    key_row = lax.broadcasted_iota(jnp.int32, (2 * BLK, 1), 0)
    no_prev = jnp.where(key_row < BLK, jnp.where(first, NEG, 0.0), 0.0).astype(F32)
    sink_rows = [jnp.where(lane256 < BLK, sink_ref[layer, ATT_GROUP * j + par] * LOG2E,
                           sink_ref[layer, ATT_GROUP * j + par + 2] * LOG2E)
                 for j in range(ATT_KV_HEADS) for par in range(2)]
    top = lax.broadcasted_iota(jnp.int32, (KV_W, 1), 0) < ATT_DH

    def value_variants(vblk):
        vt = vblk.T
        return [jnp.where(top, vt, 1.0).astype(BF16), jnp.where(top, 1.0, vt).astype(BF16)]

    def key_variants(kblk):
        rolled = pltpu.roll(kblk, ATT_DH, 1)
        return [jnp.where(low, kblk, 0.0).astype(BF16), jnp.where(low, 0.0, rolled).astype(BF16),
                jnp.where(low, rolled, 0.0).astype(BF16), jnp.where(low, 0.0, kblk).astype(BF16)]

    def scores(i, k_prev):
        r = slice(i * BLK, (i + 1) * BLK)
        aq = (za[r, 0:512] * (ATT_DH ** -0.5 * LOG2E)).astype(BF16)
        k_cur = key_variants(za[r, 512:640])
        out = []
        for j in range(ATT_KV_HEADS):
            wq = jnp.concatenate([aq[:, j * 256:j * 256 + 128], aq[:, j * 256 + 128:j * 256 + 256]], axis=0)
            for par in range(2):
                v = 2 * j + par
                out.append(_dot_nt(jnp.concatenate([k_prev[v], k_cur[v]], axis=0), wq))
        return out, k_cur

    gate1 = []
    s_next, k_prev = scores(0, key_variants(wk_ref[0]))
    vt_prev = value_variants(wv_ref[0])
    for i in range(nblk):
        r = slice(i * BLK, (i + 1) * BLK)
        s_cur = s_next
        if i + 1 < nblk:
            s_next, k_prev = scores(i + 1, k_prev)
        gate1.append(gate_slice(1, i, gate_w))
        ps, sink_ps = [], []
        for v in range(2 * ATT_KV_HEADS):
            s = s_cur[v] + bias_ref[v]
            if i == 0:
                s = s + no_prev
            m = jnp.maximum(jnp.max(s, axis=0, keepdims=True), sink_rows[v])
            ps.append(jnp.exp2(s - m).astype(BF16))
            sink_ps.append(jnp.exp2(sink_rows[v] - m))
        vt_cur = value_variants(za[r, 640:768])
        heads = [None] * ATT_HEADS
        for j in range(ATT_KV_HEADS):
            vt2 = jnp.concatenate([vt_prev[j], vt_cur[j]], axis=1)
            ones_row = (1 - j) * ATT_DH
            for par in range(2):
                v = 2 * j + par
                pv = _dot(vt2, ps[v])
                rden = 1.0 / (pv[ones_row:ones_row + 1] + sink_ps[v])
                ot = pv[j * ATT_DH:(j + 1) * ATT_DH] * rden
                heads[ATT_GROUP * j + par] = ot[:, 0:BLK]
                heads[ATT_GROUP * j + par + 2] = ot[:, BLK:2 * BLK]
        att_scr[r, :] = jnp.concatenate(heads, axis=0).T.astype(BF16)
        vt_prev = vt_cur
    wk_ref[0] = za[tt - BLK:tt, 512:640]
    wv_ref[0] = za[tt - BLK:tt, 640:768]
    mixed = mixed + jnp.concatenate(gate1, axis=1) * _dot(att_scr[...], wbr_ref[1])

    mixed = mixed.astype(BF16)
    for rw in rows:
        o_ref[0, rw, :] = x_ref[0, rw, :] + _rms(_dot(mixed[rw], wout_ref[...]), gpost_ref[...])


def _mix_prompt_call(layer, x, sinks, gpre, win, convw, wbr, wout, gpost, cast_jobs=()):
    b, t, _ = x.shape
    tt = min(PROMPT_TILE, t)
    nt = t // tt
    cast_in, cast_out, cast_shapes = _cast_specs(cast_jobs, b * nt, lambda bi, ti: bi * nt + ti)
    dmask, qdec, kdec, cdec = _retention_consts(BLK)
    dmask = jnp.asarray(dmask.reshape(RET_HEADS * BLK, BLK), F32)
    qdec = jnp.asarray(np.broadcast_to(qdec[:, ::RET_DK].T.reshape(RET_HEADS * BLK, 1), (RET_HEADS * BLK, BLK)), F32)
    kdec = jnp.asarray(kdec.T, F32)
    cdec = jnp.asarray(cdec, F32)
    bias = jnp.asarray(_prompt_attn_bias(), F32)
    per_b = lambda bi, ti: (bi, 0, 0)
    return pl.pallas_call(
        functools.partial(_mix_prompt_body, layer, len(cast_jobs)),
        grid=(b, nt),
        in_specs=[
            pl.BlockSpec(memory_space=pltpu.SMEM),
            pl.BlockSpec((1, tt, D_MODEL), lambda bi, ti: (bi, ti, 0)),
            _layer_spec(layer, (1, D_MODEL)),
            _const_spec((D_MODEL, N_IN)),
            _layer_spec(layer, (CONV_W, CONV_DIM)),
            _const_spec(bias.shape),
            _const_spec(dmask.shape),
            _const_spec(qdec.shape),
            _const_spec(kdec.shape),
            _const_spec(cdec.shape),
            _const_spec((N_BRANCH, BRANCH_W, D_MODEL)),
            _const_spec((D_MODEL, D_MODEL)),
            _layer_spec(layer, (1, D_MODEL)),
        ] + cast_in,
        out_specs=[
            pl.BlockSpec((1, tt, D_MODEL), lambda bi, ti: (bi, ti, 0)),
            pl.BlockSpec((1, RET_W, RET_DV), per_b),
            pl.BlockSpec((1, WINDOW, KV_W), per_b),
            pl.BlockSpec((1, WINDOW, KV_W), per_b),
            pl.BlockSpec((1, CONV_W - 1, CONV_DIM), per_b),
        ] + cast_out,
        out_shape=[
            jax.ShapeDtypeStruct((b, t, D_MODEL), F32),
            jax.ShapeDtypeStruct((b, RET_W, RET_DV), F32),
            jax.ShapeDtypeStruct((b, WINDOW, KV_W), F32),
            jax.ShapeDtypeStruct((b, WINDOW, KV_W), F32),
            jax.ShapeDtypeStruct((b, CONV_W - 1, CONV_DIM), F32),
        ] + cast_shapes,
        scratch_shapes=[
            pltpu.VMEM((tt, BRANCH_W), BF16),
            pltpu.VMEM((tt, BRANCH_W), BF16),
            pltpu.VMEM((tt + 8, CONV_DIM), F32),
        ],
        compiler_params=pltpu.CompilerParams(
            dimension_semantics=("arbitrary", "arbitrary"), vmem_limit_bytes=VMEM_LIMIT),
        name="mix_prompt",
    )(sinks, x, gpre, win, convw, bias, dmask, qdec, kdec, cdec, wbr, wout, gpost, *[j[0] for j in cast_jobs])


N_SAMPLE_IN = 17
N_SAMPLE_STATE = 4


def _mix_sample_body(n_alias, *refs):
    (x_ref, sret_ref, kt_ref, vt_ref, sconv_ref, gpre_ref, win_ref, convw_ref, bias_ref, sinkrow_ref,
     dmask_ref, qdec_ref, kdec_ref, cdec_ref, wbr_ref, wout_ref, gpost_ref) = refs[:N_SAMPLE_IN]
    o_ref, ret_ref, wk_ref, wv_ref, conv_ref = refs[N_SAMPLE_IN + n_alias:N_SAMPLE_IN + n_alias + 5]
    z_scr, oc_scr, att_scr, cv_scr, br_scr, u_scr = refs[N_SAMPLE_IN + n_alias + 5:]
    rows = x_ref.shape[0]
    nseq = sret_ref.shape[0]
    slabs = range(ret_ref.shape[0])
    t = rows // nseq

    x = x_ref[...]
    h = _rms(x, gpre_ref[...]).astype(BF16)
    lane256 = lax.broadcasted_iota(jnp.int32, (1, 256), 1)
    lane128 = lax.broadcasted_iota(jnp.int32, (1, 128), 1)
    zero_bf = jnp.zeros((), BF16)
    head_masks = [(lane256 // RET_DK) == hd for hd in range(RET_HEADS)]
    new_lanes = lane128 >= BLK - t
    zpad = jnp.zeros((BLK - t, KV_W), F32)
    snk = sinkrow_ref[:, 0:1]

    z_scr[...] = _dot(h, win_ref[:, 0:OFF_GATE])

    def new_rows_t(rs, c0):
        return jnp.concatenate([zpad, z_scr[rs, c0:c0 + KV_W]], axis=0).T

    ngroups = -(-nseq // SEQ_GROUP)
    gate_w = N_BRANCH * D_MODEL // ngroups
    gate_parts = []
    for g0 in range(0, nseq, SEQ_GROUP):
        seqs = list(range(g0, min(g0 + SEQ_GROUP, nseq)))
        rsl = {b: slice(b * t, (b + 1) * t) for b in seqs}

        for b in seqs:
            rs = rsl[b]
            qd = z_scr[rs, 0:256] * qdec_ref[...]
            qm = jnp.concatenate([jnp.where(hm, qd, 0.0) for hm in head_masks], axis=0).astype(BF16)
            s_old = sret_ref[b]
            oc = _dot(qm, s_old.astype(BF16))
            for hd in range(RET_HEADS):
                oc_scr[hd, rs, :] = oc[hd * t:(hd + 1) * t]
            kdt = (z_scr[rs, 256:512] * kdec_ref[...]).T.astype(BF16)
            vb = z_scr[rs, 512:1024].astype(BF16)
            for hd in range(RET_HEADS):
                hr = slice(hd * RET_DK, (hd + 1) * RET_DK)
                s_new = s_old[hr] * cdec_ref[hr, :] + _dot(kdt[hr], vb[:, hd * RET_DV:(hd + 1) * RET_DV])
                for l in slabs:
                    ret_ref[l, b, hr, :] = s_new

        s_raw, nkt = {}, {}
        for b in seqs:
            rs = rsl[b]
            aq = z_scr[rs, 1536:2048] * (ATT_DH ** -0.5)
            pieces = []
            for hd in range(ATT_HEADS):
                piece = aq[:, (hd // 2) * 128:(hd // 2 + 1) * 128]
                if hd % 2 != hd // ATT_GROUP:
                    piece = pltpu.roll(piece, ATT_DH, 1)
                pieces.append(jnp.where((lane128 // ATT_DH) == hd // ATT_GROUP, piece, 0.0))
            qs = jnp.concatenate(pieces, axis=0).astype(BF16)
            nkt[b] = new_rows_t(rs, 2048)
            k2t = jnp.concatenate([kt_ref[b], nkt[b]], axis=1).astype(BF16)
            s_raw[b] = _dot(qs, k2t)

        c0 = OFF_GATE + (g0 // SEQ_GROUP) * gate_w
        gate_parts.append(jax.nn.sigmoid(_dot(h, win_ref[:, c0:c0 + gate_w])))

        probs = {}
        for b in seqs:
            s = s_raw[b] + bias_ref[...]
            m = jnp.maximum(jnp.max(s, axis=-1, keepdims=True), snk)
            p = jnp.exp(s - m)
            rden = 1.0 / (jnp.sum(p, axis=-1, keepdims=True) + jnp.exp(snk - m))
            probs[b] = (p * rden).astype(BF16)

        for b in seqs:
            rs = rsl[b]
            nvt = new_rows_t(rs, 2176)
            v2t = jnp.concatenate([vt_ref[b], nvt], axis=1).astype(BF16)
            o = _dot_nt(probs[b], v2t)
            for pr in range(ATT_HEADS // 2):
                lo = o[(2 * pr) * t:(2 * pr + 1) * t]
                hi = o[(2 * pr + 1) * t:(2 * pr + 2) * t]
                if (2 * pr) // ATT_GROUP == 1:
                    lo = pltpu.roll(lo, ATT_DH, 1)
                if (2 * pr + 1) // ATT_GROUP == 0:
                    hi = pltpu.roll(hi, ATT_DH, 1)
                att_scr[rs, pr * 128:(pr + 1) * 128] = jnp.where(lane128 < ATT_DH, lo, hi)
            k_new = jnp.where(new_lanes, nkt[b], pltpu.roll(kt_ref[b], BLK - t, 1))
            v_new = jnp.where(new_lanes, nvt, pltpu.roll(vt_ref[b], BLK - t, 1))
            for l in slabs:
                wk_ref[l, b] = k_new
                wv_ref[l, b] = v_new

        for b in seqs:
            rs = rsl[b]
            u = z_scr[rs, 2816:3328] * z_scr[rs, 3328:3840]
            u_scr[b, 6:8, :] = sconv_ref[b]
            u_scr[b, 8:8 + t, :] = u
            y = (u_scr[b, 6:6 + t, :] * convw_ref[0:1, :] + u_scr[b, 7:7 + t, :] * convw_ref[1:2, :]
                 + u * convw_ref[2:3, :])
            cv_scr[rs, :] = z_scr[rs, 2304:2816] * y
            for l in slabs:
                conv_ref[l, b] = u_scr[b, 6 + t:8 + t, :]

    qb = z_scr[:, 0:256].astype(BF16)
    kb = z_scr[:, 256:512].astype(BF16)
    for hd in range(RET_HEADS):
        inner = _dot_nt(jnp.where(head_masks[hd], qb, zero_bf), kb) * dmask_ref[hd]
        vh = z_scr[:, 512 + hd * RET_DV:512 + (hd + 1) * RET_DV].astype(BF16)
        o = _dot(inner.astype(BF16), vh) + oc_scr[hd]
        o = o * lax.rsqrt(jnp.mean(o * o, axis=-1, keepdims=True) + EPS)
        g = z_scr[:, 1024 + hd * RET_DV:1024 + (hd + 1) * RET_DV]
        br_scr[:, hd * RET_DV:(hd + 1) * RET_DV] = (g * jax.nn.sigmoid(g) * o).astype(BF16)
    gates = jnp.concatenate(gate_parts, axis=1)
    mixed = (gates[:, 0:D_MODEL] * _dot(br_scr[...], wbr_ref[0])
             + gates[:, D_MODEL:2 * D_MODEL] * _dot(att_scr[...].astype(BF16), wbr_ref[1])
             + gates[:, 2 * D_MODEL:3 * D_MODEL] * _dot(cv_scr[...].astype(BF16), wbr_ref[2]))
    o_ref[...] = x + _rms(_dot(mixed.astype(BF16), wout_ref[...]), gpost_ref[...])


def _mix_sample_call(layer, x2, t, sinkrow, sret, kt, vt, sconv, gpre, win, convw, wbr, wout, gpost, prev):
    depth, nb = sret.shape[0], sret.shape[1]
    sb = min(SAMPLE_SEQS, nb)
    rows = sb * t
    dm, qd, kd, cdec = _retention_consts(t)
    same_seq = np.kron(np.eye(sb), np.ones((t, t)))
    dmask = jnp.asarray(np.tile(dm, (1, sb, sb)) * same_seq[None], F32)
    qdec = jnp.asarray(qd, F32)
    kdec = jnp.asarray(kd, F32)
    cdec = jnp.asarray(cdec, F32)
    bias = jnp.asarray(_sample_attn_bias(t), F32)
    row = lambda i: (i, 0)
    seq = lambda i: (layer, i, 0, 0)
    state_shapes = [(RET_W, RET_DV), (KV_W, WINDOW), (KV_W, WINDOW), (CONV_W - 1, CONV_DIM)]
    state_specs = [pl.BlockSpec((None, sb) + s, seq) for s in state_shapes]
    if prev is None:
        out_state_specs = [pl.BlockSpec((depth, sb) + s, lambda i: (0, i, 0, 0)) for s in state_shapes]
    else:
        out_state_specs = [pl.BlockSpec((1, sb) + s, seq) for s in state_shapes]
    n_alias = 0 if prev is None else N_SAMPLE_STATE
    alias_args = [] if prev is None else list(prev)
    return pl.pallas_call(
        functools.partial(_mix_sample_body, n_alias),
        grid=(nb // sb,),
        in_specs=[pl.BlockSpec((rows, D_MODEL), row)] + state_specs + [
            _layer_spec(layer, (1, D_MODEL)),
            _const_spec((D_MODEL, N_IN)),
            _layer_spec(layer, (CONV_W, CONV_DIM)),
            _const_spec(bias.shape),
            _layer_spec(layer, (ATT_HEADS * t, 128)),
            _const_spec(dmask.shape),
            _const_spec(qdec.shape),
            _const_spec(kdec.shape),
            _const_spec(cdec.shape),
            _const_spec((N_BRANCH, BRANCH_W, D_MODEL)),
            _const_spec((D_MODEL, D_MODEL)),
            _layer_spec(layer, (1, D_MODEL)),
        ] + [pl.BlockSpec(memory_space=pl.ANY)] * n_alias,
        out_specs=[pl.BlockSpec((rows, D_MODEL), row)] + out_state_specs,
        out_shape=[jax.ShapeDtypeStruct((nb * t, D_MODEL), F32)]
        + [jax.ShapeDtypeStruct((depth, nb) + s, F32) for s in state_shapes],
        input_output_aliases={N_SAMPLE_IN + k: 1 + k for k in range(n_alias)},
        scratch_shapes=[
            pltpu.VMEM((rows, OFF_GATE), F32),
            pltpu.VMEM((RET_HEADS, rows, RET_DV), F32),
            pltpu.VMEM((rows, BRANCH_W), F32),
            pltpu.VMEM((rows, BRANCH_W), F32),
            pltpu.VMEM((rows, BRANCH_W), BF16),
            pltpu.VMEM((sb, 8 + t, CONV_DIM), F32),
        ],
        compiler_params=pltpu.CompilerParams(
            dimension_semantics=("arbitrary",), vmem_limit_bytes=VMEM_LIMIT),
        name="mix_sample",
    )(x2, sret, kt, vt, sconv, gpre, win, convw, bias, sinkrow, dmask, qdec, kdec, cdec, wbr, wout, gpost,
      *alias_args)


def kernel(x_prompt, x_sample, p_prompt, p_sample, state_ret, cache_win_k, cache_win_v, state_conv,
           g_mix_pre, w_in, conv_w, attn_sinks, w_branch, w_out, g_mix_post,
           g_ffn_pre, w_ff1, w_ff2, g_ffn_post, g_ple, w_ple_gate, w_ple_proj):
    depth = w_in.shape[0]
    bp, tp, d = x_prompt.shape
    bs, ts, _ = x_sample.shape
    win_w = cache_win_k.shape[2]
    assert d == D_MODEL and w_in.shape[2] == N_IN and tp % BLK == 0 and math.gcd(ts, BLK) == ts
    assert win_w == WINDOW == BLK and ts <= BLK

    gain = lambda a: a.reshape(depth, 1, D_MODEL)
    gpre, gpost = gain(g_mix_pre), gain(g_mix_post)
    gfpre, gfpost, gple = gain(g_ffn_pre), gain(g_ffn_post), gain(g_ple)
    rows2d = lambda a: a.reshape(-1, a.shape[-1])
    mix_w = (w_in[0].astype(BF16), w_branch[0].astype(BF16), w_out[0].astype(BF16))
    wp_all = w_ple_proj.astype(BF16)
    pp = p_prompt.reshape(depth, bp * tp, D_PLE)
    ps = p_sample.reshape(depth, bs * ts, D_PLE)
    sret = state_ret.reshape(depth, bs, RET_W, RET_DV)
    kt = jnp.transpose(cache_win_k, (0, 1, 3, 4, 2)).reshape(depth, bs, KV_W, win_w)
    vt = jnp.transpose(cache_win_v, (0, 1, 3, 4, 2)).reshape(depth, bs, KV_W, win_w)
    sinkrow = jnp.broadcast_to(jnp.repeat(attn_sinks, ts, axis=1)[:, :, None], (depth, ATT_HEADS * ts, 128))

    yp = x_prompt
    ys = x_sample.reshape(bs * ts, d)
    outs_p, sample_states = [], None
    for l in range(depth):
        win, wbr, wout = mix_w
        ffn_jobs = [(rows2d(w_ff1), l, D_MODEL), (rows2d(w_ff2), l, D_FF), (rows2d(w_ple_gate), l, D_MODEL)]
        yp, r, k, v, c, w1, w2, wg = _mix_prompt_call(l, yp, attn_sinks, gpre, win, conv_w, wbr, wout, gpost,
                                                      ffn_jobs)
        outs_p.append((r, k, v, c))
        ffn_w = (gfpre, w1, w2, gfpost, gple, wg, wp_all[l])
        next_jobs = [] if l + 1 == depth else [(rows2d(w_in), l + 1, D_MODEL),
                                               (rows2d(w_branch), l + 1, N_BRANCH * BRANCH_W),
                                               (rows2d(w_out), l + 1, D_MODEL)]
        yp, *next_w = _ffn_call(l, yp.reshape(bp * tp, d), pp, *ffn_w, next_jobs)
        yp = yp.reshape(bp, tp, d)

        ys, *sample_states = _mix_sample_call(l, ys, ts, sinkrow, sret, kt, vt, state_conv, gpre, win, conv_w,
                                              wbr, wout, gpost, sample_states)
        ys, = _ffn_call(l, ys, ps, *ffn_w)
        if next_w:
            mix_w = (next_w[0], next_w[1].reshape(N_BRANCH, BRANCH_W, D_MODEL), next_w[2])

    def stack(i, shape):
        return jnp.stack([o[i] for o in outs_p]).reshape(shape)

    def untranspose(a):
        return jnp.transpose(a.reshape(depth, bs, ATT_KV_HEADS, ATT_DH, win_w), (0, 1, 4, 2, 3))

    rs, kts, vts, cs = sample_states
    return (yp, ys.reshape(bs, ts, d),
            stack(0, (depth, bp, RET_HEADS, RET_DK, RET_DV)),
            stack(1, (depth, bp, WINDOW, ATT_KV_HEADS, ATT_DH)),
            stack(2, (depth, bp, WINDOW, ATT_KV_HEADS, ATT_DH)),
            stack(3, (depth, bp, CONV_W - 1, CONV_DIM)),
            rs.reshape(depth, bs, RET_HEADS, RET_DK, RET_DV), untranspose(kts), untranspose(vts), cs)
```

```python
import functools
import math

import numpy as np
import jax
import jax.numpy as jnp
from jax import lax
from jax.experimental import pallas as pl
from jax.experimental.pallas import tpu as pltpu

F32 = jnp.float32
BF16 = jnp.bfloat16

D_MODEL = 1024
BRANCH_W = 512
N_BRANCH = 3
RET_HEADS = 4
RET_DK = 64
RET_DV = 128
RET_W = RET_HEADS * RET_DK
ATT_HEADS = 8
ATT_KV_HEADS = 2
ATT_GROUP = ATT_HEADS // ATT_KV_HEADS
ATT_DH = 64
KV_W = ATT_KV_HEADS * ATT_DH
WINDOW = 128
CONV_DIM = 512
CONV_W = 3
D_FF = 4096
D_PLE = 256
EPS = 1e-6

OFF_RET = 0
OFF_ATT = 1536
OFF_CONV = 2304
OFF_GATE = 3840
N_IN = 6912

BLK = 128
BF16_ROWS = 16
LOG2E = math.log2(math.e)
NEG = -1e30
VMEM_LIMIT = 56 * 1024 * 1024

PROMPT_TILE = 512
ROW_SPLIT = 4
FFN_TILE = 1024
FFN_ROW_GROUP = 128
FFN_CHUNK = 1024
SAMPLE_SEQS = 16
SEQ_GROUP = 16


def _dot(a, b):
    return jnp.dot(a, b, preferred_element_type=F32)


def _dot_nt(a, b):
    return lax.dot_general(a, b, (((1,), (1,)), ((), ())), preferred_element_type=F32)


def _rms(x, g):
    return x * lax.rsqrt(jnp.mean(x * x, axis=-1, keepdims=True) + EPS) * g


def _const_spec(shape):
    nd = len(shape)
    return pl.BlockSpec(shape, lambda *_: (0,) * nd, pipeline_mode=pl.Buffered(1))


def _layer_spec(layer, shape):
    nd = len(shape)
    return pl.BlockSpec((None,) + tuple(shape), lambda *_: (layer,) + (0,) * nd, pipeline_mode=pl.Buffered(1))


def _cast_specs(jobs, nsteps, step_of):
    in_specs, out_specs, out_shapes = [], [], []
    for src, layer, nrows in jobs:
        slab = nrows // nsteps
        assert nrows % nsteps == 0 and slab % BF16_ROWS == 0, (nrows, nsteps)
        cols = src.shape[1]
        in_specs.append(pl.BlockSpec((slab, cols), lambda *g, base=layer * nsteps: (base + step_of(*g), 0)))
        out_specs.append(pl.BlockSpec((slab, cols), lambda *g: (step_of(*g), 0)))
        out_shapes.append(jax.ShapeDtypeStruct((nrows, cols), BF16))
    return in_specs, out_specs, out_shapes


def _cast_slabs(src_refs, dst_refs):
    for src, dst in zip(src_refs, dst_refs):
        dst[...] = src[...].astype(BF16)


def _log_gamma():
    return np.log1p(-np.exp2(-5.0 - np.arange(RET_HEADS, dtype=np.float64)))


def _retention_consts(c):
    lg = _log_gamma()
    idx = np.arange(c, dtype=np.float64)
    diff = idx[:, None] - idx[None, :]
    dmask = np.where(diff >= 0, np.exp(lg[:, None, None] * np.maximum(diff, 0.0)), 0.0) * RET_DK ** -0.5
    qdec = np.repeat(np.exp(lg[:, None] * (idx + 1.0)).T, RET_DK, axis=1)
    kdec = np.repeat(np.exp(lg[:, None] * (c - 1.0 - idx)).T, RET_DK, axis=1) * RET_DK ** -0.5
    cdec = np.broadcast_to(np.repeat(np.exp(lg * c), RET_DK)[:, None], (RET_W, RET_DV))
    return dmask, qdec, kdec, cdec


def _slopes():
    return np.exp2(-8.0 * (np.arange(ATT_HEADS, dtype=np.float64) + 1.0) / ATT_HEADS)


def _prompt_attn_bias():
    s = np.arange(2 * BLK)[:, None]
    q = np.arange(BLK)[None, :]
    dist = q + BLK - s
    allowed = (dist >= 0) & (dist < WINDOW)
    per_head = np.where(allowed[None], -_slopes()[:, None, None] * dist[None] * LOG2E, NEG)
    out = [np.concatenate([per_head[ATT_GROUP * j + par], per_head[ATT_GROUP * j + par + 2]], axis=1)
           for j in range(ATT_KV_HEADS) for par in range(2)]
    return np.stack(out)


def _sample_attn_bias(t):
    i = np.arange(t)[:, None]
    col = np.arange(2 * BLK)[None, :]
    new = col >= 2 * BLK - t
    dist = np.where(new, i - (col - (2 * BLK - t)), i + WINDOW - col)
    allowed = (dist >= 0) & (dist < WINDOW) & ((col < WINDOW) | new)
    b = np.where(allowed[None], -_slopes()[:, None, None] * dist[None], NEG)
    return b.reshape(ATT_HEADS * t, 2 * BLK)


N_FFN_IN = 9


def _ffn_body(n_cast, *refs):
    x_ref, p_ref, gpre_ref, w1_ref, w2_ref, gpost_ref, gple_ref, wg_ref, wp_ref = refs[:N_FFN_IN]
    o_ref = refs[N_FFN_IN + n_cast]
    _cast_slabs(refs[N_FFN_IN:N_FFN_IN + n_cast], refs[N_FFN_IN + n_cast + 1:])
    tile = x_ref.shape[0]
    nchunk = D_FF // FFN_CHUNK
    rows = [slice(r0, r0 + FFN_ROW_GROUP) for r0 in range(0, tile, FFN_ROW_GROUP)]

    def up(hh, c):
        a = _dot(hh, w1_ref[:, c * FFN_CHUNK:(c + 1) * FFN_CHUNK])
        return jnp.square(jnp.maximum(a, 0.0)).astype(BF16)

    def down(a, c):
        return _dot(a, w2_ref[c * FFN_CHUNK:(c + 1) * FFN_CHUNK, :])

    h_parts = [_rms(x_ref[rw, :], gpre_ref[...]).astype(BF16) for rw in rows]
    a0 = jnp.concatenate([up(hp, 0) for hp in h_parts], axis=0)
    h = jnp.concatenate(h_parts, axis=0)
    acc = down(a0, 0)
    for c in range(1, nchunk - 1):
        acc = acc + down(up(h, c), c)
    a_last = up(h, nchunk - 1)
    f_parts = [acc[rw] + down(a_last[rw], nchunk - 1) for rw in rows]
    pp = _dot(p_ref[...].astype(BF16), wp_ref[...])
    for rw, f in zip(rows, f_parts):
        x2 = x_ref[rw, :] + _rms(f, gpost_ref[...])
        gate = jax.nn.sigmoid(_dot(_rms(x2, gple_ref[...]).astype(BF16), wg_ref[...]))
        o_ref[rw, :] = x2 + gate * pp[rw]


def _ffn_call(layer, x, p, gpre, w1, w2, gpost, gple, wg, wp, cast_jobs=()):
    n = x.shape[0]
    tile = min(FFN_TILE, n)
    row = lambda i: (i, 0)
    cast_in, cast_out, cast_shapes = _cast_specs(cast_jobs, n // tile, lambda i: i)
    return pl.pallas_call(
        functools.partial(_ffn_body, len(cast_jobs)),
        grid=(n // tile,),
        in_specs=[
            pl.BlockSpec((tile, D_MODEL), row),
            pl.BlockSpec((None, tile, D_PLE), lambda i: (layer, i, 0)),
            _layer_spec(layer, (1, D_MODEL)),
            _const_spec((D_MODEL, D_FF)),
            _const_spec((D_FF, D_MODEL)),
            _layer_spec(layer, (1, D_MODEL)),
            _layer_spec(layer, (1, D_MODEL)),
            _const_spec((D_MODEL, D_MODEL)),
            _const_spec((D_PLE, D_MODEL)),
        ] + cast_in,
        out_specs=[pl.BlockSpec((tile, D_MODEL), row)] + cast_out,
        out_shape=[jax.ShapeDtypeStruct((n, D_MODEL), F32)] + cast_shapes,
        compiler_params=pltpu.CompilerParams(
            dimension_semantics=("arbitrary",), vmem_limit_bytes=VMEM_LIMIT),
        name="ffn",
    )(x, p, gpre, w1, w2, gpost, gple, wg, wp, *[j[0] for j in cast_jobs])


N_PROMPT_IN = 13


def _mix_prompt_body(layer, n_cast, *refs):
    (sink_ref, x_ref, gpre_ref, win_ref, convw_ref, bias_ref, dmask_ref, qdec_ref, kdec_ref, cdec_ref,
     wbr_ref, wout_ref, gpost_ref) = refs[:N_PROMPT_IN]
    o_ref, ret_ref, wk_ref, wv_ref, conv_ref = refs[N_PROMPT_IN + n_cast:N_PROMPT_IN + n_cast + 5]
    br_scr, att_scr, u_scr = refs[N_PROMPT_IN + 2 * n_cast + 5:]
    _cast_slabs(refs[N_PROMPT_IN:N_PROMPT_IN + n_cast],
                refs[N_PROMPT_IN + n_cast + 5:N_PROMPT_IN + 2 * n_cast + 5])
    tt = x_ref.shape[1]
    nblk = tt // BLK
    first = pl.program_id(1) == 0

    @pl.when(first)
    def _init():
        ret_ref[...] = jnp.zeros_like(ret_ref)
        wk_ref[...] = jnp.zeros_like(wk_ref)
        wv_ref[...] = jnp.zeros_like(wv_ref)
        u_scr[0:8, :] = jnp.zeros((8, CONV_DIM), F32)

    lane256 = lax.broadcasted_iota(jnp.int32, (1, 256), 1)
    zero_bf = jnp.zeros((), BF16)

    rows = [slice(i * tt // ROW_SPLIT, (i + 1) * tt // ROW_SPLIT) for i in range(ROW_SPLIT)]
    h_parts = [_rms(x_ref[0, rw, :], gpre_ref[...]).astype(BF16) for rw in rows]
    zr = jnp.concatenate([_dot(hp, win_ref[:, OFF_RET:OFF_RET + 1536]) for hp in h_parts], axis=0)
    h = jnp.concatenate(h_parts, axis=0)

    def gate_slice(n, c, width):
        c0 = OFF_GATE + n * D_MODEL + c * width
        return jax.nn.sigmoid(_dot(h, win_ref[:, c0:c0 + width]))

    zc = _dot(h, win_ref[:, OFF_CONV:OFF_CONV + 1536])
    u_scr[8:8 + tt, :] = zc[:, 512:1024] * zc[:, 1024:1536]

    def conv_rows(c):
        r0 = c * BLK
        y = (u_scr[6 + r0:6 + r0 + BLK, :] * convw_ref[0:1, :] + u_scr[7 + r0:7 + r0 + BLK, :] * convw_ref[1:2, :]
             + u_scr[8 + r0:8 + r0 + BLK, :] * convw_ref[2:3, :])
        return (zc[r0:r0 + BLK, 0:512] * y).astype(BF16)

    head_masks = [(lane256 // RET_DK) == hd for hd in range(RET_HEADS)]

    def head_stack(a):
        return jnp.concatenate([jnp.where(hm, a, zero_bf) for hm in head_masks], axis=0)

    def values(c, hd):
        return zr[c * BLK:(c + 1) * BLK, 512 + hd * RET_DV:512 + (hd + 1) * RET_DV].astype(BF16)

    kts, incs = [], []
    for c in range(nblk):
        kt = zr[c * BLK:(c + 1) * BLK, 256:512].T
        kdt = (kt * kdec_ref[...]).astype(BF16)
        kts.append(kt.astype(BF16))
        incs.append(jnp.concatenate(
            [_dot(kdt[hd * RET_DK:(hd + 1) * RET_DK], values(c, hd)) for hd in range(RET_HEADS)], axis=0))

    def scores_and_cross(c, state):
        q_st = head_stack(zr[c * BLK:(c + 1) * BLK, 0:256].astype(BF16))
        return _dot(q_st, jnp.concatenate([kts[c], state.astype(BF16)], axis=1))

    gate_w = D_MODEL // nblk
    gate0, gate2, conv_out = [], [], []
    state = ret_ref[0]
    fused_next = scores_and_cross(0, state)
    for c in range(nblk):
        r = slice(c * BLK, (c + 1) * BLK)
        fused = fused_next
        state = state * cdec_ref[...] + incs[c]
        if c + 1 < nblk:
            fused_next = scores_and_cross(c + 1, state)
        gate0.append(gate_slice(0, c, gate_w))
        gate2.append(gate_slice(2, c, gate_w))
        conv_out.append(conv_rows(c))
        inner = (fused[:, 0:BLK] * dmask_ref[...]).astype(BF16)
        cross = fused[:, BLK:2 * BLK] * qdec_ref[...]
        for hd in range(RET_HEADS):
            hq = slice(hd * BLK, (hd + 1) * BLK)
            o = _dot(inner[hq], values(c, hd)) + cross[hq]
            o = o * lax.rsqrt(jnp.mean(o * o, axis=-1, keepdims=True) + EPS)
            g = zr[r, 1024 + hd * RET_DV:1024 + (hd + 1) * RET_DV]
            br_scr[r, hd * RET_DV:(hd + 1) * RET_DV] = (g * jax.nn.sigmoid(g) * o).astype(BF16)
    ret_ref[0] = state
    tail = u_scr[6 + tt:8 + tt, :]
    conv_ref[0] = tail
    u_scr[6:8, :] = tail
    mixed = (jnp.concatenate(gate0, axis=1) * _dot(br_scr[...], wbr_ref[0])
             + jnp.concatenate(gate2, axis=1) * _dot(jnp.concatenate(conv_out, axis=0), wbr_ref[2]))

    za = _dot(h, win_ref[:, OFF_ATT:OFF_ATT + 768])
    low = lax.broadcasted_iota(jnp.int32, (1, 128), 1) < ATT_DH
    key_row = lax.broadcasted_iota(jnp.int32, (2 * BLK, 1), 0)
    no_prev = jnp.where(key_row < BLK, jnp.where(first, NEG, 0.0), 0.0).astype(F32)
    sink_rows = [jnp.where(lane256 < BLK, sink_ref[layer, ATT_GROUP * j + par] * LOG2E,
                           sink_ref[layer, ATT_GROUP * j + par + 2] * LOG2E)
                 for j in range(ATT_KV_HEADS) for par in range(2)]
    top = lax.broadcasted_iota(jnp.int32, (KV_W, 1), 0) < ATT_DH

    def value_variants(vblk):
        vt = vblk.T
        return [jnp.where(top, vt, 1.0).astype(BF16), jnp.where(top, 1.0, vt).astype(BF16)]

    def key_variants(kblk):
        rolled = pltpu.roll(kblk, ATT_DH, 1)
        return [jnp.where(low, kblk, 0.0).astype(BF16), jnp.where(low, 0.0, rolled).astype(BF16),
                jnp.where(low, rolled, 0.0).astype(BF16), jnp.where(low, 0.0, kblk).astype(BF16)]

    def scores(i, k_prev):
        r = slice(i * BLK, (i + 1) * BLK)
        aq = (za[r, 0:512] * (ATT_DH ** -0.5 * LOG2E)).astype(BF16)
        k_cur = key_variants(za[r, 512:640])
        out = []
        for j in range(ATT_KV_HEADS):
            wq = jnp.concatenate([aq[:, j * 256:j * 256 + 128], aq[:, j * 256 + 128:j * 256 + 256]], axis=0)
            for par in range(2):
                v = 2 * j + par
                out.append(_dot_nt(jnp.concatenate([k_prev[v], k_cur[v]], axis=0), wq))
        return out, k_cur

    gate1 = []
    s_next, k_prev = scores(0, key_variants(wk_ref[0]))
    vt_prev = value_variants(wv_ref[0])
    for i in range(nblk):
        r = slice(i * BLK, (i + 1) * BLK)
        s_cur = s_next
        if i + 1 < nblk:
            s_next, k_prev = scores(i + 1, k_prev)
        gate1.append(gate_slice(1, i, gate_w))
        ps, sink_ps = [], []
        for v in range(2 * ATT_KV_HEADS):
            s = s_cur[v] + bias_ref[v]
            if i == 0:
                s = s + no_prev
            m = jnp.maximum(jnp.max(s, axis=0, keepdims=True), sink_rows[v])
            ps.append(jnp.exp2(s - m).astype(BF16))
            sink_ps.append(jnp.exp2(sink_rows[v] - m))
        vt_cur = value_variants(za[r, 640:768])
        heads = [None] * ATT_HEADS
        for j in range(ATT_KV_HEADS):
            vt2 = jnp.concatenate([vt_prev[j], vt_cur[j]], axis=1)
            ones_row = (1 - j) * ATT_DH
            for par in range(2):
                v = 2 * j + par
                pv = _dot(vt2, ps[v])
                rden = 1.0 / (pv[ones_row:ones_row + 1] + sink_ps[v])
                ot = pv[j * ATT_DH:(j + 1) * ATT_DH] * rden
                heads[ATT_GROUP * j + par] = ot[:, 0:BLK]
                heads[ATT_GROUP * j + par + 2] = ot[:, BLK:2 * BLK]
        att_scr[r, :] = jnp.concatenate(heads, axis=0).T.astype(BF16)
        vt_prev = vt_cur
    wk_ref[0] = za[tt - BLK:tt, 512:640]
    wv_ref[0] = za[tt - BLK:tt, 640:768]
    mixed = mixed + jnp.concatenate(gate1, axis=1) * _dot(att_scr[...], wbr_ref[1])

    mixed = mixed.astype(BF16)
    for rw in rows:
        o_ref[0, rw, :] = x_ref[0, rw, :] + _rms(_dot(mixed[rw], wout_ref[...]), gpost_ref[...])


def _mix_prompt_call(layer, x, sinks, gpre, win, convw, wbr, wout, gpost, cast_jobs=()):
    b, t, _ = x.shape
    tt = min(PROMPT_TILE, t)
    nt = t // tt
    cast_in, cast_out, cast_shapes = _cast_specs(cast_jobs, b * nt, lambda bi, ti: bi * nt + ti)
    dmask, qdec, kdec, cdec = _retention_consts(BLK)
    dmask = jnp.asarray(dmask.reshape(RET_HEADS * BLK, BLK), F32)
    qdec = jnp.asarray(np.broadcast_to(qdec[:, ::RET_DK].T.reshape(RET_HEADS * BLK, 1), (RET_HEADS * BLK, BLK)), F32)
    kdec = jnp.asarray(kdec.T, F32)
    cdec = jnp.asarray(cdec, F32)
    bias = jnp.asarray(_prompt_attn_bias(), F32)
    per_b = lambda bi, ti: (bi, 0, 0)
    return pl.pallas_call(
        functools.partial(_mix_prompt_body, layer, len(cast_jobs)),
        grid=(b, nt),
        in_specs=[
            pl.BlockSpec(memory_space=pltpu.SMEM),
            pl.BlockSpec((1, tt, D_MODEL), lambda bi, ti: (bi, ti, 0)),
            _layer_spec(layer, (1, D_MODEL)),
            _const_spec((D_MODEL, N_IN)),
            _layer_spec(layer, (CONV_W, CONV_DIM)),
            _const_spec(bias.shape),
            _const_spec(dmask.shape),
            _const_spec(qdec.shape),
            _const_spec(kdec.shape),
            _const_spec(cdec.shape),
            _const_spec((N_BRANCH, BRANCH_W, D_MODEL)),
            _const_spec((D_MODEL, D_MODEL)),
            _layer_spec(layer, (1, D_MODEL)),
        ] + cast_in,
        out_specs=[
            pl.BlockSpec((1, tt, D_MODEL), lambda bi, ti: (bi, ti, 0)),
            pl.BlockSpec((1, RET_W, RET_DV), per_b),
            pl.BlockSpec((1, WINDOW, KV_W), per_b),
            pl.BlockSpec((1, WINDOW, KV_W), per_b),
            pl.BlockSpec((1, CONV_W - 1, CONV_DIM), per_b),
        ] + cast_out,
        out_shape=[
            jax.ShapeDtypeStruct((b, t, D_MODEL), F32),
            jax.ShapeDtypeStruct((b, RET_W, RET_DV), F32),
            jax.ShapeDtypeStruct((b, WINDOW, KV_W), F32),
            jax.ShapeDtypeStruct((b, WINDOW, KV_W), F32),
            jax.ShapeDtypeStruct((b, CONV_W - 1, CONV_DIM), F32),
        ] + cast_shapes,
        scratch_shapes=[
            pltpu.VMEM((tt, BRANCH_W), BF16),
            pltpu.VMEM((tt, BRANCH_W), BF16),
            pltpu.VMEM((tt + 8, CONV_DIM), F32),
        ],
        compiler_params=pltpu.CompilerParams(
            dimension_semantics=("arbitrary", "arbitrary"), vmem_limit_bytes=VMEM_LIMIT),
        name="mix_prompt",
    )(sinks, x, gpre, win, convw, bias, dmask, qdec, kdec, cdec, wbr, wout, gpost, *[j[0] for j in cast_jobs])


N_SAMPLE_IN = 17
N_SAMPLE_STATE = 4


def _mix_sample_body(n_alias, *refs):
    (x_ref, sret_ref, kt_ref, vt_ref, sconv_ref, gpre_ref, win_ref, convw_ref, bias_ref, sinkrow_ref,
     dmask_ref, qdec_ref, kdec_ref, cdec_ref, wbr_ref, wout_ref, gpost_ref) = refs[:N_SAMPLE_IN]
    o_ref, ret_ref, wk_ref, wv_ref, conv_ref = refs[N_SAMPLE_IN + n_alias:N_SAMPLE_IN + n_alias + 5]
    z_scr, oc_scr, att_scr, cv_scr, br_scr, u_scr = refs[N_SAMPLE_IN + n_alias + 5:]
    rows = x_ref.shape[0]
    nseq = sret_ref.shape[0]
    slabs = range(ret_ref.shape[0])
    t = rows // nseq

    x = x_ref[...]
    h = _rms(x, gpre_ref[...]).astype(BF16)
    lane256 = lax.broadcasted_iota(jnp.int32, (1, 256), 1)
    lane128 = lax.broadcasted_iota(jnp.int32, (1, 128), 1)
    zero_bf = jnp.zeros((), BF16)
    head_masks = [(lane256 // RET_DK) == hd for hd in range(RET_HEADS)]
    new_lanes = lane128 >= BLK - t
    zpad = jnp.zeros((BLK - t, KV_W), F32)
    snk = sinkrow_ref[:, 0:1]

    z_scr[...] = _dot(h, win_ref[:, 0:OFF_GATE])

    def new_rows_t(rs, c0):
        return jnp.concatenate([zpad, z_scr[rs, c0:c0 + KV_W]], axis=0).T

    ngroups = -(-nseq // SEQ_GROUP)
    gate_w = N_BRANCH * D_MODEL // ngroups
    gate_parts = []
    for g0 in range(0, nseq, SEQ_GROUP):
        seqs = list(range(g0, min(g0 + SEQ_GROUP, nseq)))
        rsl = {b: slice(b * t, (b + 1) * t) for b in seqs}

        for b in seqs:
            rs = rsl[b]
            qd = z_scr[rs, 0:256] * qdec_ref[...]
            qm = jnp.concatenate([jnp.where(hm, qd, 0.0) for hm in head_masks], axis=0).astype(BF16)
            s_old = sret_ref[b]
            oc = _dot(qm, s_old.astype(BF16))
            for hd in range(RET_HEADS):
                oc_scr[hd, rs, :] = oc[hd * t:(hd + 1) * t]
            kdt = (z_scr[rs, 256:512] * kdec_ref[...]).T.astype(BF16)
            vb = z_scr[rs, 512:1024].astype(BF16)
            for hd in range(RET_HEADS):
                hr = slice(hd * RET_DK, (hd + 1) * RET_DK)
                s_new = s_old[hr] * cdec_ref[hr, :] + _dot(kdt[hr], vb[:, hd * RET_DV:(hd + 1) * RET_DV])
                for l in slabs:
                    ret_ref[l, b, hr, :] = s_new

        s_raw, nkt = {}, {}
        for b in seqs:
            rs = rsl[b]
            aq = z_scr[rs, 1536:2048] * (ATT_DH ** -0.5)
            pieces = []
            for hd in range(ATT_HEADS):
                piece = aq[:, (hd // 2) * 128:(hd // 2 + 1) * 128]
                if hd % 2 != hd // ATT_GROUP:
                    piece = pltpu.roll(piece, ATT_DH, 1)
                pieces.append(jnp.where((lane128 // ATT_DH) == hd // ATT_GROUP, piece, 0.0))
            qs = jnp.concatenate(pieces, axis=0).astype(BF16)
            nkt[b] = new_rows_t(rs, 2048)
            k2t = jnp.concatenate([kt_ref[b], nkt[b]], axis=1).astype(BF16)
            s_raw[b] = _dot(qs, k2t)

        c0 = OFF_GATE + (g0 // SEQ_GROUP) * gate_w
        gate_parts.append(jax.nn.sigmoid(_dot(h, win_ref[:, c0:c0 + gate_w])))

        probs = {}
        for b in seqs:
            s = s_raw[b] + bias_ref[...]
            m = jnp.maximum(jnp.max(s, axis=-1, keepdims=True), snk)
            p = jnp.exp(s - m)
            rden = 1.0 / (jnp.sum(p, axis=-1, keepdims=True) + jnp.exp(snk - m))
            probs[b] = (p * rden).astype(BF16)

        for b in seqs:
            rs = rsl[b]
            nvt = new_rows_t(rs, 2176)
            v2t = jnp.concatenate([vt_ref[b], nvt], axis=1).astype(BF16)
            o = _dot_nt(probs[b], v2t)
            for pr in range(ATT_HEADS // 2):
                lo = o[(2 * pr) * t:(2 * pr + 1) * t]
                hi = o[(2 * pr + 1) * t:(2 * pr + 2) * t]
                if (2 * pr) // ATT_GROUP == 1:
                    lo = pltpu.roll(lo, ATT_DH, 1)
                if (2 * pr + 1) // ATT_GROUP == 0:
                    hi = pltpu.roll(hi, ATT_DH, 1)
                att_scr[rs, pr * 128:(pr + 1) * 128] = jnp.where(lane128 < ATT_DH, lo, hi)
            k_new = jnp.where(new_lanes, nkt[b], pltpu.roll(kt_ref[b], BLK - t, 1))
            v_new = jnp.where(new_lanes, nvt, pltpu.roll(vt_ref[b], BLK - t, 1))
            for l in slabs:
                wk_ref[l, b] = k_new
                wv_ref[l, b] = v_new

        for b in seqs:
            rs = rsl[b]
            u = z_scr[rs, 2816:3328] * z_scr[rs, 3328:3840]
            u_scr[b, 6:8, :] = sconv_ref[b]
            u_scr[b, 8:8 + t, :] = u
            y = (u_scr[b, 6:6 + t, :] * convw_ref[0:1, :] + u_scr[b, 7:7 + t, :] * convw_ref[1:2, :]
                 + u * convw_ref[2:3, :])
            cv_scr[rs, :] = z_scr[rs, 2304:2816] * y
            for l in slabs:
                conv_ref[l, b] = u_scr[b, 6 + t:8 + t, :]

    qb = z_scr[:, 0:256].astype(BF16)
    kb = z_scr[:, 256:512].astype(BF16)
    for hd in range(RET_HEADS):
        inner = _dot_nt(jnp.where(head_masks[hd], qb, zero_bf), kb) * dmask_ref[hd]
        vh = z_scr[:, 512 + hd * RET_DV:512 + (hd + 1) * RET_DV].astype(BF16)
        o = _dot(inner.astype(BF16), vh) + oc_scr[hd]
        o = o * lax.rsqrt(jnp.mean(o * o, axis=-1, keepdims=True) + EPS)
        g = z_scr[:, 1024 + hd * RET_DV:1024 + (hd + 1) * RET_DV]
        br_scr[:, hd * RET_DV:(hd + 1) * RET_DV] = (g * jax.nn.sigmoid(g) * o).astype(BF16)
    gates = jnp.concatenate(gate_parts, axis=1)
    mixed = (gates[:, 0:D_MODEL] * _dot(br_scr[...], wbr_ref[0])
             + gates[:, D_MODEL:2 * D_MODEL] * _dot(att_scr[...].astype(BF16), wbr_ref[1])
             + gates[:, 2 * D_MODEL:3 * D_MODEL] * _dot(cv_scr[...].astype(BF16), wbr_ref[2]))
    o_ref[...] = x + _rms(_dot(mixed.astype(BF16), wout_ref[...]), gpost_ref[...])


def _mix_sample_call(layer, x2, t, sinkrow, sret, kt, vt, sconv, gpre, win, convw, wbr, wout, gpost, prev):
    depth, nb = sret.shape[0], sret.shape[1]
    sb = min(SAMPLE_SEQS, nb)
    rows = sb * t
    dm, qd, kd, cdec = _retention_consts(t)
    same_seq = np.kron(np.eye(sb), np.ones((t, t)))
    dmask = jnp.asarray(np.tile(dm, (1, sb, sb)) * same_seq[None], F32)
    qdec = jnp.asarray(qd, F32)
    kdec = jnp.asarray(kd, F32)
    cdec = jnp.asarray(cdec, F32)
    bias = jnp.asarray(_sample_attn_bias(t), F32)
    row = lambda i: (i, 0)
    seq = lambda i: (layer, i, 0, 0)
    state_shapes = [(RET_W, RET_DV), (KV_W, WINDOW), (KV_W, WINDOW), (CONV_W - 1, CONV_DIM)]
    state_specs = [pl.BlockSpec((None, sb) + s, seq) for s in state_shapes]
    if prev is None:
        out_state_specs = [pl.BlockSpec((depth, sb) + s, lambda i: (0, i, 0, 0)) for s in state_shapes]
    else:
        out_state_specs = [pl.BlockSpec((1, sb) + s, seq) for s in state_shapes]
    n_alias = 0 if prev is None else N_SAMPLE_STATE
    alias_args = [] if prev is None else list(prev)
    return pl.pallas_call(
        functools.partial(_mix_sample_body, n_alias),
        grid=(nb // sb,),
        in_specs=[pl.BlockSpec((rows, D_MODEL), row)] + state_specs + [
            _layer_spec(layer, (1, D_MODEL)),
            _const_spec((D_MODEL, N_IN)),
            _layer_spec(layer, (CONV_W, CONV_DIM)),
            _const_spec(bias.shape),
            _layer_spec(layer, (ATT_HEADS * t, 128)),
            _const_spec(dmask.shape),
            _const_spec(qdec.shape),
            _const_spec(kdec.shape),
            _const_spec(cdec.shape),
            _const_spec((N_BRANCH, BRANCH_W, D_MODEL)),
            _const_spec((D_MODEL, D_MODEL)),
            _layer_spec(layer, (1, D_MODEL)),
        ] + [pl.BlockSpec(memory_space=pl.ANY)] * n_alias,
        out_specs=[pl.BlockSpec((rows, D_MODEL), row)] + out_state_specs,
        out_shape=[jax.ShapeDtypeStruct((nb * t, D_MODEL), F32)]
        + [jax.ShapeDtypeStruct((depth, nb) + s, F32) for s in state_shapes],
        input_output_aliases={N_SAMPLE_IN + k: 1 + k for k in range(n_alias)},
        scratch_shapes=[
            pltpu.VMEM((rows, OFF_GATE), F32),
            pltpu.VMEM((RET_HEADS, rows, RET_DV), F32),
            pltpu.VMEM((rows, BRANCH_W), F32),
            pltpu.VMEM((rows, BRANCH_W), F32),
            pltpu.VMEM((rows, BRANCH_W), BF16),
            pltpu.VMEM((sb, 8 + t, CONV_DIM), F32),
        ],
        compiler_params=pltpu.CompilerParams(
            dimension_semantics=("arbitrary",), vmem_limit_bytes=VMEM_LIMIT),
        name="mix_sample",
    )(x2, sret, kt, vt, sconv, gpre, win, convw, bias, sinkrow, dmask, qdec, kdec, cdec, wbr, wout, gpost,
      *alias_args)


def kernel(x_prompt, x_sample, p_prompt, p_sample, state_ret, cache_win_k, cache_win_v, state_conv,
           g_mix_pre, w_in, conv_w, attn_sinks, w_branch, w_out, g_mix_post,
           g_ffn_pre, w_ff1, w_ff2, g_ffn_post, g_ple, w_ple_gate, w_ple_proj):
    depth = w_in.shape[0]
    bp, tp, d = x_prompt.shape
    bs, ts, _ = x_sample.shape
    win_w = cache_win_k.shape[2]
    assert d == D_MODEL and w_in.shape[2] == N_IN and tp % BLK == 0 and math.gcd(ts, BLK) == ts
    assert win_w == WINDOW == BLK and ts <= BLK

    gain = lambda a: a.reshape(depth, 1, D_MODEL)
    gpre, gpost = gain(g_mix_pre), gain(g_mix_post)
    gfpre, gfpost, gple = gain(g_ffn_pre), gain(g_ffn_post), gain(g_ple)
    rows2d = lambda a: a.reshape(-1, a.shape[-1])
    mix_w = (w_in[0].astype(BF16), w_branch[0].astype(BF16), w_out[0].astype(BF16))
    wp_all = w_ple_proj.astype(BF16)
    pp = p_prompt.reshape(depth, bp * tp, D_PLE)
    ps = p_sample.reshape(depth, bs * ts, D_PLE)
    sret = state_ret.reshape(depth, bs, RET_W, RET_DV)
    kt = jnp.transpose(cache_win_k, (0, 1, 3, 4, 2)).reshape(depth, bs, KV_W, win_w)
    vt = jnp.transpose(cache_win_v, (0, 1, 3, 4, 2)).reshape(depth, bs, KV_W, win_w)
    sinkrow = jnp.broadcast_to(jnp.repeat(attn_sinks, ts, axis=1)[:, :, None], (depth, ATT_HEADS * ts, 128))

    yp = x_prompt
    ys = x_sample.reshape(bs * ts, d)
    outs_p, sample_states = [], None
    for l in range(depth):
        win, wbr, wout = mix_w
        ffn_jobs = [(rows2d(w_ff1), l, D_MODEL), (rows2d(w_ff2), l, D_FF), (rows2d(w_ple_gate), l, D_MODEL)]
        yp, r, k, v, c, w1, w2, wg = _mix_prompt_call(l, yp, attn_sinks, gpre, win, conv_w, wbr, wout, gpost,
                                                      ffn_jobs)
        outs_p.append((r, k, v, c))
        ffn_w = (gfpre, w1, w2, gfpost, gple, wg, wp_all[l])
        next_jobs = [] if l + 1 == depth else [(rows2d(w_in), l + 1, D_MODEL),
                                               (rows2d(w_branch), l + 1, N_BRANCH * BRANCH_W),
                                               (rows2d(w_out), l + 1, D_MODEL)]
        yp, *next_w = _ffn_call(l, yp.reshape(bp * tp, d), pp, *ffn_w, next_jobs)
        yp = yp.reshape(bp, tp, d)

        ys, *sample_states = _mix_sample_call(l, ys, ts, sinkrow, sret, kt, vt, state_conv, gpre, win, conv_w,
                                              wbr, wout, gpost, sample_states)
        ys, = _ffn_call(l, ys, ps, *ffn_w)
        if next_w:
            mix_w = (next_w[0], next_w[1].reshape(N_BRANCH, BRANCH_W, D_MODEL), next_w[2])

    def stack(i, shape):
        return jnp.stack([o[i] for o in outs_p]).reshape(shape)

    def untranspose(a):
        return jnp.transpose(a.reshape(depth, bs, ATT_KV_HEADS, ATT_DH, win_w), (0, 1, 4, 2, 3))

    rs, kts, vts, cs = sample_states
    return (yp, ys.reshape(bs, ts, d),
            stack(0, (depth, bp, RET_HEADS, RET_DK, RET_DV)),
            stack(1, (depth, bp, WINDOW, ATT_KV_HEADS, ATT_DH)),
            stack(2, (depth, bp, WINDOW, ATT_KV_HEADS, ATT_DH)),
            stack(3, (depth, bp, CONV_W - 1, CONV_DIM)),
            rs.reshape(depth, bs, RET_HEADS, RET_DK, RET_DV), untranspose(kts), untranspose(vts), cs)
```

```python
import functools
import math

import numpy as np
import jax
import jax.numpy as jnp
from jax import lax
from jax.experimental import pallas as pl
from jax.experimental.pallas import tpu as pltpu

F32 = jnp.float32
BF16 = jnp.bfloat16

D_MODEL = 1024
BRANCH_W = 512
N_BRANCH = 3
RET_HEADS = 4
RET_DK = 64
RET_DV = 128
RET_W = RET_HEADS * RET_DK
ATT_HEADS = 8
ATT_KV_HEADS = 2
ATT_GROUP = ATT_HEADS // ATT_KV_HEADS
ATT_DH = 64
KV_W = ATT_KV_HEADS * ATT_DH
WINDOW = 128
CONV_DIM = 512
CONV_W = 3
D_FF = 4096
D_PLE = 256
EPS = 1e-6

OFF_RET = 0
OFF_ATT = 1536
OFF_CONV = 2304
OFF_GATE = 3840
N_IN = 6912

BLK = 128
BF16_ROWS = 16
LOG2E = math.log2(math.e)
NEG = -1e30
VMEM_LIMIT = 56 * 1024 * 1024

PROMPT_TILE = 512
ROW_SPLIT = 4
FFN_TILE = 1024
FFN_ROW_GROUP = 128
FFN_CHUNK = 1024
SAMPLE_SEQS = 16
SEQ_GROUP = 16


def _dot(a, b):
    return jnp.dot(a, b, preferred_element_type=F32)


def _dot_nt(a, b):
    return lax.dot_general(a, b, (((1,), (1,)), ((), ())), preferred_element_type=F32)


def _rms(x, g):
    return x * lax.rsqrt(jnp.mean(x * x, axis=-1, keepdims=True) + EPS) * g


def _gain_row(g_ref, layer):
    return g_ref[layer:layer + 1, :]


def _const_spec(shape):
    nd = len(shape)
    return pl.BlockSpec(shape, lambda *_: (0,) * nd, pipeline_mode=pl.Buffered(1))


def _layer_spec(layer, shape):
    nd = len(shape)
    return pl.BlockSpec((None,) + tuple(shape), lambda *_: (layer,) + (0,) * nd, pipeline_mode=pl.Buffered(1))


def _cast_specs(jobs, nsteps, step_of):
    in_specs, out_specs, out_shapes = [], [], []
    for src, layer, nrows in jobs:
        slab = nrows // nsteps
        assert nrows % nsteps == 0 and slab % BF16_ROWS == 0, (nrows, nsteps)
        cols = src.shape[1]
        in_specs.append(pl.BlockSpec((slab, cols), lambda *g, base=layer * nsteps: (base + step_of(*g), 0)))
        out_specs.append(pl.BlockSpec((slab, cols), lambda *g: (step_of(*g), 0)))
        out_shapes.append(jax.ShapeDtypeStruct((nrows, cols), BF16))
    return in_specs, out_specs, out_shapes


def _cast_slabs(src_refs, dst_refs):
    for src, dst in zip(src_refs, dst_refs):
        dst[...] = src[...].astype(BF16)


def _log_gamma():
    return np.log1p(-np.exp2(-5.0 - np.arange(RET_HEADS, dtype=np.float64)))


def _retention_consts(c):
    lg = _log_gamma()
    idx = np.arange(c, dtype=np.float64)
    diff = idx[:, None] - idx[None, :]
    dmask = np.where(diff >= 0, np.exp(lg[:, None, None] * np.maximum(diff, 0.0)), 0.0) * RET_DK ** -0.5
    qdec = np.repeat(np.exp(lg[:, None] * (idx + 1.0)).T, RET_DK, axis=1)
    kdec = np.repeat(np.exp(lg[:, None] * (c - 1.0 - idx)).T, RET_DK, axis=1) * RET_DK ** -0.5
    cdec = np.broadcast_to(np.repeat(np.exp(lg * c), RET_DK)[:, None], (RET_W, RET_DV))
    return dmask, qdec, kdec, cdec


def _slopes():
    return np.exp2(-8.0 * (np.arange(ATT_HEADS, dtype=np.float64) + 1.0) / ATT_HEADS)


def _prompt_attn_bias():
    s = np.arange(2 * BLK)[:, None]
    q = np.arange(BLK)[None, :]
    dist = q + BLK - s
    allowed = (dist >= 0) & (dist < WINDOW)
    per_head = np.where(allowed[None], -_slopes()[:, None, None] * dist[None] * LOG2E, NEG)
    out = [np.concatenate([per_head[ATT_GROUP * j + par], per_head[ATT_GROUP * j + par + 2]], axis=1)
           for j in range(ATT_KV_HEADS) for par in range(2)]
    return np.stack(out)


def _sample_attn_bias(t):
    i = np.arange(t)[:, None]
    col = np.arange(2 * BLK)[None, :]
    new = col >= 2 * BLK - t
    dist = np.where(new, i - (col - (2 * BLK - t)), i + WINDOW - col)
    allowed = (dist >= 0) & (dist < WINDOW) & ((col < WINDOW) | new)
    b = np.where(allowed[None], -_slopes()[:, None, None] * dist[None], NEG)
    return b.reshape(ATT_HEADS * t, 2 * BLK)


N_FFN_IN = 9


def _ffn_body(layer, n_cast, *refs):
    x_ref, p_ref, gpre_ref, w1_ref, w2_ref, gpost_ref, gple_ref, wg_ref, wp_ref = refs[:N_FFN_IN]
    o_ref = refs[N_FFN_IN + n_cast]
    _cast_slabs(refs[N_FFN_IN:N_FFN_IN + n_cast], refs[N_FFN_IN + n_cast + 1:])
    tile = x_ref.shape[0]
    nchunk = D_FF // FFN_CHUNK
    rows = [slice(r0, r0 + FFN_ROW_GROUP) for r0 in range(0, tile, FFN_ROW_GROUP)]

    def up(hh, c):
        a = _dot(hh, w1_ref[:, c * FFN_CHUNK:(c + 1) * FFN_CHUNK])
        return jnp.square(jnp.maximum(a, 0.0)).astype(BF16)

    def down(a, c):
        return _dot(a, w2_ref[c * FFN_CHUNK:(c + 1) * FFN_CHUNK, :])

    gpre, gpost, gple = (_gain_row(g, layer) for g in (gpre_ref, gpost_ref, gple_ref))
    h_parts = [_rms(x_ref[rw, :], gpre).astype(BF16) for rw in rows]
    a0 = jnp.concatenate([up(hp, 0) for hp in h_parts], axis=0)
    h = jnp.concatenate(h_parts, axis=0)
    acc = down(a0, 0)
    for c in range(1, nchunk - 1):
        acc = acc + down(up(h, c), c)
    a_last = up(h, nchunk - 1)
    f_parts = [acc[rw] + down(a_last[rw], nchunk - 1) for rw in rows]
    pp = _dot(p_ref[...].astype(BF16), wp_ref[...])
    for rw, f in zip(rows, f_parts):
        x2 = x_ref[rw, :] + _rms(f, gpost)
        gate = jax.nn.sigmoid(_dot(_rms(x2, gple).astype(BF16), wg_ref[...]))
        o_ref[rw, :] = x2 + gate * pp[rw]


def _ffn_call(layer, x, p, gpre, w1, w2, gpost, gple, wg, wp, cast_jobs=()):
    n = x.shape[0]
    tile = min(FFN_TILE, n)
    row = lambda i: (i, 0)
    cast_in, cast_out, cast_shapes = _cast_specs(cast_jobs, n // tile, lambda i: i)
    return pl.pallas_call(
        functools.partial(_ffn_body, layer, len(cast_jobs)),
        grid=(n // tile,),
        in_specs=[
            pl.BlockSpec((tile, D_MODEL), row),
            pl.BlockSpec((None, tile, D_PLE), lambda i: (layer, i, 0)),
            _const_spec(gpre.shape),
            _const_spec((D_MODEL, D_FF)),
            _const_spec((D_FF, D_MODEL)),
            _const_spec(gpost.shape),
            _const_spec(gple.shape),
            _const_spec((D_MODEL, D_MODEL)),
            _const_spec((D_PLE, D_MODEL)),
        ] + cast_in,
        out_specs=[pl.BlockSpec((tile, D_MODEL), row)] + cast_out,
        out_shape=[jax.ShapeDtypeStruct((n, D_MODEL), F32)] + cast_shapes,
        compiler_params=pltpu.CompilerParams(
            dimension_semantics=("arbitrary",), vmem_limit_bytes=VMEM_LIMIT),
        name="ffn",
    )(x, p, gpre, w1, w2, gpost, gple, wg, wp, *[j[0] for j in cast_jobs])


N_PROMPT_IN = 13


def _mix_prompt_body(layer, n_cast, *refs):
    (sink_ref, x_ref, gpre_ref, win_ref, convw_ref, bias_ref, dmask_ref, qdec_ref, kdec_ref, cdec_ref,
     wbr_ref, wout_ref, gpost_ref) = refs[:N_PROMPT_IN]
    o_ref, ret_ref, wk_ref, wv_ref, conv_ref = refs[N_PROMPT_IN + n_cast:N_PROMPT_IN + n_cast + 5]
    br_scr, att_scr, u_scr = refs[N_PROMPT_IN + 2 * n_cast + 5:]
    _cast_slabs(refs[N_PROMPT_IN:N_PROMPT_IN + n_cast],
                refs[N_PROMPT_IN + n_cast + 5:N_PROMPT_IN + 2 * n_cast + 5])
    tt = x_ref.shape[1]
    nblk = tt // BLK
    first = pl.program_id(1) == 0

    @pl.when(first)
    def _init():
        ret_ref[...] = jnp.zeros_like(ret_ref)
        wk_ref[...] = jnp.zeros_like(wk_ref)
        wv_ref[...] = jnp.zeros_like(wv_ref)
        u_scr[0:8, :] = jnp.zeros((8, CONV_DIM), F32)

    lane256 = lax.broadcasted_iota(jnp.int32, (1, 256), 1)
    zero_bf = jnp.zeros((), BF16)

    rows = [slice(i * tt // ROW_SPLIT, (i + 1) * tt // ROW_SPLIT) for i in range(ROW_SPLIT)]
    h_parts = [_rms(x_ref[0, rw, :], _gain_row(gpre_ref, layer)).astype(BF16) for rw in rows]
    zr = jnp.concatenate([_dot(hp, win_ref[:, OFF_RET:OFF_RET + 1536]) for hp in h_parts], axis=0)
    h = jnp.concatenate(h_parts, axis=0)

    def gate_slice(n, c, width):
        c0 = OFF_GATE + n * D_MODEL + c * width
        return jax.nn.sigmoid(_dot(h, win_ref[:, c0:c0 + width]))

    z_ch = _dot(h, win_ref[:, OFF_CONV + CONV_DIM:OFF_CONV + 3 * CONV_DIM])
    u_scr[8:8 + tt, :] = z_ch[:, 0:CONV_DIM] * z_ch[:, CONV_DIM:2 * CONV_DIM]
    taps = (u_scr[6:6 + tt, :] * convw_ref[0:1, :] + u_scr[7:7 + tt, :] * convw_ref[1:2, :]
            + u_scr[8:8 + tt, :] * convw_ref[2:3, :])
    z_b = _dot(h, win_ref[:, OFF_CONV:OFF_CONV + CONV_DIM])

    def conv_rows(c):
        r = slice(c * BLK, (c + 1) * BLK)
        return (z_b[r] * taps[r]).astype(BF16)

    head_masks = [(lane256 // RET_DK) == hd for hd in range(RET_HEADS)]

    def head_stack(a):
        return jnp.concatenate([jnp.where(hm, a, zero_bf) for hm in head_masks], axis=0)

    def values(c, hd):
        return zr[c * BLK:(c + 1) * BLK, 512 + hd * RET_DV:512 + (hd + 1) * RET_DV].astype(BF16)

    kts, incs = [], []
    for c in range(nblk):
        kt = zr[c * BLK:(c + 1) * BLK, 256:512].T
        kdt = (kt * kdec_ref[...]).astype(BF16)
        kts.append(kt.astype(BF16))
        incs.append(jnp.concatenate(
            [_dot(kdt[hd * RET_DK:(hd + 1) * RET_DK], values(c, hd)) for hd in range(RET_HEADS)], axis=0))

    def scores_and_cross(c, state):
        q_st = head_stack(zr[c * BLK:(c + 1) * BLK, 0:256].astype(BF16))
        return _dot(q_st, jnp.concatenate([kts[c], state.astype(BF16)], axis=1))

    gate_w = D_MODEL // nblk
    gate0, gate2, conv_out = [], [], []
    state = ret_ref[0]
    fused_next = scores_and_cross(0, state)
    for c in range(nblk):
        r = slice(c * BLK, (c + 1) * BLK)
        fused = fused_next
        state = state * cdec_ref[...] + incs[c]
        if c + 1 < nblk:
            fused_next = scores_and_cross(c + 1, state)
        gate0.append(gate_slice(0, c, gate_w))
        gate2.append(gate_slice(2, c, gate_w))
        conv_out.append(conv_rows(c))
        inner = (fused[:, 0:BLK] * dmask_ref[...]).astype(BF16)
        cross = fused[:, BLK:2 * BLK] * qdec_ref[...]
        for hd in range(RET_HEADS):
            hq = slice(hd * BLK, (hd + 1) * BLK)
            o = _dot(inner[hq], values(c, hd)) + cross[hq]
            o = o * lax.rsqrt(jnp.mean(o * o, axis=-1, keepdims=True) + EPS)
            g = zr[r, 1024 + hd * RET_DV:1024 + (hd + 1) * RET_DV]
            br_scr[r, hd * RET_DV:(hd + 1) * RET_DV] = (g * jax.nn.sigmoid(g) * o).astype(BF16)
    ret_ref[0] = state
    tail = u_scr[6 + tt:8 + tt, :]
    conv_ref[0] = tail
    u_scr[6:8, :] = tail
    mixed = (jnp.concatenate(gate0, axis=1) * _dot(br_scr[...], wbr_ref[0])
             + jnp.concatenate(gate2, axis=1) * _dot(jnp.concatenate(conv_out, axis=0), wbr_ref[2]))

    za = _dot(h, win_ref[:, OFF_ATT:OFF_ATT + 768])
    low = lax.broadcasted_iota(jnp.int32, (1, 128), 1) < ATT_DH
    key_row = lax.broadcasted_iota(jnp.int32, (2 * BLK, 1), 0)
    no_prev = jnp.where(key_row < BLK, jnp.where(first, NEG, 0.0), 0.0).astype(F32)
    sink_rows = [jnp.where(lane256 < BLK, sink_ref[layer, ATT_GROUP * j + par] * LOG2E,
                           sink_ref[layer, ATT_GROUP * j + par + 2] * LOG2E)
                 for j in range(ATT_KV_HEADS) for par in range(2)]
    top = lax.broadcasted_iota(jnp.int32, (KV_W, 1), 0) < ATT_DH

    def value_variants(vblk):
        vt = vblk.T
        return [jnp.where(top, vt, 1.0).astype(BF16), jnp.where(top, 1.0, vt).astype(BF16)]

    def key_variants(kblk):
        rolled = pltpu.roll(kblk, ATT_DH, 1)
        return [jnp.where(low, kblk, 0.0).astype(BF16), jnp.where(low, 0.0, rolled).astype(BF16),
                jnp.where(low, rolled, 0.0).astype(BF16), jnp.where(low, 0.0, kblk).astype(BF16)]

    def scores(i, k_prev):
        r = slice(i * BLK, (i + 1) * BLK)
        aq = (za[r, 0:512] * (ATT_DH ** -0.5 * LOG2E)).astype(BF16)
        k_cur = key_variants(za[r, 512:640])
        out = []
        for j in range(ATT_KV_HEADS):
            wq = jnp.concatenate([aq[:, j * 256:j * 256 + 128], aq[:, j * 256 + 128:j * 256 + 256]], axis=0)
            for par in range(2):
                v = 2 * j + par
                out.append(_dot_nt(jnp.concatenate([k_prev[v], k_cur[v]], axis=0), wq))
        return out, k_cur

    gate1 = []
    s_next, k_prev = scores(0, key_variants(wk_ref[0]))
    vt_prev = value_variants(wv_ref[0])
    for i in range(nblk):
        r = slice(i * BLK, (i + 1) * BLK)
        s_cur = s_next
        if i + 1 < nblk:
            s_next, k_prev = scores(i + 1, k_prev)
        gate1.append(gate_slice(1, i, gate_w))
        ps, sink_ps = [], []
        for v in range(2 * ATT_KV_HEADS):
            s = s_cur[v] + bias_ref[v]
            if i == 0:
                s = s + no_prev
            m = jnp.maximum(jnp.max(s, axis=0, keepdims=True), sink_rows[v])
            ps.append(jnp.exp2(s - m).astype(BF16))
            sink_ps.append(jnp.exp2(sink_rows[v] - m))
        vt_cur = value_variants(za[r, 640:768])
        heads = [None] * ATT_HEADS
        for j in range(ATT_KV_HEADS):
            vt2 = jnp.concatenate([vt_prev[j], vt_cur[j]], axis=1)
            ones_row = (1 - j) * ATT_DH
            for par in range(2):
                v = 2 * j + par
                pv = _dot(vt2, ps[v])
                rden = 1.0 / (pv[ones_row:ones_row + 1] + sink_ps[v])
                ot = pv[j * ATT_DH:(j + 1) * ATT_DH] * rden
                heads[ATT_GROUP * j + par] = ot[:, 0:BLK]
                heads[ATT_GROUP * j + par + 2] = ot[:, BLK:2 * BLK]
        att_scr[r, :] = jnp.concatenate(heads, axis=0).T.astype(BF16)
        vt_prev = vt_cur
    wk_ref[0] = za[tt - BLK:tt, 512:640]
    wv_ref[0] = za[tt - BLK:tt, 640:768]
    mixed = mixed + jnp.concatenate(gate1, axis=1) * _dot(att_scr[...], wbr_ref[1])

    mixed = mixed.astype(BF16)
    gpost = _gain_row(gpost_ref, layer)
    for rw in rows:
        o_ref[0, rw, :] = x_ref[0, rw, :] + _rms(_dot(mixed[rw], wout_ref[...]), gpost)


def _mix_prompt_call(layer, x, sinks, gpre, win, convw, wbr, wout, gpost, cast_jobs=()):
    b, t, _ = x.shape
    tt = min(PROMPT_TILE, t)
    nt = t // tt
    cast_in, cast_out, cast_shapes = _cast_specs(cast_jobs, b * nt, lambda bi, ti: bi * nt + ti)
    dmask, qdec, kdec, cdec = _retention_consts(BLK)
    dmask = jnp.asarray(dmask.reshape(RET_HEADS * BLK, BLK), F32)
    qdec = jnp.asarray(np.broadcast_to(qdec[:, ::RET_DK].T.reshape(RET_HEADS * BLK, 1), (RET_HEADS * BLK, BLK)), F32)
    kdec = jnp.asarray(kdec.T, F32)
    cdec = jnp.asarray(cdec, F32)
    bias = jnp.asarray(_prompt_attn_bias(), F32)
    per_b = lambda bi, ti: (bi, 0, 0)
    return pl.pallas_call(
        functools.partial(_mix_prompt_body, layer, len(cast_jobs)),
        grid=(b, nt),
        in_specs=[
            pl.BlockSpec(memory_space=pltpu.SMEM),
            pl.BlockSpec((1, tt, D_MODEL), lambda bi, ti: (bi, ti, 0)),
            _const_spec(gpre.shape),
            _const_spec((D_MODEL, N_IN)),
            _layer_spec(layer, (CONV_W, CONV_DIM)),
            _const_spec(bias.shape),
            _const_spec(dmask.shape),
            _const_spec(qdec.shape),
            _const_spec(kdec.shape),
            _const_spec(cdec.shape),
            _const_spec((N_BRANCH, BRANCH_W, D_MODEL)),
            _const_spec((D_MODEL, D_MODEL)),
            _const_spec(gpost.shape),
        ] + cast_in,
        out_specs=[
            pl.BlockSpec((1, tt, D_MODEL), lambda bi, ti: (bi, ti, 0)),
            pl.BlockSpec((1, RET_W, RET_DV), per_b),
            pl.BlockSpec((1, WINDOW, KV_W), per_b),
            pl.BlockSpec((1, WINDOW, KV_W), per_b),
            pl.BlockSpec((1, CONV_W - 1, CONV_DIM), per_b),
        ] + cast_out,
        out_shape=[
            jax.ShapeDtypeStruct((b, t, D_MODEL), F32),
            jax.ShapeDtypeStruct((b, RET_W, RET_DV), F32),
            jax.ShapeDtypeStruct((b, WINDOW, KV_W), F32),
            jax.ShapeDtypeStruct((b, WINDOW, KV_W), F32),
            jax.ShapeDtypeStruct((b, CONV_W - 1, CONV_DIM), F32),
        ] + cast_shapes,
        scratch_shapes=[
            pltpu.VMEM((tt, BRANCH_W), BF16),
            pltpu.VMEM((tt, BRANCH_W), BF16),
            pltpu.VMEM((tt + 8, CONV_DIM), F32),
        ],
        compiler_params=pltpu.CompilerParams(
            dimension_semantics=("arbitrary", "arbitrary"), vmem_limit_bytes=VMEM_LIMIT),
        name="mix_prompt",
    )(sinks, x, gpre, win, convw, bias, dmask, qdec, kdec, cdec, wbr, wout, gpost, *[j[0] for j in cast_jobs])


N_SAMPLE_IN = 17
N_SAMPLE_STATE = 4


def _mix_sample_body(layer, n_alias, *refs):
    (x_ref, sret_ref, kt_ref, vt_ref, sconv_ref, gpre_ref, win_ref, convw_ref, bias_ref, sinkrow_ref,
     dmask_ref, qdec_ref, kdec_ref, cdec_ref, wbr_ref, wout_ref, gpost_ref) = refs[:N_SAMPLE_IN]
    o_ref, ret_ref, wk_ref, wv_ref, conv_ref = refs[N_SAMPLE_IN + n_alias:N_SAMPLE_IN + n_alias + 5]
    z_scr, oc_scr, att_scr, cv_scr, br_scr, u_scr = refs[N_SAMPLE_IN + n_alias + 5:]
    rows = x_ref.shape[0]
    nseq = sret_ref.shape[0]
    slabs = range(ret_ref.shape[0])
    t = rows // nseq

    x = x_ref[...]
    h = _rms(x, _gain_row(gpre_ref, layer)).astype(BF16)
    lane256 = lax.broadcasted_iota(jnp.int32, (1, 256), 1)
    lane128 = lax.broadcasted_iota(jnp.int32, (1, 128), 1)
    zero_bf = jnp.zeros((), BF16)
    head_masks = [(lane256 // RET_DK) == hd for hd in range(RET_HEADS)]
    new_lanes = lane128 >= BLK - t
    zpad = jnp.zeros((BLK - t, KV_W), F32)
    snk = sinkrow_ref[:, 0:1]

    z_scr[...] = _dot(h, win_ref[:, 0:OFF_GATE])

    def new_rows_t(rs, c0):
        return jnp.concatenate([zpad, z_scr[rs, c0:c0 + KV_W]], axis=0).T

    ngroups = -(-nseq // SEQ_GROUP)
    gate_w = N_BRANCH * D_MODEL // ngroups
    gate_parts = []
    for g0 in range(0, nseq, SEQ_GROUP):
        seqs = list(range(g0, min(g0 + SEQ_GROUP, nseq)))
        rsl = {b: slice(b * t, (b + 1) * t) for b in seqs}

        for b in seqs:
            rs = rsl[b]
            qd = z_scr[rs, 0:256] * qdec_ref[...]
            qm = jnp.concatenate([jnp.where(hm, qd, 0.0) for hm in head_masks], axis=0).astype(BF16)
            s_old = sret_ref[b]
            oc = _dot(qm, s_old.astype(BF16))
            for hd in range(RET_HEADS):
                oc_scr[hd, rs, :] = oc[hd * t:(hd + 1) * t]
            kdt = (z_scr[rs, 256:512] * kdec_ref[...]).T.astype(BF16)
            vb = z_scr[rs, 512:1024].astype(BF16)
            for hd in range(RET_HEADS):
                hr = slice(hd * RET_DK, (hd + 1) * RET_DK)
                s_new = s_old[hr] * cdec_ref[hr, :] + _dot(kdt[hr], vb[:, hd * RET_DV:(hd + 1) * RET_DV])
                for l in slabs:
                    ret_ref[l, b, hr, :] = s_new

        s_raw, nkt = {}, {}
        for b in seqs:
            rs = rsl[b]
            aq = z_scr[rs, 1536:2048] * (ATT_DH ** -0.5)
            pieces = []
            for hd in range(ATT_HEADS):
                piece = aq[:, (hd // 2) * 128:(hd // 2 + 1) * 128]
                if hd % 2 != hd // ATT_GROUP:
                    piece = pltpu.roll(piece, ATT_DH, 1)
                pieces.append(jnp.where((lane128 // ATT_DH) == hd // ATT_GROUP, piece, 0.0))
            qs = jnp.concatenate(pieces, axis=0).astype(BF16)
            nkt[b] = new_rows_t(rs, 2048)
            k2t = jnp.concatenate([kt_ref[b], nkt[b]], axis=1).astype(BF16)
            s_raw[b] = _dot(qs, k2t)

        c0 = OFF_GATE + (g0 // SEQ_GROUP) * gate_w
        gate_parts.append(jax.nn.sigmoid(_dot(h, win_ref[:, c0:c0 + gate_w])))

        probs = {}
        for b in seqs:
            s = s_raw[b] + bias_ref[...]
            m = jnp.maximum(jnp.max(s, axis=-1, keepdims=True), snk)
            p = jnp.exp(s - m)
            rden = 1.0 / (jnp.sum(p, axis=-1, keepdims=True) + jnp.exp(snk - m))
            probs[b] = (p * rden).astype(BF16)

        for b in seqs:
            rs = rsl[b]
            nvt = new_rows_t(rs, 2176)
            v2t = jnp.concatenate([vt_ref[b], nvt], axis=1).astype(BF16)
            o = _dot_nt(probs[b], v2t)
            for pr in range(ATT_HEADS // 2):
                lo = o[(2 * pr) * t:(2 * pr + 1) * t]
                hi = o[(2 * pr + 1) * t:(2 * pr + 2) * t]
                if (2 * pr) // ATT_GROUP == 1:
                    lo = pltpu.roll(lo, ATT_DH, 1)
                if (2 * pr + 1) // ATT_GROUP == 0:
                    hi = pltpu.roll(hi, ATT_DH, 1)
                att_scr[rs, pr * 128:(pr + 1) * 128] = jnp.where(lane128 < ATT_DH, lo, hi)
            k_new = jnp.where(new_lanes, nkt[b], pltpu.roll(kt_ref[b], BLK - t, 1))
            v_new = jnp.where(new_lanes, nvt, pltpu.roll(vt_ref[b], BLK - t, 1))
            for l in slabs:
                wk_ref[l, b] = k_new
                wv_ref[l, b] = v_new

        for b in seqs:
            rs = rsl[b]
            u = z_scr[rs, 2816:3328] * z_scr[rs, 3328:3840]
            u_scr[b, 6:8, :] = sconv_ref[b]
            u_scr[b, 8:8 + t, :] = u
            y = (u_scr[b, 6:6 + t, :] * convw_ref[0:1, :] + u_scr[b, 7:7 + t, :] * convw_ref[1:2, :]
                 + u * convw_ref[2:3, :])
            cv_scr[rs, :] = z_scr[rs, 2304:2816] * y
            for l in slabs:
                conv_ref[l, b] = u_scr[b, 6 + t:8 + t, :]

    qb = z_scr[:, 0:256].astype(BF16)
    kb = z_scr[:, 256:512].astype(BF16)
    for hd in range(RET_HEADS):
        inner = _dot_nt(jnp.where(head_masks[hd], qb, zero_bf), kb) * dmask_ref[hd]
        vh = z_scr[:, 512 + hd * RET_DV:512 + (hd + 1) * RET_DV].astype(BF16)
        o = _dot(inner.astype(BF16), vh) + oc_scr[hd]
        o = o * lax.rsqrt(jnp.mean(o * o, axis=-1, keepdims=True) + EPS)
        g = z_scr[:, 1024 + hd * RET_DV:1024 + (hd + 1) * RET_DV]
        br_scr[:, hd * RET_DV:(hd + 1) * RET_DV] = (g * jax.nn.sigmoid(g) * o).astype(BF16)
    gates = jnp.concatenate(gate_parts, axis=1)
    mixed = (gates[:, 0:D_MODEL] * _dot(br_scr[...], wbr_ref[0])
             + gates[:, D_MODEL:2 * D_MODEL] * _dot(att_scr[...].astype(BF16), wbr_ref[1])
             + gates[:, 2 * D_MODEL:3 * D_MODEL] * _dot(cv_scr[...].astype(BF16), wbr_ref[2]))
    o_ref[...] = x + _rms(_dot(mixed.astype(BF16), wout_ref[...]), _gain_row(gpost_ref, layer))


def _mix_sample_call(layer, x2, t, sinkrow, sret, kt, vt, sconv, gpre, win, convw, wbr, wout, gpost, prev):
    depth, nb = sret.shape[0], sret.shape[1]
    sb = min(SAMPLE_SEQS, nb)
    rows = sb * t
    dm, qd, kd, cdec = _retention_consts(t)
    same_seq = np.kron(np.eye(sb), np.ones((t, t)))
    dmask = jnp.asarray(np.tile(dm, (1, sb, sb)) * same_seq[None], F32)
    qdec = jnp.asarray(qd, F32)
    kdec = jnp.asarray(kd, F32)
    cdec = jnp.asarray(cdec, F32)
    bias = jnp.asarray(_sample_attn_bias(t), F32)
    row = lambda i: (i, 0)
    seq = lambda i: (layer, i, 0, 0)
    state_shapes = [(RET_W, RET_DV), (KV_W, WINDOW), (KV_W, WINDOW), (CONV_W - 1, CONV_DIM)]
    state_specs = [pl.BlockSpec((None, sb) + s, seq) for s in state_shapes]
    if prev is None:
        out_state_specs = [pl.BlockSpec((depth, sb) + s, lambda i: (0, i, 0, 0)) for s in state_shapes]
    else:
        out_state_specs = [pl.BlockSpec((1, sb) + s, seq) for s in state_shapes]
    n_alias = 0 if prev is None else N_SAMPLE_STATE
    alias_args = [] if prev is None else list(prev)
    return pl.pallas_call(
        functools.partial(_mix_sample_body, layer, n_alias),
        grid=(nb // sb,),
        in_specs=[pl.BlockSpec((rows, D_MODEL), row)] + state_specs + [
            _const_spec(gpre.shape),
            _const_spec((D_MODEL, N_IN)),
            _layer_spec(layer, (CONV_W, CONV_DIM)),
            _const_spec(bias.shape),
            _layer_spec(layer, (ATT_HEADS * t, 128)),
            _const_spec(dmask.shape),
            _const_spec(qdec.shape),
            _const_spec(kdec.shape),
            _const_spec(cdec.shape),
            _const_spec((N_BRANCH, BRANCH_W, D_MODEL)),
            _const_spec((D_MODEL, D_MODEL)),
            _const_spec(gpost.shape),
        ] + [pl.BlockSpec(memory_space=pl.ANY)] * n_alias,
        out_specs=[pl.BlockSpec((rows, D_MODEL), row)] + out_state_specs,
        out_shape=[jax.ShapeDtypeStruct((nb * t, D_MODEL), F32)]
        + [jax.ShapeDtypeStruct((depth, nb) + s, F32) for s in state_shapes],
        input_output_aliases={N_SAMPLE_IN + k: 1 + k for k in range(n_alias)},
        scratch_shapes=[
            pltpu.VMEM((rows, OFF_GATE), F32),
            pltpu.VMEM((RET_HEADS, rows, RET_DV), F32),
            pltpu.VMEM((rows, BRANCH_W), F32),
            pltpu.VMEM((rows, BRANCH_W), F32),
            pltpu.VMEM((rows, BRANCH_W), BF16),
            pltpu.VMEM((sb, 8 + t, CONV_DIM), F32),
        ],
        compiler_params=pltpu.CompilerParams(
            dimension_semantics=("arbitrary",), vmem_limit_bytes=VMEM_LIMIT),
        name="mix_sample",
    )(x2, sret, kt, vt, sconv, gpre, win, convw, bias, sinkrow, dmask, qdec, kdec, cdec, wbr, wout, gpost,
      *alias_args)


def kernel(x_prompt, x_sample, p_prompt, p_sample, state_ret, cache_win_k, cache_win_v, state_conv,
           g_mix_pre, w_in, conv_w, attn_sinks, w_branch, w_out, g_mix_post,
           g_ffn_pre, w_ff1, w_ff2, g_ffn_post, g_ple, w_ple_gate, w_ple_proj):
    depth = w_in.shape[0]
    bp, tp, d = x_prompt.shape
    bs, ts, _ = x_sample.shape
    win_w = cache_win_k.shape[2]
    assert d == D_MODEL and w_in.shape[2] == N_IN and tp % BLK == 0 and math.gcd(ts, BLK) == ts
    assert win_w == WINDOW == BLK and ts <= BLK

    gpre, gpost = g_mix_pre, g_mix_post
    gfpre, gfpost, gple = g_ffn_pre, g_ffn_post, g_ple
    rows2d = lambda a: a.reshape(-1, a.shape[-1])
    mix_w = (w_in[0].astype(BF16), w_branch[0].astype(BF16), w_out[0].astype(BF16))
    wp_all = w_ple_proj.astype(BF16)
    pp = p_prompt.reshape(depth, bp * tp, D_PLE)
    ps = p_sample.reshape(depth, bs * ts, D_PLE)
    sret = state_ret.reshape(depth, bs, RET_W, RET_DV)
    kt = jnp.transpose(cache_win_k, (0, 1, 3, 4, 2)).reshape(depth, bs, KV_W, win_w)
    vt = jnp.transpose(cache_win_v, (0, 1, 3, 4, 2)).reshape(depth, bs, KV_W, win_w)
    sinkrow = jnp.broadcast_to(jnp.repeat(attn_sinks, ts, axis=1)[:, :, None], (depth, ATT_HEADS * ts, 128))

    yp = x_prompt
    ys = x_sample.reshape(bs * ts, d)
    outs_p, sample_states = [], None
    for l in range(depth):
        win, wbr, wout = mix_w
        ffn_jobs = [(rows2d(w_ff1), l, D_MODEL), (rows2d(w_ff2), l, D_FF), (rows2d(w_ple_gate), l, D_MODEL)]
        yp, r, k, v, c, w1, w2, wg = _mix_prompt_call(l, yp, attn_sinks, gpre, win, conv_w, wbr, wout, gpost,
                                                      ffn_jobs)
        outs_p.append((r, k, v, c))
        ffn_w = (gfpre, w1, w2, gfpost, gple, wg, wp_all[l])
        next_jobs = [] if l + 1 == depth else [(rows2d(w_in), l + 1, D_MODEL),
                                               (rows2d(w_branch), l + 1, N_BRANCH * BRANCH_W),
                                               (rows2d(w_out), l + 1, D_MODEL)]
        yp, *next_w = _ffn_call(l, yp.reshape(bp * tp, d), pp, *ffn_w, next_jobs)
        yp = yp.reshape(bp, tp, d)

        ys, *sample_states = _mix_sample_call(l, ys, ts, sinkrow, sret, kt, vt, state_conv, gpre, win, conv_w,
                                              wbr, wout, gpost, sample_states)
        ys, = _ffn_call(l, ys, ps, *ffn_w)
        if next_w:
            mix_w = (next_w[0], next_w[1].reshape(N_BRANCH, BRANCH_W, D_MODEL), next_w[2])

    def stack(i, shape):
        return jnp.stack([o[i] for o in outs_p]).reshape(shape)

    def untranspose(a):
        return jnp.transpose(a.reshape(depth, bs, ATT_KV_HEADS, ATT_DH, win_w), (0, 1, 4, 2, 3))

    rs, kts, vts, cs = sample_states
    return (yp, ys.reshape(bs, ts, d),
            stack(0, (depth, bp, RET_HEADS, RET_DK, RET_DV)),
            stack(1, (depth, bp, WINDOW, ATT_KV_HEADS, ATT_DH)),
            stack(2, (depth, bp, WINDOW, ATT_KV_HEADS, ATT_DH)),
            stack(3, (depth, bp, CONV_W - 1, CONV_DIM)),
            rs.reshape(depth, bs, RET_HEADS, RET_DK, RET_DV), untranspose(kts), untranspose(vts), cs)
```

```python
import functools
import math

import numpy as np
import jax
import jax.numpy as jnp
from jax import lax
from jax.experimental import pallas as pl
from jax.experimental.pallas import tpu as pltpu

F32 = jnp.float32
BF16 = jnp.bfloat16

D_MODEL = 1024
BRANCH_W = 512
N_BRANCH = 3
RET_HEADS = 4
RET_DK = 64
RET_DV = 128
RET_W = RET_HEADS * RET_DK
ATT_HEADS = 8
ATT_KV_HEADS = 2
ATT_GROUP = ATT_HEADS // ATT_KV_HEADS
ATT_DH = 64
KV_W = ATT_KV_HEADS * ATT_DH
WINDOW = 128
CONV_DIM = 512
CONV_W = 3
D_FF = 4096
D_PLE = 256
EPS = 1e-6

OFF_RET = 0
OFF_ATT = 1536
OFF_CONV = 2304
OFF_GATE = 3840
N_IN = 6912

BLK = 128
BF16_ROWS = 16
LOG2E = math.log2(math.e)
NEG = -1e30
VMEM_LIMIT = 56 * 1024 * 1024

PROMPT_TILE = 512
ROW_SPLIT = 4
FFN_TILE = 1024
FFN_ROW_GROUP = 128
FFN_CHUNK = 1024
SAMPLE_SEQS = 16
SEQ_GROUP = 16


def _dot(a, b):
    return jnp.dot(a, b, preferred_element_type=F32)


def _dot_nt(a, b):
    return lax.dot_general(a, b, (((1,), (1,)), ((), ())), preferred_element_type=F32)


def _rms(x, g):
    return x * lax.rsqrt(jnp.mean(x * x, axis=-1, keepdims=True) + EPS) * g


def _gain_row(g_ref, layer):
    return g_ref[layer:layer + 1, :]


def _const_spec(shape):
    nd = len(shape)
    return pl.BlockSpec(shape, lambda *_: (0,) * nd, pipeline_mode=pl.Buffered(1))


def _layer_spec(layer, shape):
    nd = len(shape)
    return pl.BlockSpec((None,) + tuple(shape), lambda *_: (layer,) + (0,) * nd, pipeline_mode=pl.Buffered(1))


def _cast_specs(jobs, nsteps, step_of):
    in_specs, out_specs, out_shapes = [], [], []
    for src, layer, nrows in jobs:
        slab = nrows // nsteps
        assert nrows % nsteps == 0 and slab % BF16_ROWS == 0, (nrows, nsteps)
        cols = src.shape[1]
        in_specs.append(pl.BlockSpec((slab, cols), lambda *g, base=layer * nsteps: (base + step_of(*g), 0)))
        out_specs.append(pl.BlockSpec((slab, cols), lambda *g: (step_of(*g), 0)))
        out_shapes.append(jax.ShapeDtypeStruct((nrows, cols), BF16))
    return in_specs, out_specs, out_shapes


def _cast_slabs(src_refs, dst_refs):
    for src, dst in zip(src_refs, dst_refs):
        dst[...] = src[...].astype(BF16)


def _log_gamma():
    return np.log1p(-np.exp2(-5.0 - np.arange(RET_HEADS, dtype=np.float64)))


def _retention_consts(c):
    lg = _log_gamma()
    idx = np.arange(c, dtype=np.float64)
    diff = idx[:, None] - idx[None, :]
    dmask = np.where(diff >= 0, np.exp(lg[:, None, None] * np.maximum(diff, 0.0)), 0.0) * RET_DK ** -0.5
    qdec = np.repeat(np.exp(lg[:, None] * (idx + 1.0)).T, RET_DK, axis=1)
    kdec = np.repeat(np.exp(lg[:, None] * (c - 1.0 - idx)).T, RET_DK, axis=1) * RET_DK ** -0.5
    cdec = np.broadcast_to(np.repeat(np.exp(lg * c), RET_DK)[:, None], (RET_W, RET_DV))
    return dmask, qdec, kdec, cdec


def _slopes():
    return np.exp2(-8.0 * (np.arange(ATT_HEADS, dtype=np.float64) + 1.0) / ATT_HEADS)


def _prompt_attn_bias():
    s = np.arange(2 * BLK)[:, None]
    q = np.arange(BLK)[None, :]
    dist = q + BLK - s
    allowed = (dist >= 0) & (dist < WINDOW)
    per_head = np.where(allowed[None], -_slopes()[:, None, None] * dist[None] * LOG2E, NEG)
    out = [np.concatenate([per_head[ATT_GROUP * j + par], per_head[ATT_GROUP * j + par + 2]], axis=1)
           for j in range(ATT_KV_HEADS) for par in range(2)]
    return np.stack(out)


def _sample_attn_bias(t):
    i = np.arange(t)[:, None]
    col = np.arange(2 * BLK)[None, :]
    new = col >= 2 * BLK - t
    dist = np.where(new, i - (col - (2 * BLK - t)), i + WINDOW - col)
    allowed = (dist >= 0) & (dist < WINDOW) & ((col < WINDOW) | new)
    b = np.where(allowed[None], -_slopes()[:, None, None] * dist[None], NEG)
    return b.reshape(ATT_HEADS * t, 2 * BLK)


N_FFN_IN = 9


def _ffn_body(layer, n_cast, *refs):
    x_ref, p_ref, gpre_ref, w1_ref, w2_ref, gpost_ref, gple_ref, wg_ref, wp_ref = refs[:N_FFN_IN]
    o_ref = refs[N_FFN_IN + n_cast]
    _cast_slabs(refs[N_FFN_IN:N_FFN_IN + n_cast], refs[N_FFN_IN + n_cast + 1:])
    tile = x_ref.shape[0]
    nchunk = D_FF // FFN_CHUNK
    rows = [slice(r0, r0 + FFN_ROW_GROUP) for r0 in range(0, tile, FFN_ROW_GROUP)]

    def up(hh, c):
        a = _dot(hh, w1_ref[:, c * FFN_CHUNK:(c + 1) * FFN_CHUNK])
        return jnp.square(jnp.maximum(a, 0.0)).astype(BF16)

    def down(a, c):
        return _dot(a, w2_ref[c * FFN_CHUNK:(c + 1) * FFN_CHUNK, :])

    gpre, gpost, gple = (_gain_row(g, layer) for g in (gpre_ref, gpost_ref, gple_ref))
    h_parts = [_rms(x_ref[rw, :], gpre).astype(BF16) for rw in rows]
    a0 = jnp.concatenate([up(hp, 0) for hp in h_parts], axis=0)
    h = jnp.concatenate(h_parts, axis=0)
    acc = down(a0, 0)
    for c in range(1, nchunk - 1):
        acc = acc + down(up(h, c), c)
    a_last = up(h, nchunk - 1)
    f_parts = [acc[rw] + down(a_last[rw], nchunk - 1) for rw in rows]
    pp = _dot(p_ref[...].astype(BF16), wp_ref[...])
    for rw, f in zip(rows, f_parts):
        x2 = x_ref[rw, :] + _rms(f, gpost)
        gate = jax.nn.sigmoid(_dot(_rms(x2, gple).astype(BF16), wg_ref[...]))
        o_ref[rw, :] = x2 + gate * pp[rw]


def _ffn_call(layer, x, p, gpre, w1, w2, gpost, gple, wg, wp, cast_jobs=()):
    n = x.shape[0]
    tile = min(FFN_TILE, n)
    row = lambda i: (i, 0)
    cast_in, cast_out, cast_shapes = _cast_specs(cast_jobs, n // tile, lambda i: i)
    return pl.pallas_call(
        functools.partial(_ffn_body, layer, len(cast_jobs)),
        grid=(n // tile,),
        in_specs=[
            pl.BlockSpec((tile, D_MODEL), row),
            pl.BlockSpec((None, tile, D_PLE), lambda i: (layer, i, 0)),
            _const_spec(gpre.shape),
            _const_spec((D_MODEL, D_FF)),
            _const_spec((D_FF, D_MODEL)),
            _const_spec(gpost.shape),
            _const_spec(gple.shape),
            _const_spec((D_MODEL, D_MODEL)),
            _const_spec((D_PLE, D_MODEL)),
        ] + cast_in,
        out_specs=[pl.BlockSpec((tile, D_MODEL), row)] + cast_out,
        out_shape=[jax.ShapeDtypeStruct((n, D_MODEL), F32)] + cast_shapes,
        compiler_params=pltpu.CompilerParams(
            dimension_semantics=("arbitrary",), vmem_limit_bytes=VMEM_LIMIT),
        name="ffn",
    )(x, p, gpre, w1, w2, gpost, gple, wg, wp, *[j[0] for j in cast_jobs])


N_PROMPT_IN = 13


def _mix_prompt_body(layer, n_cast, *refs):
    (sink_ref, x_ref, gpre_ref, win_ref, convw_ref, bias_ref, dmask_ref, qdec_ref, kdec_ref, cdec_ref,
     wbr_ref, wout_ref, gpost_ref) = refs[:N_PROMPT_IN]
    o_ref, ret_ref, wk_ref, wv_ref, conv_ref = refs[N_PROMPT_IN + n_cast:N_PROMPT_IN + n_cast + 5]
    br_scr, att_scr, u_scr = refs[N_PROMPT_IN + 2 * n_cast + 5:]
    _cast_slabs(refs[N_PROMPT_IN:N_PROMPT_IN + n_cast],
                refs[N_PROMPT_IN + n_cast + 5:N_PROMPT_IN + 2 * n_cast + 5])
    tt = x_ref.shape[1]
    nblk = tt // BLK
    first = pl.program_id(1) == 0

    @pl.when(first)
    def _init():
        ret_ref[...] = jnp.zeros_like(ret_ref)
        wk_ref[...] = jnp.zeros_like(wk_ref)
        wv_ref[...] = jnp.zeros_like(wv_ref)
        u_scr[0:8, :] = jnp.zeros((8, CONV_DIM), F32)

    lane256 = lax.broadcasted_iota(jnp.int32, (1, 256), 1)
    zero_bf = jnp.zeros((), BF16)

    rows = [slice(i * tt // ROW_SPLIT, (i + 1) * tt // ROW_SPLIT) for i in range(ROW_SPLIT)]
    h_parts = [_rms(x_ref[0, rw, :], _gain_row(gpre_ref, layer)).astype(BF16) for rw in rows]
    zr = jnp.concatenate([_dot(hp, win_ref[:, OFF_RET:OFF_RET + 1536]) for hp in h_parts], axis=0)
    h = jnp.concatenate(h_parts, axis=0)
    za = _dot(h, win_ref[:, OFF_ATT:OFF_ATT + 768])
    low = lax.broadcasted_iota(jnp.int32, (1, 128), 1) < ATT_DH
    key_row = lax.broadcasted_iota(jnp.int32, (2 * BLK, 1), 0)
    no_prev = jnp.where(key_row < BLK, jnp.where(first, NEG, 0.0), 0.0).astype(F32)
    sink_rows = [jnp.where(lane256 < BLK, sink_ref[layer, ATT_GROUP * j + par] * LOG2E,
                           sink_ref[layer, ATT_GROUP * j + par + 2] * LOG2E)
                 for j in range(ATT_KV_HEADS) for par in range(2)]
    top = lax.broadcasted_iota(jnp.int32, (KV_W, 1), 0) < ATT_DH

    def value_variants(vblk):
        vt = vblk.T
        return [jnp.where(top, vt, 1.0).astype(BF16), jnp.where(top, 1.0, vt).astype(BF16)]

    def key_variants(kblk):
        rolled = pltpu.roll(kblk, ATT_DH, 1)
        return [jnp.where(low, kblk, 0.0).astype(BF16), jnp.where(low, 0.0, rolled).astype(BF16),
                jnp.where(low, rolled, 0.0).astype(BF16), jnp.where(low, 0.0, kblk).astype(BF16)]

    blocks = [slice(i * BLK, (i + 1) * BLK) for i in range(nblk)]
    att_q = [(za[r, 0:512] * (ATT_DH ** -0.5 * LOG2E)).astype(BF16) for r in blocks]
    att_k = [key_variants(wk_ref[0])] + [key_variants(za[r, 512:640]) for r in blocks]
    att_vt = [value_variants(wv_ref[0])] + [value_variants(za[r, 640:768]) for r in blocks]

    def gate_slice(n, c, width):
        c0 = OFF_GATE + n * D_MODEL + c * width
        return jax.nn.sigmoid(_dot(h, win_ref[:, c0:c0 + width]))

    z_ch = _dot(h, win_ref[:, OFF_CONV + CONV_DIM:OFF_CONV + 3 * CONV_DIM])
    u_scr[8:8 + tt, :] = z_ch[:, 0:CONV_DIM] * z_ch[:, CONV_DIM:2 * CONV_DIM]
    taps = (u_scr[6:6 + tt, :] * convw_ref[0:1, :] + u_scr[7:7 + tt, :] * convw_ref[1:2, :]
            + u_scr[8:8 + tt, :] * convw_ref[2:3, :])
    z_b = _dot(h, win_ref[:, OFF_CONV:OFF_CONV + CONV_DIM])

    def conv_rows(c):
        r = slice(c * BLK, (c + 1) * BLK)
        return (z_b[r] * taps[r]).astype(BF16)

    head_masks = [(lane256 // RET_DK) == hd for hd in range(RET_HEADS)]

    def head_stack(a):
        return jnp.concatenate([jnp.where(hm, a, zero_bf) for hm in head_masks], axis=0)

    def values(c, hd):
        return zr[c * BLK:(c + 1) * BLK, 512 + hd * RET_DV:512 + (hd + 1) * RET_DV].astype(BF16)

    kts, incs = [], []
    for c in range(nblk):
        kt = zr[c * BLK:(c + 1) * BLK, 256:512].T
        kdt = (kt * kdec_ref[...]).astype(BF16)
        kts.append(kt.astype(BF16))
        incs.append(jnp.concatenate(
            [_dot(kdt[hd * RET_DK:(hd + 1) * RET_DK], values(c, hd)) for hd in range(RET_HEADS)], axis=0))

    def scores_and_cross(c, state):
        q_st = head_stack(zr[c * BLK:(c + 1) * BLK, 0:256].astype(BF16))
        return _dot(q_st, jnp.concatenate([kts[c], state.astype(BF16)], axis=1))

    gate_w = D_MODEL // nblk
    gate0, gate2, conv_out = [], [], []
    state = ret_ref[0]
    fused_next = scores_and_cross(0, state)
    for c in range(nblk):
        r = slice(c * BLK, (c + 1) * BLK)
        fused = fused_next
        state = state * cdec_ref[...] + incs[c]
        if c + 1 < nblk:
            fused_next = scores_and_cross(c + 1, state)
        gate0.append(gate_slice(0, c, gate_w))
        gate2.append(gate_slice(2, c, gate_w))
        conv_out.append(conv_rows(c))
        inner = (fused[:, 0:BLK] * dmask_ref[...]).astype(BF16)
        cross = fused[:, BLK:2 * BLK] * qdec_ref[...]
        for hd in range(RET_HEADS):
            hq = slice(hd * BLK, (hd + 1) * BLK)
            o = _dot(inner[hq], values(c, hd)) + cross[hq]
            o = o * lax.rsqrt(jnp.mean(o * o, axis=-1, keepdims=True) + EPS)
            g = zr[r, 1024 + hd * RET_DV:1024 + (hd + 1) * RET_DV]
            br_scr[r, hd * RET_DV:(hd + 1) * RET_DV] = (g * jax.nn.sigmoid(g) * o).astype(BF16)
    ret_ref[0] = state
    tail = u_scr[6 + tt:8 + tt, :]
    conv_ref[0] = tail
    u_scr[6:8, :] = tail
    mixed = (jnp.concatenate(gate0, axis=1) * _dot(br_scr[...], wbr_ref[0])
             + jnp.concatenate(gate2, axis=1) * _dot(jnp.concatenate(conv_out, axis=0), wbr_ref[2]))


    def scores(i):
        aq = att_q[i]
        out = []
        for j in range(ATT_KV_HEADS):
            wq = jnp.concatenate([aq[:, j * 256:j * 256 + 128], aq[:, j * 256 + 128:j * 256 + 256]], axis=0)
            for par in range(2):
                v = 2 * j + par
                out.append(_dot_nt(jnp.concatenate([att_k[i][v], att_k[i + 1][v]], axis=0), wq))
        return out

    gate1 = []
    s_next = scores(0)
    for i in range(nblk):
        r = blocks[i]
        s_cur = s_next
        if i + 1 < nblk:
            s_next = scores(i + 1)
        gate1.append(gate_slice(1, i, gate_w))
        ps, sink_ps = [], []
        for v in range(2 * ATT_KV_HEADS):
            s = s_cur[v] + bias_ref[v]
            if i == 0:
                s = s + no_prev
            m = jnp.maximum(jnp.max(s, axis=0, keepdims=True), sink_rows[v])
            ps.append(jnp.exp2(s - m).astype(BF16))
            sink_ps.append(jnp.exp2(sink_rows[v] - m))
        heads = [None] * ATT_HEADS
        for j in range(ATT_KV_HEADS):
            vt2 = jnp.concatenate([att_vt[i][j], att_vt[i + 1][j]], axis=1)
            ones_row = (1 - j) * ATT_DH
            for par in range(2):
                v = 2 * j + par
                pv = _dot(vt2, ps[v])
                rden = 1.0 / (pv[ones_row:ones_row + 1] + sink_ps[v])
                ot = pv[j * ATT_DH:(j + 1) * ATT_DH] * rden
                heads[ATT_GROUP * j + par] = ot[:, 0:BLK]
                heads[ATT_GROUP * j + par + 2] = ot[:, BLK:2 * BLK]
        att_scr[r, :] = jnp.concatenate(heads, axis=0).T.astype(BF16)
    wk_ref[0] = za[tt - BLK:tt, 512:640]
    wv_ref[0] = za[tt - BLK:tt, 640:768]
    mixed = mixed + jnp.concatenate(gate1, axis=1) * _dot(att_scr[...], wbr_ref[1])

    mixed = mixed.astype(BF16)
    gpost = _gain_row(gpost_ref, layer)
    for rw in rows:
        o_ref[0, rw, :] = x_ref[0, rw, :] + _rms(_dot(mixed[rw], wout_ref[...]), gpost)


def _mix_prompt_call(layer, x, sinks, gpre, win, convw, wbr, wout, gpost, cast_jobs=()):
    b, t, _ = x.shape
    tt = min(PROMPT_TILE, t)
    nt = t // tt
    cast_in, cast_out, cast_shapes = _cast_specs(cast_jobs, b * nt, lambda bi, ti: bi * nt + ti)
    dmask, qdec, kdec, cdec = _retention_consts(BLK)
    dmask = jnp.asarray(dmask.reshape(RET_HEADS * BLK, BLK), F32)
    qdec = jnp.asarray(np.broadcast_to(qdec[:, ::RET_DK].T.reshape(RET_HEADS * BLK, 1), (RET_HEADS * BLK, BLK)), F32)
    kdec = jnp.asarray(kdec.T, F32)
    cdec = jnp.asarray(cdec, F32)
    bias = jnp.asarray(_prompt_attn_bias(), F32)
    per_b = lambda bi, ti: (bi, 0, 0)
    return pl.pallas_call(
        functools.partial(_mix_prompt_body, layer, len(cast_jobs)),
        grid=(b, nt),
        in_specs=[
            pl.BlockSpec(memory_space=pltpu.SMEM),
            pl.BlockSpec((1, tt, D_MODEL), lambda bi, ti: (bi, ti, 0)),
            _const_spec(gpre.shape),
            _const_spec((D_MODEL, N_IN)),
            _layer_spec(layer, (CONV_W, CONV_DIM)),
            _const_spec(bias.shape),
            _const_spec(dmask.shape),
            _const_spec(qdec.shape),
            _const_spec(kdec.shape),
            _const_spec(cdec.shape),
            _const_spec((N_BRANCH, BRANCH_W, D_MODEL)),
            _const_spec((D_MODEL, D_MODEL)),
            _const_spec(gpost.shape),
        ] + cast_in,
        out_specs=[
            pl.BlockSpec((1, tt, D_MODEL), lambda bi, ti: (bi, ti, 0)),
            pl.BlockSpec((1, RET_W, RET_DV), per_b),
            pl.BlockSpec((1, WINDOW, KV_W), per_b),
            pl.BlockSpec((1, WINDOW, KV_W), per_b),
            pl.BlockSpec((1, CONV_W - 1, CONV_DIM), per_b),
        ] + cast_out,
        out_shape=[
            jax.ShapeDtypeStruct((b, t, D_MODEL), F32),
            jax.ShapeDtypeStruct((b, RET_W, RET_DV), F32),
            jax.ShapeDtypeStruct((b, WINDOW, KV_W), F32),
            jax.ShapeDtypeStruct((b, WINDOW, KV_W), F32),
            jax.ShapeDtypeStruct((b, CONV_W - 1, CONV_DIM), F32),
        ] + cast_shapes,
        scratch_shapes=[
            pltpu.VMEM((tt, BRANCH_W), BF16),
            pltpu.VMEM((tt, BRANCH_W), BF16),
            pltpu.VMEM((tt + 8, CONV_DIM), F32),
        ],
        compiler_params=pltpu.CompilerParams(
            dimension_semantics=("arbitrary", "arbitrary"), vmem_limit_bytes=VMEM_LIMIT),
        name="mix_prompt",
    )(sinks, x, gpre, win, convw, bias, dmask, qdec, kdec, cdec, wbr, wout, gpost, *[j[0] for j in cast_jobs])


N_SAMPLE_IN = 17
N_SAMPLE_STATE = 4


def _mix_sample_body(layer, n_alias, *refs):
    (x_ref, sret_ref, kt_ref, vt_ref, sconv_ref, gpre_ref, win_ref, convw_ref, bias_ref, sinkrow_ref,
     dmask_ref, qdec_ref, kdec_ref, cdec_ref, wbr_ref, wout_ref, gpost_ref) = refs[:N_SAMPLE_IN]
    o_ref, ret_ref, wk_ref, wv_ref, conv_ref = refs[N_SAMPLE_IN + n_alias:N_SAMPLE_IN + n_alias + 5]
    z_scr, oc_scr, att_scr, cv_scr, br_scr, u_scr = refs[N_SAMPLE_IN + n_alias + 5:]
    rows = x_ref.shape[0]
    nseq = sret_ref.shape[0]
    slabs = range(ret_ref.shape[0])
    t = rows // nseq

    x = x_ref[...]
    h = _rms(x, _gain_row(gpre_ref, layer)).astype(BF16)
    lane256 = lax.broadcasted_iota(jnp.int32, (1, 256), 1)
    lane128 = lax.broadcasted_iota(jnp.int32, (1, 128), 1)
    zero_bf = jnp.zeros((), BF16)
    head_masks = [(lane256 // RET_DK) == hd for hd in range(RET_HEADS)]
    new_lanes = lane128 >= BLK - t
    zpad = jnp.zeros((BLK - t, KV_W), F32)
    snk = sinkrow_ref[:, 0:1]

    z_scr[...] = _dot(h, win_ref[:, 0:OFF_GATE])

    def new_rows_t(rs, c0):
        return jnp.concatenate([zpad, z_scr[rs, c0:c0 + KV_W]], axis=0).T

    ngroups = -(-nseq // SEQ_GROUP)
    gate_w = N_BRANCH * D_MODEL // ngroups
    gate_parts = []
    for g0 in range(0, nseq, SEQ_GROUP):
        seqs = list(range(g0, min(g0 + SEQ_GROUP, nseq)))
        rsl = {b: slice(b * t, (b + 1) * t) for b in seqs}

        for b in seqs:
            rs = rsl[b]
            qd = z_scr[rs, 0:256] * qdec_ref[...]
            qm = jnp.concatenate([jnp.where(hm, qd, 0.0) for hm in head_masks], axis=0).astype(BF16)
            s_old = sret_ref[b]
            oc = _dot(qm, s_old.astype(BF16))
            for hd in range(RET_HEADS):
                oc_scr[hd, rs, :] = oc[hd * t:(hd + 1) * t]
            kdt = (z_scr[rs, 256:512] * kdec_ref[...]).T.astype(BF16)
            vb = z_scr[rs, 512:1024].astype(BF16)
            for hd in range(RET_HEADS):
                hr = slice(hd * RET_DK, (hd + 1) * RET_DK)
                s_new = s_old[hr] * cdec_ref[hr, :] + _dot(kdt[hr], vb[:, hd * RET_DV:(hd + 1) * RET_DV])
                for l in slabs:
                    ret_ref[l, b, hr, :] = s_new

        s_raw, nkt = {}, {}
        for b in seqs:
            rs = rsl[b]
            aq = z_scr[rs, 1536:2048] * (ATT_DH ** -0.5)
            pieces = []
            for hd in range(ATT_HEADS):
                piece = aq[:, (hd // 2) * 128:(hd // 2 + 1) * 128]
                if hd % 2 != hd // ATT_GROUP:
                    piece = pltpu.roll(piece, ATT_DH, 1)
                pieces.append(jnp.where((lane128 // ATT_DH) == hd // ATT_GROUP, piece, 0.0))
            qs = jnp.concatenate(pieces, axis=0).astype(BF16)
            nkt[b] = new_rows_t(rs, 2048)
            k2t = jnp.concatenate([kt_ref[b], nkt[b]], axis=1).astype(BF16)
            s_raw[b] = _dot(qs, k2t)

        c0 = OFF_GATE + (g0 // SEQ_GROUP) * gate_w
        gate_parts.append(jax.nn.sigmoid(_dot(h, win_ref[:, c0:c0 + gate_w])))

        probs = {}
        for b in seqs:
            s = s_raw[b] + bias_ref[...]
            m = jnp.maximum(jnp.max(s, axis=-1, keepdims=True), snk)
            p = jnp.exp(s - m)
            rden = 1.0 / (jnp.sum(p, axis=-1, keepdims=True) + jnp.exp(snk - m))
            probs[b] = (p * rden).astype(BF16)

        for b in seqs:
            rs = rsl[b]
            nvt = new_rows_t(rs, 2176)
            v2t = jnp.concatenate([vt_ref[b], nvt], axis=1).astype(BF16)
            o = _dot_nt(probs[b], v2t)
            for pr in range(ATT_HEADS // 2):
                lo = o[(2 * pr) * t:(2 * pr + 1) * t]
                hi = o[(2 * pr + 1) * t:(2 * pr + 2) * t]
                if (2 * pr) // ATT_GROUP == 1:
                    lo = pltpu.roll(lo, ATT_DH, 1)
                if (2 * pr + 1) // ATT_GROUP == 0:
                    hi = pltpu.roll(hi, ATT_DH, 1)
                att_scr[rs, pr * 128:(pr + 1) * 128] = jnp.where(lane128 < ATT_DH, lo, hi)
            k_new = jnp.where(new_lanes, nkt[b], pltpu.roll(kt_ref[b], BLK - t, 1))
            v_new = jnp.where(new_lanes, nvt, pltpu.roll(vt_ref[b], BLK - t, 1))
            for l in slabs:
                wk_ref[l, b] = k_new
                wv_ref[l, b] = v_new

        for b in seqs:
            rs = rsl[b]
            u = z_scr[rs, 2816:3328] * z_scr[rs, 3328:3840]
            u_scr[b, 6:8, :] = sconv_ref[b]
            u_scr[b, 8:8 + t, :] = u
            y = (u_scr[b, 6:6 + t, :] * convw_ref[0:1, :] + u_scr[b, 7:7 + t, :] * convw_ref[1:2, :]
                 + u * convw_ref[2:3, :])
            cv_scr[rs, :] = z_scr[rs, 2304:2816] * y
            for l in slabs:
                conv_ref[l, b] = u_scr[b, 6 + t:8 + t, :]

    qb = z_scr[:, 0:256].astype(BF16)
    kb = z_scr[:, 256:512].astype(BF16)
    for hd in range(RET_HEADS):
        inner = _dot_nt(jnp.where(head_masks[hd], qb, zero_bf), kb) * dmask_ref[hd]
        vh = z_scr[:, 512 + hd * RET_DV:512 + (hd + 1) * RET_DV].astype(BF16)
        o = _dot(inner.astype(BF16), vh) + oc_scr[hd]
        o = o * lax.rsqrt(jnp.mean(o * o, axis=-1, keepdims=True) + EPS)
        g = z_scr[:, 1024 + hd * RET_DV:1024 + (hd + 1) * RET_DV]
        br_scr[:, hd * RET_DV:(hd + 1) * RET_DV] = (g * jax.nn.sigmoid(g) * o).astype(BF16)
    gates = jnp.concatenate(gate_parts, axis=1)
    mixed = (gates[:, 0:D_MODEL] * _dot(br_scr[...], wbr_ref[0])
             + gates[:, D_MODEL:2 * D_MODEL] * _dot(att_scr[...].astype(BF16), wbr_ref[1])
             + gates[:, 2 * D_MODEL:3 * D_MODEL] * _dot(cv_scr[...].astype(BF16), wbr_ref[2]))
    o_ref[...] = x + _rms(_dot(mixed.astype(BF16), wout_ref[...]), _gain_row(gpost_ref, layer))


def _mix_sample_call(layer, x2, t, sinkrow, sret, kt, vt, sconv, gpre, win, convw, wbr, wout, gpost, prev):
    depth, nb = sret.shape[0], sret.shape[1]
    sb = min(SAMPLE_SEQS, nb)
    rows = sb * t
    dm, qd, kd, cdec = _retention_consts(t)
    same_seq = np.kron(np.eye(sb), np.ones((t, t)))
    dmask = jnp.asarray(np.tile(dm, (1, sb, sb)) * same_seq[None], F32)
    qdec = jnp.asarray(qd, F32)
    kdec = jnp.asarray(kd, F32)
    cdec = jnp.asarray(cdec, F32)
    bias = jnp.asarray(_sample_attn_bias(t), F32)
    row = lambda i: (i, 0)
    seq = lambda i: (layer, i, 0, 0)
    state_shapes = [(RET_W, RET_DV), (KV_W, WINDOW), (KV_W, WINDOW), (CONV_W - 1, CONV_DIM)]
    state_specs = [pl.BlockSpec((None, sb) + s, seq) for s in state_shapes]
    if prev is None:
        out_state_specs = [pl.BlockSpec((depth, sb) + s, lambda i: (0, i, 0, 0)) for s in state_shapes]
    else:
        out_state_specs = [pl.BlockSpec((1, sb) + s, seq) for s in state_shapes]
    n_alias = 0 if prev is None else N_SAMPLE_STATE
    alias_args = [] if prev is None else list(prev)
    return pl.pallas_call(
        functools.partial(_mix_sample_body, layer, n_alias),
        grid=(nb // sb,),
        in_specs=[pl.BlockSpec((rows, D_MODEL), row)] + state_specs + [
            _const_spec(gpre.shape),
            _const_spec((D_MODEL, N_IN)),
            _layer_spec(layer, (CONV_W, CONV_DIM)),
            _const_spec(bias.shape),
            _layer_spec(layer, (ATT_HEADS * t, 128)),
            _const_spec(dmask.shape),
            _const_spec(qdec.shape),
            _const_spec(kdec.shape),
            _const_spec(cdec.shape),
            _const_spec((N_BRANCH, BRANCH_W, D_MODEL)),
            _const_spec((D_MODEL, D_MODEL)),
            _const_spec(gpost.shape),
        ] + [pl.BlockSpec(memory_space=pl.ANY)] * n_alias,
        out_specs=[pl.BlockSpec((rows, D_MODEL), row)] + out_state_specs,
        out_shape=[jax.ShapeDtypeStruct((nb * t, D_MODEL), F32)]
        + [jax.ShapeDtypeStruct((depth, nb) + s, F32) for s in state_shapes],
        input_output_aliases={N_SAMPLE_IN + k: 1 + k for k in range(n_alias)},
        scratch_shapes=[
            pltpu.VMEM((rows, OFF_GATE), F32),
            pltpu.VMEM((RET_HEADS, rows, RET_DV), F32),
            pltpu.VMEM((rows, BRANCH_W), F32),
            pltpu.VMEM((rows, BRANCH_W), F32),
            pltpu.VMEM((rows, BRANCH_W), BF16),
            pltpu.VMEM((sb, 8 + t, CONV_DIM), F32),
        ],
        compiler_params=pltpu.CompilerParams(
            dimension_semantics=("arbitrary",), vmem_limit_bytes=VMEM_LIMIT),
        name="mix_sample",
    )(x2, sret, kt, vt, sconv, gpre, win, convw, bias, sinkrow, dmask, qdec, kdec, cdec, wbr, wout, gpost,
      *alias_args)


def kernel(x_prompt, x_sample, p_prompt, p_sample, state_ret, cache_win_k, cache_win_v, state_conv,
           g_mix_pre, w_in, conv_w, attn_sinks, w_branch, w_out, g_mix_post,
           g_ffn_pre, w_ff1, w_ff2, g_ffn_post, g_ple, w_ple_gate, w_ple_proj):
    depth = w_in.shape[0]
    bp, tp, d = x_prompt.shape
    bs, ts, _ = x_sample.shape
    win_w = cache_win_k.shape[2]
    assert d == D_MODEL and w_in.shape[2] == N_IN and tp % BLK == 0 and math.gcd(ts, BLK) == ts
    assert win_w == WINDOW == BLK and ts <= BLK

    gpre, gpost = g_mix_pre, g_mix_post
    gfpre, gfpost, gple = g_ffn_pre, g_ffn_post, g_ple
    rows2d = lambda a: a.reshape(-1, a.shape[-1])
    mix_w = (w_in[0].astype(BF16), w_branch[0].astype(BF16), w_out[0].astype(BF16))
    wp_all = w_ple_proj.astype(BF16)
    pp = p_prompt.reshape(depth, bp * tp, D_PLE)
    ps = p_sample.reshape(depth, bs * ts, D_PLE)
    sret = state_ret.reshape(depth, bs, RET_W, RET_DV)
    kt = jnp.transpose(cache_win_k, (0, 1, 3, 4, 2)).reshape(depth, bs, KV_W, win_w)
    vt = jnp.transpose(cache_win_v, (0, 1, 3, 4, 2)).reshape(depth, bs, KV_W, win_w)
    sinkrow = jnp.broadcast_to(jnp.repeat(attn_sinks, ts, axis=1)[:, :, None], (depth, ATT_HEADS * ts, 128))

    yp = x_prompt
    ys = x_sample.reshape(bs * ts, d)
    outs_p, sample_states = [], None
    for l in range(depth):
        win, wbr, wout = mix_w
        ffn_jobs = [(rows2d(w_ff1), l, D_MODEL), (rows2d(w_ff2), l, D_FF), (rows2d(w_ple_gate), l, D_MODEL)]
        yp, r, k, v, c, w1, w2, wg = _mix_prompt_call(l, yp, attn_sinks, gpre, win, conv_w, wbr, wout, gpost,
                                                      ffn_jobs)
        outs_p.append((r, k, v, c))
        ffn_w = (gfpre, w1, w2, gfpost, gple, wg, wp_all[l])
        next_jobs = [] if l + 1 == depth else [(rows2d(w_in), l + 1, D_MODEL),
                                               (rows2d(w_branch), l + 1, N_BRANCH * BRANCH_W),
                                               (rows2d(w_out), l + 1, D_MODEL)]
        yp, *next_w = _ffn_call(l, yp.reshape(bp * tp, d), pp, *ffn_w, next_jobs)
        yp = yp.reshape(bp, tp, d)

        ys, *sample_states = _mix_sample_call(l, ys, ts, sinkrow, sret, kt, vt, state_conv, gpre, win, conv_w,
                                              wbr, wout, gpost, sample_states)
        ys, = _ffn_call(l, ys, ps, *ffn_w)
        if next_w:
            mix_w = (next_w[0], next_w[1].reshape(N_BRANCH, BRANCH_W, D_MODEL), next_w[2])

    def stack(i, shape):
        return jnp.stack([o[i] for o in outs_p]).reshape(shape)

    def untranspose(a):
        return jnp.transpose(a.reshape(depth, bs, ATT_KV_HEADS, ATT_DH, win_w), (0, 1, 4, 2, 3))

    rs, kts, vts, cs = sample_states
    return (yp, ys.reshape(bs, ts, d),
            stack(0, (depth, bp, RET_HEADS, RET_DK, RET_DV)),
            stack(1, (depth, bp, WINDOW, ATT_KV_HEADS, ATT_DH)),
            stack(2, (depth, bp, WINDOW, ATT_KV_HEADS, ATT_DH)),
            stack(3, (depth, bp, CONV_W - 1, CONV_DIM)),
            rs.reshape(depth, bs, RET_HEADS, RET_DK, RET_DV), untranspose(kts), untranspose(vts), cs)
```

```python
import functools
import math

import numpy as np
import jax
import jax.numpy as jnp
from jax import lax
from jax.experimental import pallas as pl
from jax.experimental.pallas import tpu as pltpu

F32 = jnp.float32
BF16 = jnp.bfloat16

D_MODEL = 1024
BRANCH_W = 512
N_BRANCH = 3
RET_HEADS = 4
RET_DK = 64
RET_DV = 128
RET_W = RET_HEADS * RET_DK
ATT_HEADS = 8
ATT_KV_HEADS = 2
ATT_GROUP = ATT_HEADS // ATT_KV_HEADS
ATT_DH = 64
KV_W = ATT_KV_HEADS * ATT_DH
WINDOW = 128
CONV_DIM = 512
CONV_W = 3
D_FF = 4096
D_PLE = 256
EPS = 1e-6

OFF_RET = 0
OFF_ATT = 1536
OFF_CONV = 2304
OFF_GATE = 3840
N_IN = 6912

BLK = 128
BF16_ROWS = 16
LOG2E = math.log2(math.e)
NEG = -1e30
VMEM_LIMIT = 56 * 1024 * 1024

PROMPT_TILE = 512
ROW_SPLIT = 4
FFN_TILE = 1024
FFN_ROW_GROUP = 128
FFN_CHUNK = 1024
SAMPLE_SEQS = 16
SEQ_GROUP = 16


def _dot(a, b):
    return jnp.dot(a, b, preferred_element_type=F32)


def _dot_nt(a, b):
    return lax.dot_general(a, b, (((1,), (1,)), ((), ())), preferred_element_type=F32)


def _rms(x, g):
    return x * lax.rsqrt(jnp.mean(x * x, axis=-1, keepdims=True) + EPS) * g


def _gain_row(g_ref, layer):
    return g_ref[layer:layer + 1, :]


def _const_spec(shape):
    nd = len(shape)
    return pl.BlockSpec(shape, lambda *_: (0,) * nd, pipeline_mode=pl.Buffered(1))


def _layer_spec(layer, shape):
    nd = len(shape)
    return pl.BlockSpec((None,) + tuple(shape), lambda *_: (layer,) + (0,) * nd, pipeline_mode=pl.Buffered(1))


def _cast_specs(jobs, nsteps, step_of):
    in_specs, out_specs, out_shapes = [], [], []
    for src, layer, nrows in jobs:
        slab = nrows // nsteps
        assert nrows % nsteps == 0 and slab % BF16_ROWS == 0, (nrows, nsteps)
        cols = src.shape[1]
        in_specs.append(pl.BlockSpec((slab, cols), lambda *g, base=layer * nsteps: (base + step_of(*g), 0)))
        out_specs.append(pl.BlockSpec((slab, cols), lambda *g: (step_of(*g), 0)))
        out_shapes.append(jax.ShapeDtypeStruct((nrows, cols), BF16))
    return in_specs, out_specs, out_shapes


def _cast_slabs(src_refs, dst_refs):
    for src, dst in zip(src_refs, dst_refs):
        dst[...] = src[...].astype(BF16)


def _log_gamma():
    return np.log1p(-np.exp2(-5.0 - np.arange(RET_HEADS, dtype=np.float64)))


def _retention_consts(c):
    lg = _log_gamma()
    idx = np.arange(c, dtype=np.float64)
    diff = idx[:, None] - idx[None, :]
    dmask = np.where(diff >= 0, np.exp(lg[:, None, None] * np.maximum(diff, 0.0)), 0.0) * RET_DK ** -0.5
    qdec = np.repeat(np.exp(lg[:, None] * (idx + 1.0)).T, RET_DK, axis=1)
    kdec = np.repeat(np.exp(lg[:, None] * (c - 1.0 - idx)).T, RET_DK, axis=1) * RET_DK ** -0.5
    cdec = np.broadcast_to(np.repeat(np.exp(lg * c), RET_DK)[:, None], (RET_W, RET_DV))
    return dmask, qdec, kdec, cdec


def _slopes():
    return np.exp2(-8.0 * (np.arange(ATT_HEADS, dtype=np.float64) + 1.0) / ATT_HEADS)


def _prompt_attn_bias():
    s = np.arange(2 * BLK)[:, None]
    q = np.arange(BLK)[None, :]
    dist = q + BLK - s
    allowed = (dist >= 0) & (dist < WINDOW)
    per_head = np.where(allowed[None], -_slopes()[:, None, None] * dist[None] * LOG2E, NEG)
    out = [np.concatenate([per_head[ATT_GROUP * j + par], per_head[ATT_GROUP * j + par + 2]], axis=1)
           for j in range(ATT_KV_HEADS) for par in range(2)]
    return np.stack(out)


def _sample_attn_bias(t):
    i = np.arange(t)[:, None]
    col = np.arange(2 * BLK)[None, :]
    new = col >= 2 * BLK - t
    dist = np.where(new, i - (col - (2 * BLK - t)), i + WINDOW - col)
    allowed = (dist >= 0) & (dist < WINDOW) & ((col < WINDOW) | new)
    b = np.where(allowed[None], -_slopes()[:, None, None] * dist[None], NEG)
    return b.reshape(ATT_HEADS * t, 2 * BLK)


N_FFN_IN = 9


def _ffn_body(layer, n_cast, *refs):
    x_ref, p_ref, gpre_ref, w1_ref, w2_ref, gpost_ref, gple_ref, wg_ref, wp_ref = refs[:N_FFN_IN]
    o_ref = refs[N_FFN_IN + n_cast]
    _cast_slabs(refs[N_FFN_IN:N_FFN_IN + n_cast], refs[N_FFN_IN + n_cast + 1:])
    tile = x_ref.shape[0]
    nchunk = D_FF // FFN_CHUNK
    rows = [slice(r0, r0 + FFN_ROW_GROUP) for r0 in range(0, tile, FFN_ROW_GROUP)]

    def up(hh, c):
        a = _dot(hh, w1_ref[:, c * FFN_CHUNK:(c + 1) * FFN_CHUNK])
        return jnp.square(jnp.maximum(a, 0.0)).astype(BF16)

    def down(a, c):
        return _dot(a, w2_ref[c * FFN_CHUNK:(c + 1) * FFN_CHUNK, :])

    gpre, gpost, gple = (_gain_row(g, layer) for g in (gpre_ref, gpost_ref, gple_ref))
    h_parts = [_rms(x_ref[rw, :], gpre).astype(BF16) for rw in rows]
    a0 = jnp.concatenate([up(hp, 0) for hp in h_parts], axis=0)
    h = jnp.concatenate(h_parts, axis=0)
    acc = down(a0, 0)
    for c in range(1, nchunk - 1):
        acc = acc + down(up(h, c), c)
    a_last = up(h, nchunk - 1)
    f_parts = [acc[rw] + down(a_last[rw], nchunk - 1) for rw in rows]
    pp = _dot(p_ref[...].astype(BF16), wp_ref[...])
    for rw, f in zip(rows, f_parts):
        x2 = x_ref[rw, :] + _rms(f, gpost)
        gate = jax.nn.sigmoid(_dot(_rms(x2, gple).astype(BF16), wg_ref[...]))
        o_ref[rw, :] = x2 + gate * pp[rw]


def _ffn_call(layer, x, p, gpre, w1, w2, gpost, gple, wg, wp, cast_jobs=()):
    n = x.shape[0]
    tile = min(FFN_TILE, n)
    row = lambda i: (i, 0)
    cast_in, cast_out, cast_shapes = _cast_specs(cast_jobs, n // tile, lambda i: i)
    return pl.pallas_call(
        functools.partial(_ffn_body, layer, len(cast_jobs)),
        grid=(n // tile,),
        in_specs=[
            pl.BlockSpec((tile, D_MODEL), row),
            pl.BlockSpec((None, tile, D_PLE), lambda i: (layer, i, 0)),
            _const_spec(gpre.shape),
            _const_spec((D_MODEL, D_FF)),
            _const_spec((D_FF, D_MODEL)),
            _const_spec(gpost.shape),
            _const_spec(gple.shape),
            _const_spec((D_MODEL, D_MODEL)),
            _const_spec((D_PLE, D_MODEL)),
        ] + cast_in,
        out_specs=[pl.BlockSpec((tile, D_MODEL), row)] + cast_out,
        out_shape=[jax.ShapeDtypeStruct((n, D_MODEL), F32)] + cast_shapes,
        compiler_params=pltpu.CompilerParams(
            dimension_semantics=("arbitrary",), vmem_limit_bytes=VMEM_LIMIT),
        name="ffn",
    )(x, p, gpre, w1, w2, gpost, gple, wg, wp, *[j[0] for j in cast_jobs])


N_PROMPT_IN = 13


def _mix_prompt_body(layer, n_cast, *refs):
    (sink_ref, x_ref, gpre_ref, win_ref, convw_ref, bias_ref, dmask_ref, qdec_ref, kdec_ref, cdec_ref,
     wbr_ref, wout_ref, gpost_ref) = refs[:N_PROMPT_IN]
    o_ref, ret_ref, wk_ref, wv_ref, conv_ref = refs[N_PROMPT_IN + n_cast:N_PROMPT_IN + n_cast + 5]
    br_scr, att_scr, u_scr = refs[N_PROMPT_IN + 2 * n_cast + 5:]
    _cast_slabs(refs[N_PROMPT_IN:N_PROMPT_IN + n_cast],
                refs[N_PROMPT_IN + n_cast + 5:N_PROMPT_IN + 2 * n_cast + 5])
    tt = x_ref.shape[1]
    nblk = tt // BLK
    first = pl.program_id(1) == 0

    @pl.when(first)
    def _init():
        ret_ref[...] = jnp.zeros_like(ret_ref)
        wk_ref[...] = jnp.zeros_like(wk_ref)
        wv_ref[...] = jnp.zeros_like(wv_ref)
        u_scr[0:8, :] = jnp.zeros((8, CONV_DIM), F32)

    lane256 = lax.broadcasted_iota(jnp.int32, (1, 256), 1)
    zero_bf = jnp.zeros((), BF16)

    rows = [slice(i * tt // ROW_SPLIT, (i + 1) * tt // ROW_SPLIT) for i in range(ROW_SPLIT)]
    h_parts = [_rms(x_ref[0, rw, :], _gain_row(gpre_ref, layer)).astype(BF16) for rw in rows]
    zr = jnp.concatenate([_dot(hp, win_ref[:, OFF_RET:OFF_RET + 1536]) for hp in h_parts], axis=0)
    h = jnp.concatenate(h_parts, axis=0)
    za = _dot(h, win_ref[:, OFF_ATT:OFF_ATT + 768])

    head_masks = [(lane256 // RET_DK) == hd for hd in range(RET_HEADS)]

    def head_stack(a):
        return jnp.concatenate([jnp.where(hm, a, zero_bf) for hm in head_masks], axis=0)

    def values(c, hd):
        return zr[c * BLK:(c + 1) * BLK, 512 + hd * RET_DV:512 + (hd + 1) * RET_DV].astype(BF16)

    kts, incs = [], []
    for c in range(nblk):
        kt = zr[c * BLK:(c + 1) * BLK, 256:512].T
        kdt = (kt * kdec_ref[...]).astype(BF16)
        kts.append(kt.astype(BF16))
        incs.append(jnp.concatenate(
            [_dot(kdt[hd * RET_DK:(hd + 1) * RET_DK], values(c, hd)) for hd in range(RET_HEADS)], axis=0))

    def scores_and_cross(c, state):
        q_st = head_stack(zr[c * BLK:(c + 1) * BLK, 0:256].astype(BF16))
        return _dot(q_st, jnp.concatenate([kts[c], state.astype(BF16)], axis=1))

    state = ret_ref[0]
    fused_next = scores_and_cross(0, state)

    low = lax.broadcasted_iota(jnp.int32, (1, 128), 1) < ATT_DH
    key_row = lax.broadcasted_iota(jnp.int32, (2 * BLK, 1), 0)
    no_prev = jnp.where(key_row < BLK, jnp.where(first, NEG, 0.0), 0.0).astype(F32)
    sink_rows = [jnp.where(lane256 < BLK, sink_ref[layer, ATT_GROUP * j + par] * LOG2E,
                           sink_ref[layer, ATT_GROUP * j + par + 2] * LOG2E)
                 for j in range(ATT_KV_HEADS) for par in range(2)]
    top = lax.broadcasted_iota(jnp.int32, (KV_W, 1), 0) < ATT_DH

    def value_variants(vblk):
        vt = vblk.T
        return [jnp.where(top, vt, 1.0).astype(BF16), jnp.where(top, 1.0, vt).astype(BF16)]

    def key_variants(kblk):
        rolled = pltpu.roll(kblk, ATT_DH, 1)
        return [jnp.where(low, kblk, 0.0).astype(BF16), jnp.where(low, 0.0, rolled).astype(BF16),
                jnp.where(low, rolled, 0.0).astype(BF16), jnp.where(low, 0.0, kblk).astype(BF16)]

    blocks = [slice(i * BLK, (i + 1) * BLK) for i in range(nblk)]
    att_q = [(za[r, 0:512] * (ATT_DH ** -0.5 * LOG2E)).astype(BF16) for r in blocks]
    att_k = [key_variants(wk_ref[0])] + [key_variants(za[r, 512:640]) for r in blocks]
    att_vt = [value_variants(wv_ref[0])] + [value_variants(za[r, 640:768]) for r in blocks]

    def gate_slice(n, c, width):
        c0 = OFF_GATE + n * D_MODEL + c * width
        return jax.nn.sigmoid(_dot(h, win_ref[:, c0:c0 + width]))

    z_ch = _dot(h, win_ref[:, OFF_CONV + CONV_DIM:OFF_CONV + 3 * CONV_DIM])
    u_scr[8:8 + tt, :] = z_ch[:, 0:CONV_DIM] * z_ch[:, CONV_DIM:2 * CONV_DIM]
    taps = (u_scr[6:6 + tt, :] * convw_ref[0:1, :] + u_scr[7:7 + tt, :] * convw_ref[1:2, :]
            + u_scr[8:8 + tt, :] * convw_ref[2:3, :])
    z_b = _dot(h, win_ref[:, OFF_CONV:OFF_CONV + CONV_DIM])

    def conv_rows(c):
        r = slice(c * BLK, (c + 1) * BLK)
        return (z_b[r] * taps[r]).astype(BF16)

    gate_w = D_MODEL // nblk
    gate0, gate2, conv_out = [], [], []
    for c in range(nblk):
        r = slice(c * BLK, (c + 1) * BLK)
        fused = fused_next
        state = state * cdec_ref[...] + incs[c]
        gate0.append(gate_slice(0, c, gate_w))
        if c + 1 < nblk:
            fused_next = scores_and_cross(c + 1, state)
        gate2.append(gate_slice(2, c, gate_w))
        conv_out.append(conv_rows(c))
        inner = (fused[:, 0:BLK] * dmask_ref[...]).astype(BF16)
        cross = fused[:, BLK:2 * BLK] * qdec_ref[...]
        for hd in range(RET_HEADS):
            hq = slice(hd * BLK, (hd + 1) * BLK)
            o = _dot(inner[hq], values(c, hd)) + cross[hq]
            o = o * lax.rsqrt(jnp.mean(o * o, axis=-1, keepdims=True) + EPS)
            g = zr[r, 1024 + hd * RET_DV:1024 + (hd + 1) * RET_DV]
            br_scr[r, hd * RET_DV:(hd + 1) * RET_DV] = (g * jax.nn.sigmoid(g) * o).astype(BF16)
    ret_ref[0] = state
    tail = u_scr[6 + tt:8 + tt, :]
    conv_ref[0] = tail
    u_scr[6:8, :] = tail
    proj2 = _dot(jnp.concatenate(conv_out, axis=0), wbr_ref[2])
    mixed = (jnp.concatenate(gate0, axis=1) * _dot(br_scr[...], wbr_ref[0])
             + jnp.concatenate(gate2, axis=1) * proj2)


    def scores(i):
        aq = att_q[i]
        out = []
        for j in range(ATT_KV_HEADS):
            wq = jnp.concatenate([aq[:, j * 256:j * 256 + 128], aq[:, j * 256 + 128:j * 256 + 256]], axis=0)
            for par in range(2):
                v = 2 * j + par
                out.append(_dot_nt(jnp.concatenate([att_k[i][v], att_k[i + 1][v]], axis=0), wq))
        return out

    gate1 = []
    s_next = scores(0)
    for i in range(nblk):
        r = blocks[i]
        s_cur = s_next
        if i + 1 < nblk:
            s_next = scores(i + 1)
        gate1.append(gate_slice(1, i, gate_w))
        ps, sink_ps = [], []
        for v in range(2 * ATT_KV_HEADS):
            s = s_cur[v] + bias_ref[v]
            if i == 0:
                s = s + no_prev
            m = jnp.maximum(jnp.max(s, axis=0, keepdims=True), sink_rows[v])
            ps.append(jnp.exp2(s - m).astype(BF16))
            sink_ps.append(jnp.exp2(sink_rows[v] - m))
        heads = [None] * ATT_HEADS
        for j in range(ATT_KV_HEADS):
            vt2 = jnp.concatenate([att_vt[i][j], att_vt[i + 1][j]], axis=1)
            ones_row = (1 - j) * ATT_DH
            for par in range(2):
                v = 2 * j + par
                pv = _dot(vt2, ps[v])
                rden = 1.0 / (pv[ones_row:ones_row + 1] + sink_ps[v])
                ot = pv[j * ATT_DH:(j + 1) * ATT_DH] * rden
                heads[ATT_GROUP * j + par] = ot[:, 0:BLK]
                heads[ATT_GROUP * j + par + 2] = ot[:, BLK:2 * BLK]
        att_scr[r, :] = jnp.concatenate(heads, axis=0).T.astype(BF16)
    wk_ref[0] = za[tt - BLK:tt, 512:640]
    wv_ref[0] = za[tt - BLK:tt, 640:768]
    mixed = mixed + jnp.concatenate(gate1, axis=1) * _dot(att_scr[...], wbr_ref[1])

    mixed = mixed.astype(BF16)
    gpost = _gain_row(gpost_ref, layer)
    for rw in rows:
        o_ref[0, rw, :] = x_ref[0, rw, :] + _rms(_dot(mixed[rw], wout_ref[...]), gpost)


def _mix_prompt_call(layer, x, sinks, gpre, win, convw, wbr, wout, gpost, cast_jobs=()):
    b, t, _ = x.shape
    tt = min(PROMPT_TILE, t)
    nt = t // tt
    cast_in, cast_out, cast_shapes = _cast_specs(cast_jobs, b * nt, lambda bi, ti: bi * nt + ti)
    dmask, qdec, kdec, cdec = _retention_consts(BLK)
    dmask = jnp.asarray(dmask.reshape(RET_HEADS * BLK, BLK), F32)
    qdec = jnp.asarray(np.broadcast_to(qdec[:, ::RET_DK].T.reshape(RET_HEADS * BLK, 1), (RET_HEADS * BLK, BLK)), F32)
    kdec = jnp.asarray(kdec.T, F32)
    cdec = jnp.asarray(cdec, F32)
    bias = jnp.asarray(_prompt_attn_bias(), F32)
    per_b = lambda bi, ti: (bi, 0, 0)
    return pl.pallas_call(
        functools.partial(_mix_prompt_body, layer, len(cast_jobs)),
        grid=(b, nt),
        in_specs=[
            pl.BlockSpec(memory_space=pltpu.SMEM),
            pl.BlockSpec((1, tt, D_MODEL), lambda bi, ti: (bi, ti, 0)),
            _const_spec(gpre.shape),
            _const_spec((D_MODEL, N_IN)),
            _layer_spec(layer, (CONV_W, CONV_DIM)),
            _const_spec(bias.shape),
            _const_spec(dmask.shape),
            _const_spec(qdec.shape),
            _const_spec(kdec.shape),
            _const_spec(cdec.shape),
            _const_spec((N_BRANCH, BRANCH_W, D_MODEL)),
            _const_spec((D_MODEL, D_MODEL)),
            _const_spec(gpost.shape),
        ] + cast_in,
        out_specs=[
            pl.BlockSpec((1, tt, D_MODEL), lambda bi, ti: (bi, ti, 0)),
            pl.BlockSpec((1, RET_W, RET_DV), per_b),
            pl.BlockSpec((1, WINDOW, KV_W), per_b),
            pl.BlockSpec((1, WINDOW, KV_W), per_b),
            pl.BlockSpec((1, CONV_W - 1, CONV_DIM), per_b),
        ] + cast_out,
        out_shape=[
            jax.ShapeDtypeStruct((b, t, D_MODEL), F32),
            jax.ShapeDtypeStruct((b, RET_W, RET_DV), F32),
            jax.ShapeDtypeStruct((b, WINDOW, KV_W), F32),
            jax.ShapeDtypeStruct((b, WINDOW, KV_W), F32),
            jax.ShapeDtypeStruct((b, CONV_W - 1, CONV_DIM), F32),
        ] + cast_shapes,
        scratch_shapes=[
            pltpu.VMEM((tt, BRANCH_W), BF16),
            pltpu.VMEM((tt, BRANCH_W), BF16),
            pltpu.VMEM((tt + 8, CONV_DIM), F32),
        ],
        compiler_params=pltpu.CompilerParams(
            dimension_semantics=("arbitrary", "arbitrary"), vmem_limit_bytes=VMEM_LIMIT),
        name="mix_prompt",
    )(sinks, x, gpre, win, convw, bias, dmask, qdec, kdec, cdec, wbr, wout, gpost, *[j[0] for j in cast_jobs])


N_SAMPLE_IN = 17
N_SAMPLE_STATE = 4


def _mix_sample_body(layer, n_alias, *refs):
    (x_ref, sret_ref, kt_ref, vt_ref, sconv_ref, gpre_ref, win_ref, convw_ref, bias_ref, sinkrow_ref,
     dmask_ref, qdec_ref, kdec_ref, cdec_ref, wbr_ref, wout_ref, gpost_ref) = refs[:N_SAMPLE_IN]
    o_ref, ret_ref, wk_ref, wv_ref, conv_ref = refs[N_SAMPLE_IN + n_alias:N_SAMPLE_IN + n_alias + 5]
    z_scr, oc_scr, att_scr, cv_scr, br_scr, u_scr = refs[N_SAMPLE_IN + n_alias + 5:]
    rows = x_ref.shape[0]
    nseq = sret_ref.shape[0]
    slabs = range(ret_ref.shape[0])
    t = rows // nseq

    x = x_ref[...]
    h = _rms(x, _gain_row(gpre_ref, layer)).astype(BF16)
    lane256 = lax.broadcasted_iota(jnp.int32, (1, 256), 1)
    lane128 = lax.broadcasted_iota(jnp.int32, (1, 128), 1)
    zero_bf = jnp.zeros((), BF16)
    head_masks = [(lane256 // RET_DK) == hd for hd in range(RET_HEADS)]
    new_lanes = lane128 >= BLK - t
    zpad = jnp.zeros((BLK - t, KV_W), F32)
    snk = sinkrow_ref[:, 0:1]

    z_scr[...] = _dot(h, win_ref[:, 0:OFF_GATE])

    def new_rows_t(rs, c0):
        return jnp.concatenate([zpad, z_scr[rs, c0:c0 + KV_W]], axis=0).T

    ngroups = -(-nseq // SEQ_GROUP)
    gate_w = N_BRANCH * D_MODEL // ngroups
    gate_parts = []
    for g0 in range(0, nseq, SEQ_GROUP):
        seqs = list(range(g0, min(g0 + SEQ_GROUP, nseq)))
        rsl = {b: slice(b * t, (b + 1) * t) for b in seqs}

        for b in seqs:
            rs = rsl[b]
            qd = z_scr[rs, 0:256] * qdec_ref[...]
            qm = jnp.concatenate([jnp.where(hm, qd, 0.0) for hm in head_masks], axis=0).astype(BF16)
            s_old = sret_ref[b]
            oc = _dot(qm, s_old.astype(BF16))
            for hd in range(RET_HEADS):
                oc_scr[hd, rs, :] = oc[hd * t:(hd + 1) * t]
            kdt = (z_scr[rs, 256:512] * kdec_ref[...]).T.astype(BF16)
            vb = z_scr[rs, 512:1024].astype(BF16)
            for hd in range(RET_HEADS):
                hr = slice(hd * RET_DK, (hd + 1) * RET_DK)
                s_new = s_old[hr] * cdec_ref[hr, :] + _dot(kdt[hr], vb[:, hd * RET_DV:(hd + 1) * RET_DV])
                for l in slabs:
                    ret_ref[l, b, hr, :] = s_new

        s_raw, nkt = {}, {}
        for b in seqs:
            rs = rsl[b]
            aq = z_scr[rs, 1536:2048] * (ATT_DH ** -0.5)
            pieces = []
            for hd in range(ATT_HEADS):
                piece = aq[:, (hd // 2) * 128:(hd // 2 + 1) * 128]
                if hd % 2 != hd // ATT_GROUP:
                    piece = pltpu.roll(piece, ATT_DH, 1)
                pieces.append(jnp.where((lane128 // ATT_DH) == hd // ATT_GROUP, piece, 0.0))
            qs = jnp.concatenate(pieces, axis=0).astype(BF16)
            nkt[b] = new_rows_t(rs, 2048)
            k2t = jnp.concatenate([kt_ref[b], nkt[b]], axis=1).astype(BF16)
            s_raw[b] = _dot(qs, k2t)

        c0 = OFF_GATE + (g0 // SEQ_GROUP) * gate_w
        gate_parts.append(jax.nn.sigmoid(_dot(h, win_ref[:, c0:c0 + gate_w])))

        probs = {}
        for b in seqs:
            s = s_raw[b] + bias_ref[...]
            m = jnp.maximum(jnp.max(s, axis=-1, keepdims=True), snk)
            p = jnp.exp(s - m)
            rden = 1.0 / (jnp.sum(p, axis=-1, keepdims=True) + jnp.exp(snk - m))
            probs[b] = (p * rden).astype(BF16)

        for b in seqs:
            rs = rsl[b]
            nvt = new_rows_t(rs, 2176)
            v2t = jnp.concatenate([vt_ref[b], nvt], axis=1).astype(BF16)
            o = _dot_nt(probs[b], v2t)
            for pr in range(ATT_HEADS // 2):
                lo = o[(2 * pr) * t:(2 * pr + 1) * t]
                hi = o[(2 * pr + 1) * t:(2 * pr + 2) * t]
                if (2 * pr) // ATT_GROUP == 1:
                    lo = pltpu.roll(lo, ATT_DH, 1)
                if (2 * pr + 1) // ATT_GROUP == 0:
                    hi = pltpu.roll(hi, ATT_DH, 1)
                att_scr[rs, pr * 128:(pr + 1) * 128] = jnp.where(lane128 < ATT_DH, lo, hi)
            k_new = jnp.where(new_lanes, nkt[b], pltpu.roll(kt_ref[b], BLK - t, 1))
            v_new = jnp.where(new_lanes, nvt, pltpu.roll(vt_ref[b], BLK - t, 1))
            for l in slabs:
                wk_ref[l, b] = k_new
                wv_ref[l, b] = v_new

        for b in seqs:
            rs = rsl[b]
            u = z_scr[rs, 2816:3328] * z_scr[rs, 3328:3840]
            u_scr[b, 6:8, :] = sconv_ref[b]
            u_scr[b, 8:8 + t, :] = u
            y = (u_scr[b, 6:6 + t, :] * convw_ref[0:1, :] + u_scr[b, 7:7 + t, :] * convw_ref[1:2, :]
                 + u * convw_ref[2:3, :])
            cv_scr[rs, :] = z_scr[rs, 2304:2816] * y
            for l in slabs:
                conv_ref[l, b] = u_scr[b, 6 + t:8 + t, :]

    qb = z_scr[:, 0:256].astype(BF16)
    kb = z_scr[:, 256:512].astype(BF16)
    for hd in range(RET_HEADS):
        inner = _dot_nt(jnp.where(head_masks[hd], qb, zero_bf), kb) * dmask_ref[hd]
        vh = z_scr[:, 512 + hd * RET_DV:512 + (hd + 1) * RET_DV].astype(BF16)
        o = _dot(inner.astype(BF16), vh) + oc_scr[hd]
        o = o * lax.rsqrt(jnp.mean(o * o, axis=-1, keepdims=True) + EPS)
        g = z_scr[:, 1024 + hd * RET_DV:1024 + (hd + 1) * RET_DV]
        br_scr[:, hd * RET_DV:(hd + 1) * RET_DV] = (g * jax.nn.sigmoid(g) * o).astype(BF16)
    gates = jnp.concatenate(gate_parts, axis=1)
    mixed = (gates[:, 0:D_MODEL] * _dot(br_scr[...], wbr_ref[0])
             + gates[:, D_MODEL:2 * D_MODEL] * _dot(att_scr[...].astype(BF16), wbr_ref[1])
             + gates[:, 2 * D_MODEL:3 * D_MODEL] * _dot(cv_scr[...].astype(BF16), wbr_ref[2]))
    o_ref[...] = x + _rms(_dot(mixed.astype(BF16), wout_ref[...]), _gain_row(gpost_ref, layer))


def _mix_sample_call(layer, x2, t, sinkrow, sret, kt, vt, sconv, gpre, win, convw, wbr, wout, gpost, prev):
    depth, nb = sret.shape[0], sret.shape[1]
    sb = min(SAMPLE_SEQS, nb)
    rows = sb * t
    dm, qd, kd, cdec = _retention_consts(t)
    same_seq = np.kron(np.eye(sb), np.ones((t, t)))
    dmask = jnp.asarray(np.tile(dm, (1, sb, sb)) * same_seq[None], F32)
    qdec = jnp.asarray(qd, F32)
    kdec = jnp.asarray(kd, F32)
    cdec = jnp.asarray(cdec, F32)
    bias = jnp.asarray(_sample_attn_bias(t), F32)
    row = lambda i: (i, 0)
    seq = lambda i: (layer, i, 0, 0)
    state_shapes = [(RET_W, RET_DV), (KV_W, WINDOW), (KV_W, WINDOW), (CONV_W - 1, CONV_DIM)]
    state_specs = [pl.BlockSpec((None, sb) + s, seq) for s in state_shapes]
    if prev is None:
        out_state_specs = [pl.BlockSpec((depth, sb) + s, lambda i: (0, i, 0, 0)) for s in state_shapes]
    else:
        out_state_specs = [pl.BlockSpec((1, sb) + s, seq) for s in state_shapes]
    n_alias = 0 if prev is None else N_SAMPLE_STATE
    alias_args = [] if prev is None else list(prev)
    return pl.pallas_call(
        functools.partial(_mix_sample_body, layer, n_alias),
        grid=(nb // sb,),
        in_specs=[pl.BlockSpec((rows, D_MODEL), row)] + state_specs + [
            _const_spec(gpre.shape),
            _const_spec((D_MODEL, N_IN)),
            _layer_spec(layer, (CONV_W, CONV_DIM)),
            _const_spec(bias.shape),
            _layer_spec(layer, (ATT_HEADS * t, 128)),
            _const_spec(dmask.shape),
            _const_spec(qdec.shape),
            _const_spec(kdec.shape),
            _const_spec(cdec.shape),
            _const_spec((N_BRANCH, BRANCH_W, D_MODEL)),
            _const_spec((D_MODEL, D_MODEL)),
            _const_spec(gpost.shape),
        ] + [pl.BlockSpec(memory_space=pl.ANY)] * n_alias,
        out_specs=[pl.BlockSpec((rows, D_MODEL), row)] + out_state_specs,
        out_shape=[jax.ShapeDtypeStruct((nb * t, D_MODEL), F32)]
        + [jax.ShapeDtypeStruct((depth, nb) + s, F32) for s in state_shapes],
        input_output_aliases={N_SAMPLE_IN + k: 1 + k for k in range(n_alias)},
        scratch_shapes=[
            pltpu.VMEM((rows, OFF_GATE), F32),
            pltpu.VMEM((RET_HEADS, rows, RET_DV), F32),
            pltpu.VMEM((rows, BRANCH_W), F32),
            pltpu.VMEM((rows, BRANCH_W), F32),
            pltpu.VMEM((rows, BRANCH_W), BF16),
            pltpu.VMEM((sb, 8 + t, CONV_DIM), F32),
        ],
        compiler_params=pltpu.CompilerParams(
            dimension_semantics=("arbitrary",), vmem_limit_bytes=VMEM_LIMIT),
        name="mix_sample",
    )(x2, sret, kt, vt, sconv, gpre, win, convw, bias, sinkrow, dmask, qdec, kdec, cdec, wbr, wout, gpost,
      *alias_args)


def kernel(x_prompt, x_sample, p_prompt, p_sample, state_ret, cache_win_k, cache_win_v, state_conv,
           g_mix_pre, w_in, conv_w, attn_sinks, w_branch, w_out, g_mix_post,
           g_ffn_pre, w_ff1, w_ff2, g_ffn_post, g_ple, w_ple_gate, w_ple_proj):
    depth = w_in.shape[0]
    bp, tp, d = x_prompt.shape
    bs, ts, _ = x_sample.shape
    win_w = cache_win_k.shape[2]
    assert d == D_MODEL and w_in.shape[2] == N_IN and tp % BLK == 0 and math.gcd(ts, BLK) == ts
    assert win_w == WINDOW == BLK and ts <= BLK

    gpre, gpost = g_mix_pre, g_mix_post
    gfpre, gfpost, gple = g_ffn_pre, g_ffn_post, g_ple
    rows2d = lambda a: a.reshape(-1, a.shape[-1])
    mix_w = (w_in[0].astype(BF16), w_branch[0].astype(BF16), w_out[0].astype(BF16))
    wp_all = w_ple_proj.astype(BF16)
    pp = p_prompt.reshape(depth, bp * tp, D_PLE)
    ps = p_sample.reshape(depth, bs * ts, D_PLE)
    sret = state_ret.reshape(depth, bs, RET_W, RET_DV)
    kt = jnp.transpose(cache_win_k, (0, 1, 3, 4, 2)).reshape(depth, bs, KV_W, win_w)
    vt = jnp.transpose(cache_win_v, (0, 1, 3, 4, 2)).reshape(depth, bs, KV_W, win_w)
    sinkrow = jnp.broadcast_to(jnp.repeat(attn_sinks, ts, axis=1)[:, :, None], (depth, ATT_HEADS * ts, 128))

    yp = x_prompt
    ys = x_sample.reshape(bs * ts, d)
    outs_p, sample_states = [], None
    for l in range(depth):
        win, wbr, wout = mix_w
        ffn_jobs = [(rows2d(w_ff1), l, D_MODEL), (rows2d(w_ff2), l, D_FF), (rows2d(w_ple_gate), l, D_MODEL)]
        yp, r, k, v, c, w1, w2, wg = _mix_prompt_call(l, yp, attn_sinks, gpre, win, conv_w, wbr, wout, gpost,
                                                      ffn_jobs)
        outs_p.append((r, k, v, c))
        ffn_w = (gfpre, w1, w2, gfpost, gple, wg, wp_all[l])
        next_jobs = [] if l + 1 == depth else [(rows2d(w_in), l + 1, D_MODEL),
                                               (rows2d(w_branch), l + 1, N_BRANCH * BRANCH_W),
                                               (rows2d(w_out), l + 1, D_MODEL)]
        yp, *next_w = _ffn_call(l, yp.reshape(bp * tp, d), pp, *ffn_w, next_jobs)
        yp = yp.reshape(bp, tp, d)

        ys, *sample_states = _mix_sample_call(l, ys, ts, sinkrow, sret, kt, vt, state_conv, gpre, win, conv_w,
                                              wbr, wout, gpost, sample_states)
        ys, = _ffn_call(l, ys, ps, *ffn_w)
        if next_w:
            mix_w = (next_w[0], next_w[1].reshape(N_BRANCH, BRANCH_W, D_MODEL), next_w[2])

    def stack(i, shape):
        return jnp.stack([o[i] for o in outs_p]).reshape(shape)

    def untranspose(a):
        return jnp.transpose(a.reshape(depth, bs, ATT_KV_HEADS, ATT_DH, win_w), (0, 1, 4, 2, 3))

    rs, kts, vts, cs = sample_states
    return (yp, ys.reshape(bs, ts, d),
            stack(0, (depth, bp, RET_HEADS, RET_DK, RET_DV)),
            stack(1, (depth, bp, WINDOW, ATT_KV_HEADS, ATT_DH)),
            stack(2, (depth, bp, WINDOW, ATT_KV_HEADS, ATT_DH)),
            stack(3, (depth, bp, CONV_W - 1, CONV_DIM)),
            rs.reshape(depth, bs, RET_HEADS, RET_DK, RET_DV), untranspose(kts), untranspose(vts), cs)
```

```python
import functools
import math

import numpy as np
import jax
import jax.numpy as jnp
from jax import lax
from jax.experimental import pallas as pl
from jax.experimental.pallas import tpu as pltpu

F32 = jnp.float32
BF16 = jnp.bfloat16

D_MODEL = 1024
BRANCH_W = 512
N_BRANCH = 3
RET_HEADS = 4
RET_DK = 64
RET_DV = 128
RET_W = RET_HEADS * RET_DK
ATT_HEADS = 8
ATT_KV_HEADS = 2
ATT_GROUP = ATT_HEADS // ATT_KV_HEADS
ATT_DH = 64
KV_W = ATT_KV_HEADS * ATT_DH
WINDOW = 128
CONV_DIM = 512
CONV_W = 3
D_FF = 4096
D_PLE = 256
EPS = 1e-6

OFF_RET = 0
OFF_ATT = 1536
OFF_CONV = 2304
OFF_GATE = 3840
N_IN = 6912

BLK = 128
BF16_ROWS = 16
LOG2E = math.log2(math.e)
NEG = -1e30
VMEM_LIMIT = 56 * 1024 * 1024

PROMPT_TILE = 512
ROW_SPLIT = 2
FFN_TILE = 1024
FFN_ROW_GROUP = 128
FFN_CHUNK = 1024
SAMPLE_SEQS = 16
SEQ_GROUP = 16


def _dot(a, b):
    return jnp.dot(a, b, preferred_element_type=F32)


def _dot_nt(a, b):
    return lax.dot_general(a, b, (((1,), (1,)), ((), ())), preferred_element_type=F32)


def _rms(x, g):
    return x * lax.rsqrt(jnp.mean(x * x, axis=-1, keepdims=True) + EPS) * g


def _gain_row(g_ref, layer):
    return g_ref[layer:layer + 1, :]


def _const_spec(shape):
    nd = len(shape)
    return pl.BlockSpec(shape, lambda *_: (0,) * nd, pipeline_mode=pl.Buffered(1))


def _layer_spec(layer, shape):
    nd = len(shape)
    return pl.BlockSpec((None,) + tuple(shape), lambda *_: (layer,) + (0,) * nd, pipeline_mode=pl.Buffered(1))


def _cast_specs(jobs, nsteps, step_of):
    in_specs, out_specs, out_shapes = [], [], []
    for src, layer, nrows in jobs:
        slab = nrows // nsteps
        assert nrows % nsteps == 0 and slab % BF16_ROWS == 0, (nrows, nsteps)
        cols = src.shape[1]
        in_specs.append(pl.BlockSpec((slab, cols), lambda *g, base=layer * nsteps: (base + step_of(*g), 0)))
        out_specs.append(pl.BlockSpec((slab, cols), lambda *g: (step_of(*g), 0)))
        out_shapes.append(jax.ShapeDtypeStruct((nrows, cols), BF16))
    return in_specs, out_specs, out_shapes


def _cast_slabs(src_refs, dst_refs):
    for src, dst in zip(src_refs, dst_refs):
        dst[...] = src[...].astype(BF16)


def _log_gamma():
    return np.log1p(-np.exp2(-5.0 - np.arange(RET_HEADS, dtype=np.float64)))


def _retention_consts(c):
    lg = _log_gamma()
    idx = np.arange(c, dtype=np.float64)
    diff = idx[:, None] - idx[None, :]
    dmask = np.where(diff >= 0, np.exp(lg[:, None, None] * np.maximum(diff, 0.0)), 0.0) * RET_DK ** -0.5
    qdec = np.repeat(np.exp(lg[:, None] * (idx + 1.0)).T, RET_DK, axis=1)
    kdec = np.repeat(np.exp(lg[:, None] * (c - 1.0 - idx)).T, RET_DK, axis=1) * RET_DK ** -0.5
    cdec = np.broadcast_to(np.repeat(np.exp(lg * c), RET_DK)[:, None], (RET_W, RET_DV))
    return dmask, qdec, kdec, cdec


def _slopes():
    return np.exp2(-8.0 * (np.arange(ATT_HEADS, dtype=np.float64) + 1.0) / ATT_HEADS)


def _prompt_attn_bias():
    s = np.arange(2 * BLK)[:, None]
    q = np.arange(BLK)[None, :]
    dist = q + BLK - s
    allowed = (dist >= 0) & (dist < WINDOW)
    per_head = np.where(allowed[None], -_slopes()[:, None, None] * dist[None] * LOG2E, NEG)
    out = [np.concatenate([per_head[ATT_GROUP * j + par], per_head[ATT_GROUP * j + par + 2]], axis=1)
           for j in range(ATT_KV_HEADS) for par in range(2)]
    return np.stack(out)


def _sample_attn_bias(t):
    i = np.arange(t)[:, None]
    col = np.arange(2 * BLK)[None, :]
    new = col >= 2 * BLK - t
    dist = np.where(new, i - (col - (2 * BLK - t)), i + WINDOW - col)
    allowed = (dist >= 0) & (dist < WINDOW) & ((col < WINDOW) | new)
    b = np.where(allowed[None], -_slopes()[:, None, None] * dist[None], NEG)
    return b.reshape(ATT_HEADS * t, 2 * BLK)


N_FFN_IN = 9


def _ffn_body(layer, n_cast, *refs):
    x_ref, p_ref, gpre_ref, w1_ref, w2_ref, gpost_ref, gple_ref, wg_ref, wp_ref = refs[:N_FFN_IN]
    o_ref = refs[N_FFN_IN + n_cast]
    _cast_slabs(refs[N_FFN_IN:N_FFN_IN + n_cast], refs[N_FFN_IN + n_cast + 1:])
    tile = x_ref.shape[0]
    nchunk = D_FF // FFN_CHUNK
    rows = [slice(r0, r0 + FFN_ROW_GROUP) for r0 in range(0, tile, FFN_ROW_GROUP)]

    def up(hh, c):
        a = _dot(hh, w1_ref[:, c * FFN_CHUNK:(c + 1) * FFN_CHUNK])
        return jnp.square(jnp.maximum(a, 0.0)).astype(BF16)

    def down(a, c):
        return _dot(a, w2_ref[c * FFN_CHUNK:(c + 1) * FFN_CHUNK, :])

    gpre, gpost, gple = (_gain_row(g, layer) for g in (gpre_ref, gpost_ref, gple_ref))
    h_parts = [_rms(x_ref[rw, :], gpre).astype(BF16) for rw in rows]
    a0 = jnp.concatenate([up(hp, 0) for hp in h_parts], axis=0)
    h = jnp.concatenate(h_parts, axis=0)
    acc = down(a0, 0)
    for c in range(1, nchunk - 1):
        acc = acc + down(up(h, c), c)
    a_last = up(h, nchunk - 1)
    f_parts = [acc[rw] + down(a_last[rw], nchunk - 1) for rw in rows]
    pp = _dot(p_ref[...].astype(BF16), wp_ref[...])
    for rw, f in zip(rows, f_parts):
        x2 = x_ref[rw, :] + _rms(f, gpost)
        gate = jax.nn.sigmoid(_dot(_rms(x2, gple).astype(BF16), wg_ref[...]))
        o_ref[rw, :] = x2 + gate * pp[rw]


def _ffn_call(layer, x, p, gpre, w1, w2, gpost, gple, wg, wp, cast_jobs=()):
    n = x.shape[0]
    tile = min(FFN_TILE, n)
    row = lambda i: (i, 0)
    cast_in, cast_out, cast_shapes = _cast_specs(cast_jobs, n // tile, lambda i: i)
    return pl.pallas_call(
        functools.partial(_ffn_body, layer, len(cast_jobs)),
        grid=(n // tile,),
        in_specs=[
            pl.BlockSpec((tile, D_MODEL), row),
            pl.BlockSpec((None, tile, D_PLE), lambda i: (layer, i, 0)),
            _const_spec(gpre.shape),
            _const_spec((D_MODEL, D_FF)),
            _const_spec((D_FF, D_MODEL)),
            _const_spec(gpost.shape),
            _const_spec(gple.shape),
            _const_spec((D_MODEL, D_MODEL)),
            _const_spec((D_PLE, D_MODEL)),
        ] + cast_in,
        out_specs=[pl.BlockSpec((tile, D_MODEL), row)] + cast_out,
        out_shape=[jax.ShapeDtypeStruct((n, D_MODEL), F32)] + cast_shapes,
        compiler_params=pltpu.CompilerParams(
            dimension_semantics=("arbitrary",), vmem_limit_bytes=VMEM_LIMIT),
        name="ffn",
    )(x, p, gpre, w1, w2, gpost, gple, wg, wp, *[j[0] for j in cast_jobs])


N_PROMPT_IN = 13


def _mix_prompt_body(layer, n_cast, *refs):
    (sink_ref, x_ref, gpre_ref, win_ref, convw_ref, bias_ref, dmask_ref, qdec_ref, kdec_ref, cdec_ref,
     wbr_ref, wout_ref, gpost_ref) = refs[:N_PROMPT_IN]
    o_ref, ret_ref, wk_ref, wv_ref, conv_ref = refs[N_PROMPT_IN + n_cast:N_PROMPT_IN + n_cast + 5]
    br_scr, att_scr, u_scr = refs[N_PROMPT_IN + 2 * n_cast + 5:]
    _cast_slabs(refs[N_PROMPT_IN:N_PROMPT_IN + n_cast],
                refs[N_PROMPT_IN + n_cast + 5:N_PROMPT_IN + 2 * n_cast + 5])
    tt = x_ref.shape[1]
    nblk = tt // BLK
    first = pl.program_id(1) == 0

    @pl.when(first)
    def _init():
        ret_ref[...] = jnp.zeros_like(ret_ref)
        wk_ref[...] = jnp.zeros_like(wk_ref)
        wv_ref[...] = jnp.zeros_like(wv_ref)
        u_scr[0:8, :] = jnp.zeros((8, CONV_DIM), F32)

    lane256 = lax.broadcasted_iota(jnp.int32, (1, 256), 1)
    zero_bf = jnp.zeros((), BF16)

    rows = [slice(i * tt // ROW_SPLIT, (i + 1) * tt // ROW_SPLIT) for i in range(ROW_SPLIT)]
    h_parts = [_rms(x_ref[0, rw, :], _gain_row(gpre_ref, layer)).astype(BF16) for rw in rows]
    zr = jnp.concatenate([_dot(hp, win_ref[:, OFF_RET:OFF_RET + 1536]) for hp in h_parts], axis=0)
    h = jnp.concatenate(h_parts, axis=0)
    za = _dot(h, win_ref[:, OFF_ATT:OFF_ATT + 768])

    head_masks = [(lane256 // RET_DK) == hd for hd in range(RET_HEADS)]

    def head_stack(a):
        return jnp.concatenate([jnp.where(hm, a, zero_bf) for hm in head_masks], axis=0)

    def values(c, hd):
        return zr[c * BLK:(c + 1) * BLK, 512 + hd * RET_DV:512 + (hd + 1) * RET_DV].astype(BF16)

    kts, incs = [], []
    for c in range(nblk):
        kt = zr[c * BLK:(c + 1) * BLK, 256:512].T
        kdt = (kt * kdec_ref[...]).astype(BF16)
        kts.append(kt.astype(BF16))
        incs.append(jnp.concatenate(
            [_dot(kdt[hd * RET_DK:(hd + 1) * RET_DK], values(c, hd)) for hd in range(RET_HEADS)], axis=0))

    def scores_and_cross(c, state):
        q_st = head_stack(zr[c * BLK:(c + 1) * BLK, 0:256].astype(BF16))
        return _dot(q_st, jnp.concatenate([kts[c], state.astype(BF16)], axis=1))

    state = ret_ref[0]
    fused_next = scores_and_cross(0, state)

    low = lax.broadcasted_iota(jnp.int32, (1, 128), 1) < ATT_DH
    key_row = lax.broadcasted_iota(jnp.int32, (2 * BLK, 1), 0)
    no_prev = jnp.where(key_row < BLK, jnp.where(first, NEG, 0.0), 0.0).astype(F32)
    sink_rows = [jnp.where(lane256 < BLK, sink_ref[layer, ATT_GROUP * j + par] * LOG2E,
                           sink_ref[layer, ATT_GROUP * j + par + 2] * LOG2E)
                 for j in range(ATT_KV_HEADS) for par in range(2)]
    top = lax.broadcasted_iota(jnp.int32, (KV_W, 1), 0) < ATT_DH

    def value_variants(vblk):
        vt = vblk.T
        return [jnp.where(top, vt, 1.0).astype(BF16), jnp.where(top, 1.0, vt).astype(BF16)]

    def key_variants(kblk):
        rolled = pltpu.roll(kblk, ATT_DH, 1)
        return [jnp.where(low, kblk, 0.0).astype(BF16), jnp.where(low, 0.0, rolled).astype(BF16),
                jnp.where(low, rolled, 0.0).astype(BF16), jnp.where(low, 0.0, kblk).astype(BF16)]

    blocks = [slice(i * BLK, (i + 1) * BLK) for i in range(nblk)]
    att_q = [(za[r, 0:512] * (ATT_DH ** -0.5 * LOG2E)).astype(BF16) for r in blocks]
    att_k = [key_variants(wk_ref[0])] + [key_variants(za[r, 512:640]) for r in blocks]
    att_vt = [value_variants(wv_ref[0])] + [value_variants(za[r, 640:768]) for r in blocks]

    def gate_slice(n, c, width):
        c0 = OFF_GATE + n * D_MODEL + c * width
        return jax.nn.sigmoid(_dot(h, win_ref[:, c0:c0 + width]))

    z_ch = _dot(h, win_ref[:, OFF_CONV + CONV_DIM:OFF_CONV + 3 * CONV_DIM])
    u_scr[8:8 + tt, :] = z_ch[:, 0:CONV_DIM] * z_ch[:, CONV_DIM:2 * CONV_DIM]
    taps = (u_scr[6:6 + tt, :] * convw_ref[0:1, :] + u_scr[7:7 + tt, :] * convw_ref[1:2, :]
            + u_scr[8:8 + tt, :] * convw_ref[2:3, :])
    z_b = _dot(h, win_ref[:, OFF_CONV:OFF_CONV + CONV_DIM])

    def conv_rows(c):
        r = slice(c * BLK, (c + 1) * BLK)
        return (z_b[r] * taps[r]).astype(BF16)

    gate_w = D_MODEL // nblk
    gate0, gate2, conv_out = [], [], []
    for c in range(nblk):
        r = slice(c * BLK, (c + 1) * BLK)
        fused = fused_next
        state = state * cdec_ref[...] + incs[c]
        gate0.append(gate_slice(0, c, gate_w))
        if c + 1 < nblk:
            fused_next = scores_and_cross(c + 1, state)
        gate2.append(gate_slice(2, c, gate_w))
        conv_out.append(conv_rows(c))
        inner = (fused[:, 0:BLK] * dmask_ref[...]).astype(BF16)
        cross = fused[:, BLK:2 * BLK] * qdec_ref[...]
        for hd in range(RET_HEADS):
            hq = slice(hd * BLK, (hd + 1) * BLK)
            o = _dot(inner[hq], values(c, hd)) + cross[hq]
            o = o * lax.rsqrt(jnp.mean(o * o, axis=-1, keepdims=True) + EPS)
            g = zr[r, 1024 + hd * RET_DV:1024 + (hd + 1) * RET_DV]
            br_scr[r, hd * RET_DV:(hd + 1) * RET_DV] = (g * jax.nn.sigmoid(g) * o).astype(BF16)
    ret_ref[0] = state
    tail = u_scr[6 + tt:8 + tt, :]
    conv_ref[0] = tail
    u_scr[6:8, :] = tail
    proj2 = _dot(jnp.concatenate(conv_out, axis=0), wbr_ref[2])
    mixed = (jnp.concatenate(gate0, axis=1) * _dot(br_scr[...], wbr_ref[0])
             + jnp.concatenate(gate2, axis=1) * proj2)


    def scores(i):
        aq = att_q[i]
        out = []
        for j in range(ATT_KV_HEADS):
            wq = jnp.concatenate([aq[:, j * 256:j * 256 + 128], aq[:, j * 256 + 128:j * 256 + 256]], axis=0)
            for par in range(2):
                v = 2 * j + par
                out.append(_dot_nt(jnp.concatenate([att_k[i][v], att_k[i + 1][v]], axis=0), wq))
        return out

    gate1 = []
    s_next = scores(0)
    for i in range(nblk):
        r = blocks[i]
        s_cur = s_next
        if i + 1 < nblk:
            s_next = scores(i + 1)
        gate1.append(gate_slice(1, i, gate_w))
        ps, sink_ps = [], []
        for v in range(2 * ATT_KV_HEADS):
            s = s_cur[v] + bias_ref[v]
            if i == 0:
                s = s + no_prev
            m = jnp.maximum(jnp.max(s, axis=0, keepdims=True), sink_rows[v])
            ps.append(jnp.exp2(s - m).astype(BF16))
            sink_ps.append(jnp.exp2(sink_rows[v] - m))
        heads = [None] * ATT_HEADS
        for j in range(ATT_KV_HEADS):
            vt2 = jnp.concatenate([att_vt[i][j], att_vt[i + 1][j]], axis=1)
            ones_row = (1 - j) * ATT_DH
            for par in range(2):
                v = 2 * j + par
                pv = _dot(vt2, ps[v])
                rden = 1.0 / (pv[ones_row:ones_row + 1] + sink_ps[v])
                ot = pv[j * ATT_DH:(j + 1) * ATT_DH] * rden
                heads[ATT_GROUP * j + par] = ot[:, 0:BLK]
                heads[ATT_GROUP * j + par + 2] = ot[:, BLK:2 * BLK]
        att_scr[r, :] = jnp.concatenate(heads, axis=0).T.astype(BF16)
    wk_ref[0] = za[tt - BLK:tt, 512:640]
    wv_ref[0] = za[tt - BLK:tt, 640:768]
    mixed = mixed + jnp.concatenate(gate1, axis=1) * _dot(att_scr[...], wbr_ref[1])

    mixed = mixed.astype(BF16)
    gpost = _gain_row(gpost_ref, layer)
    for rw in rows:
        o_ref[0, rw, :] = x_ref[0, rw, :] + _rms(_dot(mixed[rw], wout_ref[...]), gpost)


def _mix_prompt_call(layer, x, sinks, gpre, win, convw, wbr, wout, gpost, cast_jobs=()):
    b, t, _ = x.shape
    tt = min(PROMPT_TILE, t)
    nt = t // tt
    cast_in, cast_out, cast_shapes = _cast_specs(cast_jobs, b * nt, lambda bi, ti: bi * nt + ti)
    dmask, qdec, kdec, cdec = _retention_consts(BLK)
    dmask = jnp.asarray(dmask.reshape(RET_HEADS * BLK, BLK), F32)
    qdec = jnp.asarray(np.broadcast_to(qdec[:, ::RET_DK].T.reshape(RET_HEADS * BLK, 1), (RET_HEADS * BLK, BLK)), F32)
    kdec = jnp.asarray(kdec.T, F32)
    cdec = jnp.asarray(cdec, F32)
    bias = jnp.asarray(_prompt_attn_bias(), F32)
    per_b = lambda bi, ti: (bi, 0, 0)
    return pl.pallas_call(
        functools.partial(_mix_prompt_body, layer, len(cast_jobs)),
        grid=(b, nt),
        in_specs=[
            pl.BlockSpec(memory_space=pltpu.SMEM),
            pl.BlockSpec((1, tt, D_MODEL), lambda bi, ti: (bi, ti, 0)),
            _const_spec(gpre.shape),
            _const_spec((D_MODEL, N_IN)),
            _layer_spec(layer, (CONV_W, CONV_DIM)),
            _const_spec(bias.shape),
            _const_spec(dmask.shape),
            _const_spec(qdec.shape),
            _const_spec(kdec.shape),
            _const_spec(cdec.shape),
            _const_spec((N_BRANCH, BRANCH_W, D_MODEL)),
            _const_spec((D_MODEL, D_MODEL)),
            _const_spec(gpost.shape),
        ] + cast_in,
        out_specs=[
            pl.BlockSpec((1, tt, D_MODEL), lambda bi, ti: (bi, ti, 0)),
            pl.BlockSpec((1, RET_W, RET_DV), per_b),
            pl.BlockSpec((1, WINDOW, KV_W), per_b),
            pl.BlockSpec((1, WINDOW, KV_W), per_b),
            pl.BlockSpec((1, CONV_W - 1, CONV_DIM), per_b),
        ] + cast_out,
        out_shape=[
            jax.ShapeDtypeStruct((b, t, D_MODEL), F32),
            jax.ShapeDtypeStruct((b, RET_W, RET_DV), F32),
            jax.ShapeDtypeStruct((b, WINDOW, KV_W), F32),
            jax.ShapeDtypeStruct((b, WINDOW, KV_W), F32),
            jax.ShapeDtypeStruct((b, CONV_W - 1, CONV_DIM), F32),
        ] + cast_shapes,
        scratch_shapes=[
            pltpu.VMEM((tt, BRANCH_W), BF16),
            pltpu.VMEM((tt, BRANCH_W), BF16),
            pltpu.VMEM((tt + 8, CONV_DIM), F32),
        ],
        compiler_params=pltpu.CompilerParams(
            dimension_semantics=("arbitrary", "arbitrary"), vmem_limit_bytes=VMEM_LIMIT),
        name="mix_prompt",
    )(sinks, x, gpre, win, convw, bias, dmask, qdec, kdec, cdec, wbr, wout, gpost, *[j[0] for j in cast_jobs])


N_SAMPLE_IN = 17
N_SAMPLE_STATE = 4


def _mix_sample_body(layer, n_alias, *refs):
    (x_ref, sret_ref, kt_ref, vt_ref, sconv_ref, gpre_ref, win_ref, convw_ref, bias_ref, sinkrow_ref,
     dmask_ref, qdec_ref, kdec_ref, cdec_ref, wbr_ref, wout_ref, gpost_ref) = refs[:N_SAMPLE_IN]
    o_ref, ret_ref, wk_ref, wv_ref, conv_ref = refs[N_SAMPLE_IN + n_alias:N_SAMPLE_IN + n_alias + 5]
    z_scr, oc_scr, att_scr, cv_scr, br_scr, u_scr = refs[N_SAMPLE_IN + n_alias + 5:]
    rows = x_ref.shape[0]
    nseq = sret_ref.shape[0]
    slabs = range(ret_ref.shape[0])
    t = rows // nseq

    x = x_ref[...]
    h = _rms(x, _gain_row(gpre_ref, layer)).astype(BF16)
    lane256 = lax.broadcasted_iota(jnp.int32, (1, 256), 1)
    lane128 = lax.broadcasted_iota(jnp.int32, (1, 128), 1)
    zero_bf = jnp.zeros((), BF16)
    head_masks = [(lane256 // RET_DK) == hd for hd in range(RET_HEADS)]
    new_lanes = lane128 >= BLK - t
    zpad = jnp.zeros((BLK - t, KV_W), F32)
    snk = sinkrow_ref[:, 0:1]

    z_scr[...] = _dot(h, win_ref[:, 0:OFF_GATE])

    def new_rows_t(rs, c0):
        return jnp.concatenate([zpad, z_scr[rs, c0:c0 + KV_W]], axis=0).T

    ngroups = -(-nseq // SEQ_GROUP)
    gate_w = N_BRANCH * D_MODEL // ngroups
    gate_parts = []
    for g0 in range(0, nseq, SEQ_GROUP):
        seqs = list(range(g0, min(g0 + SEQ_GROUP, nseq)))
        rsl = {b: slice(b * t, (b + 1) * t) for b in seqs}

        for b in seqs:
            rs = rsl[b]
            qd = z_scr[rs, 0:256] * qdec_ref[...]
            qm = jnp.concatenate([jnp.where(hm, qd, 0.0) for hm in head_masks], axis=0).astype(BF16)
            s_old = sret_ref[b]
            oc = _dot(qm, s_old.astype(BF16))
            for hd in range(RET_HEADS):
                oc_scr[hd, rs, :] = oc[hd * t:(hd + 1) * t]
            kdt = (z_scr[rs, 256:512] * kdec_ref[...]).T.astype(BF16)
            vb = z_scr[rs, 512:1024].astype(BF16)
            for hd in range(RET_HEADS):
                hr = slice(hd * RET_DK, (hd + 1) * RET_DK)
                s_new = s_old[hr] * cdec_ref[hr, :] + _dot(kdt[hr], vb[:, hd * RET_DV:(hd + 1) * RET_DV])
                for l in slabs:
                    ret_ref[l, b, hr, :] = s_new

        s_raw, nkt = {}, {}
        for b in seqs:
            rs = rsl[b]
            aq = z_scr[rs, 1536:2048] * (ATT_DH ** -0.5)
            pieces = []
            for hd in range(ATT_HEADS):
                piece = aq[:, (hd // 2) * 128:(hd // 2 + 1) * 128]
                if hd % 2 != hd // ATT_GROUP:
                    piece = pltpu.roll(piece, ATT_DH, 1)
                pieces.append(jnp.where((lane128 // ATT_DH) == hd // ATT_GROUP, piece, 0.0))
            qs = jnp.concatenate(pieces, axis=0).astype(BF16)
            nkt[b] = new_rows_t(rs, 2048)
            k2t = jnp.concatenate([kt_ref[b], nkt[b]], axis=1).astype(BF16)
            s_raw[b] = _dot(qs, k2t)

        c0 = OFF_GATE + (g0 // SEQ_GROUP) * gate_w
        gate_parts.append(jax.nn.sigmoid(_dot(h, win_ref[:, c0:c0 + gate_w])))

        probs = {}
        for b in seqs:
            s = s_raw[b] + bias_ref[...]
            m = jnp.maximum(jnp.max(s, axis=-1, keepdims=True), snk)
            p = jnp.exp(s - m)
            rden = 1.0 / (jnp.sum(p, axis=-1, keepdims=True) + jnp.exp(snk - m))
            probs[b] = (p * rden).astype(BF16)

        for b in seqs:
            rs = rsl[b]
            nvt = new_rows_t(rs, 2176)
            v2t = jnp.concatenate([vt_ref[b], nvt], axis=1).astype(BF16)
            o = _dot_nt(probs[b], v2t)
            for pr in range(ATT_HEADS // 2):
                lo = o[(2 * pr) * t:(2 * pr + 1) * t]
                hi = o[(2 * pr + 1) * t:(2 * pr + 2) * t]
                if (2 * pr) // ATT_GROUP == 1:
                    lo = pltpu.roll(lo, ATT_DH, 1)
                if (2 * pr + 1) // ATT_GROUP == 0:
                    hi = pltpu.roll(hi, ATT_DH, 1)
                att_scr[rs, pr * 128:(pr + 1) * 128] = jnp.where(lane128 < ATT_DH, lo, hi)
            k_new = jnp.where(new_lanes, nkt[b], pltpu.roll(kt_ref[b], BLK - t, 1))
            v_new = jnp.where(new_lanes, nvt, pltpu.roll(vt_ref[b], BLK - t, 1))
            for l in slabs:
                wk_ref[l, b] = k_new
                wv_ref[l, b] = v_new

        for b in seqs:
            rs = rsl[b]
            u = z_scr[rs, 2816:3328] * z_scr[rs, 3328:3840]
            u_scr[b, 6:8, :] = sconv_ref[b]
            u_scr[b, 8:8 + t, :] = u
            y = (u_scr[b, 6:6 + t, :] * convw_ref[0:1, :] + u_scr[b, 7:7 + t, :] * convw_ref[1:2, :]
                 + u * convw_ref[2:3, :])
            cv_scr[rs, :] = z_scr[rs, 2304:2816] * y
            for l in slabs:
                conv_ref[l, b] = u_scr[b, 6 + t:8 + t, :]

    qb = z_scr[:, 0:256].astype(BF16)
    kb = z_scr[:, 256:512].astype(BF16)
    for hd in range(RET_HEADS):
        inner = _dot_nt(jnp.where(head_masks[hd], qb, zero_bf), kb) * dmask_ref[hd]
        vh = z_scr[:, 512 + hd * RET_DV:512 + (hd + 1) * RET_DV].astype(BF16)
        o = _dot(inner.astype(BF16), vh) + oc_scr[hd]
        o = o * lax.rsqrt(jnp.mean(o * o, axis=-1, keepdims=True) + EPS)
        g = z_scr[:, 1024 + hd * RET_DV:1024 + (hd + 1) * RET_DV]
        br_scr[:, hd * RET_DV:(hd + 1) * RET_DV] = (g * jax.nn.sigmoid(g) * o).astype(BF16)
    gates = jnp.concatenate(gate_parts, axis=1)
    mixed = (gates[:, 0:D_MODEL] * _dot(br_scr[...], wbr_ref[0])
             + gates[:, D_MODEL:2 * D_MODEL] * _dot(att_scr[...].astype(BF16), wbr_ref[1])
             + gates[:, 2 * D_MODEL:3 * D_MODEL] * _dot(cv_scr[...].astype(BF16), wbr_ref[2]))
    o_ref[...] = x + _rms(_dot(mixed.astype(BF16), wout_ref[...]), _gain_row(gpost_ref, layer))


def _mix_sample_call(layer, x2, t, sinkrow, sret, kt, vt, sconv, gpre, win, convw, wbr, wout, gpost, prev):
    depth, nb = sret.shape[0], sret.shape[1]
    sb = min(SAMPLE_SEQS, nb)
    rows = sb * t
    dm, qd, kd, cdec = _retention_consts(t)
    same_seq = np.kron(np.eye(sb), np.ones((t, t)))
    dmask = jnp.asarray(np.tile(dm, (1, sb, sb)) * same_seq[None], F32)
    qdec = jnp.asarray(qd, F32)
    kdec = jnp.asarray(kd, F32)
    cdec = jnp.asarray(cdec, F32)
    bias = jnp.asarray(_sample_attn_bias(t), F32)
    row = lambda i: (i, 0)
    seq = lambda i: (layer, i, 0, 0)
    state_shapes = [(RET_W, RET_DV), (KV_W, WINDOW), (KV_W, WINDOW), (CONV_W - 1, CONV_DIM)]
    state_specs = [pl.BlockSpec((None, sb) + s, seq) for s in state_shapes]
    if prev is None:
        out_state_specs = [pl.BlockSpec((depth, sb) + s, lambda i: (0, i, 0, 0)) for s in state_shapes]
    else:
        out_state_specs = [pl.BlockSpec((1, sb) + s, seq) for s in state_shapes]
    n_alias = 0 if prev is None else N_SAMPLE_STATE
    alias_args = [] if prev is None else list(prev)
    return pl.pallas_call(
        functools.partial(_mix_sample_body, layer, n_alias),
        grid=(nb // sb,),
        in_specs=[pl.BlockSpec((rows, D_MODEL), row)] + state_specs + [
            _const_spec(gpre.shape),
            _const_spec((D_MODEL, N_IN)),
            _layer_spec(layer, (CONV_W, CONV_DIM)),
            _const_spec(bias.shape),
            _layer_spec(layer, (ATT_HEADS * t, 128)),
            _const_spec(dmask.shape),
            _const_spec(qdec.shape),
            _const_spec(kdec.shape),
            _const_spec(cdec.shape),
            _const_spec((N_BRANCH, BRANCH_W, D_MODEL)),
            _const_spec((D_MODEL, D_MODEL)),
            _const_spec(gpost.shape),
        ] + [pl.BlockSpec(memory_space=pl.ANY)] * n_alias,
        out_specs=[pl.BlockSpec((rows, D_MODEL), row)] + out_state_specs,
        out_shape=[jax.ShapeDtypeStruct((nb * t, D_MODEL), F32)]
        + [jax.ShapeDtypeStruct((depth, nb) + s, F32) for s in state_shapes],
        input_output_aliases={N_SAMPLE_IN + k: 1 + k for k in range(n_alias)},
        scratch_shapes=[
            pltpu.VMEM((rows, OFF_GATE), F32),
            pltpu.VMEM((RET_HEADS, rows, RET_DV), F32),
            pltpu.VMEM((rows, BRANCH_W), F32),
            pltpu.VMEM((rows, BRANCH_W), F32),
            pltpu.VMEM((rows, BRANCH_W), BF16),
            pltpu.VMEM((sb, 8 + t, CONV_DIM), F32),
        ],
        compiler_params=pltpu.CompilerParams(
            dimension_semantics=("arbitrary",), vmem_limit_bytes=VMEM_LIMIT),
        name="mix_sample",
    )(x2, sret, kt, vt, sconv, gpre, win, convw, bias, sinkrow, dmask, qdec, kdec, cdec, wbr, wout, gpost,
      *alias_args)


def kernel(x_prompt, x_sample, p_prompt, p_sample, state_ret, cache_win_k, cache_win_v, state_conv,
           g_mix_pre, w_in, conv_w, attn_sinks, w_branch, w_out, g_mix_post,
           g_ffn_pre, w_ff1, w_ff2, g_ffn_post, g_ple, w_ple_gate, w_ple_proj):
    depth = w_in.shape[0]
    bp, tp, d = x_prompt.shape
    bs, ts, _ = x_sample.shape
    win_w = cache_win_k.shape[2]
    assert d == D_MODEL and w_in.shape[2] == N_IN and tp % BLK == 0 and math.gcd(ts, BLK) == ts
    assert win_w == WINDOW == BLK and ts <= BLK

    gpre, gpost = g_mix_pre, g_mix_post
    gfpre, gfpost, gple = g_ffn_pre, g_ffn_post, g_ple
    rows2d = lambda a: a.reshape(-1, a.shape[-1])
    mix_w = (w_in[0].astype(BF16), w_branch[0].astype(BF16), w_out[0].astype(BF16))
    wp_all = w_ple_proj.astype(BF16)
    pp = p_prompt.reshape(depth, bp * tp, D_PLE)
    ps = p_sample.reshape(depth, bs * ts, D_PLE)
    sret = state_ret.reshape(depth, bs, RET_W, RET_DV)
    kt = jnp.transpose(cache_win_k, (0, 1, 3, 4, 2)).reshape(depth, bs, KV_W, win_w)
    vt = jnp.transpose(cache_win_v, (0, 1, 3, 4, 2)).reshape(depth, bs, KV_W, win_w)
    sinkrow = jnp.broadcast_to(jnp.repeat(attn_sinks, ts, axis=1)[:, :, None], (depth, ATT_HEADS * ts, 128))

    yp = x_prompt
    ys = x_sample.reshape(bs * ts, d)
    outs_p, sample_states = [], None
    for l in range(depth):
        win, wbr, wout = mix_w
        ffn_jobs = [(rows2d(w_ff1), l, D_MODEL), (rows2d(w_ff2), l, D_FF), (rows2d(w_ple_gate), l, D_MODEL)]
        yp, r, k, v, c, w1, w2, wg = _mix_prompt_call(l, yp, attn_sinks, gpre, win, conv_w, wbr, wout, gpost,
                                                      ffn_jobs)
        outs_p.append((r, k, v, c))
        ffn_w = (gfpre, w1, w2, gfpost, gple, wg, wp_all[l])
        next_jobs = [] if l + 1 == depth else [(rows2d(w_in), l + 1, D_MODEL),
                                               (rows2d(w_branch), l + 1, N_BRANCH * BRANCH_W),
                                               (rows2d(w_out), l + 1, D_MODEL)]
        yp, *next_w = _ffn_call(l, yp.reshape(bp * tp, d), pp, *ffn_w, next_jobs)
        yp = yp.reshape(bp, tp, d)

        ys, *sample_states = _mix_sample_call(l, ys, ts, sinkrow, sret, kt, vt, state_conv, gpre, win, conv_w,
                                              wbr, wout, gpost, sample_states)
        ys, = _ffn_call(l, ys, ps, *ffn_w)
        if next_w:
            mix_w = (next_w[0], next_w[1].reshape(N_BRANCH, BRANCH_W, D_MODEL), next_w[2])

    def stack(i, shape):
        return jnp.stack([o[i] for o in outs_p]).reshape(shape)

    def untranspose(a):
        return jnp.transpose(a.reshape(depth, bs, ATT_KV_HEADS, ATT_DH, win_w), (0, 1, 4, 2, 3))

    rs, kts, vts, cs = sample_states
    return (yp, ys.reshape(bs, ts, d),
            stack(0, (depth, bp, RET_HEADS, RET_DK, RET_DV)),
            stack(1, (depth, bp, WINDOW, ATT_KV_HEADS, ATT_DH)),
            stack(2, (depth, bp, WINDOW, ATT_KV_HEADS, ATT_DH)),
            stack(3, (depth, bp, CONV_W - 1, CONV_DIM)),
            rs.reshape(depth, bs, RET_HEADS, RET_DK, RET_DV), untranspose(kts), untranspose(vts), cs)
```
